```python
import math
import jax, jax.numpy as jnp
from jax import lax
import numpy as np

D_MODEL = 1024
BATCH = 32
SEQ = 256
DEPTH = 2
DEC_BATCH = 4
DEC_SEQ = 1024
PAST_LEN = 512

GRID_W = 64
D_MIX = D_MODEL
D_RWKV = D_MIX // 2
HEAD_A = 64
N_HEADS_A = D_RWKV // HEAD_A
D_FNET = D_MIX // 4
N_FNET_GROUPS = 4
FNET_GW = D_FNET // N_FNET_GROUPS
D_POOL = D_MIX - D_RWKV - D_FNET
POOL_WINDOWS = (2, 4, 8, 16)
POOL_GW = D_POOL // len(POOL_WINDOWS)
DECAY_LORA = 64
AAA_LORA = 64
GATE_LORA = 128
N_DIR = 2
D_FF = 4 * D_MODEL
RMS_EPS = 1e-6
GN_EPS = 64e-5

R_OFF = 0
K_OFF = R_OFF + D_RWKV
V_OFF = K_OFF + D_RWKV
WD_OFF = V_OFF + D_RWKV
AD_OFF = WD_OFF + N_DIR * DECAY_LORA
GD_OFF = AD_OFF + N_DIR * AAA_LORA
SHIFT_END = GD_OFF + GATE_LORA
F_OFF = SHIFT_END
P_OFF = F_OFF + D_FNET
D_IN = P_OFF + D_POOL

kernel_name = "hybrid_rwkv7_fnet_pool_prefix_dit_step"

F32 = jnp.float32


def rms_norm(x, g):
    xf = x.astype(F32)
    y = xf * lax.rsqrt(jnp.mean(xf * xf, axis=-1, keepdims=True) + RMS_EPS)
    return (y * g.astype(F32)).astype(x.dtype)


def centred_shift(p, mu):
    zero = jnp.zeros_like(p[:, :1])
    prev = jnp.concatenate([zero, p[:, :-1]], axis=1)
    nxt = jnp.concatenate([p[:, 1:], zero], axis=1)
    return p + mu * (0.5 * (prev + nxt) - p)


def window_mean(x, w, axis):
    n = x.shape[axis]
    cs = jnp.cumsum(x.astype(F32), axis=axis)
    pad = [(0, 0)] * x.ndim
    pad[axis] = (1, 0)
    cs = jnp.pad(cs, pad)
    t = jnp.arange(n)
    lo = jnp.clip(t - w // 2, 0, n)
    hi = jnp.clip(t + w - w // 2, 0, n)
    s = jnp.take(cs, hi, axis=axis) - jnp.take(cs, lo, axis=axis)
    shape = [1] * x.ndim
    shape[axis] = n
    return s / (hi - lo).astype(F32).reshape(shape)


def pool_mixer(p, w_pool, pool_scale, grid):
    B, L, _ = p.shape
    pg = p.reshape(B, L, len(POOL_WINDOWS), POOL_GW)
    means = []
    for i, w in enumerate(POOL_WINDOWS):
        xi = pg[:, :, i]
        if grid:
            rows = L // GRID_W
            xi2 = xi.reshape(B, rows, GRID_W, POOL_GW)
            m = window_mean(window_mean(xi2, w, 1), w, 2).reshape(B, L, POOL_GW)
        else:
            m = window_mean(xi, w, 1)
        means.append(m)
    d = (jnp.stack(means, axis=2) - pg.astype(F32)).astype(p.dtype)
    y = jnp.einsum('blgc,gcd->blgd', d, w_pool).reshape(B, L, D_POOL)
    return y * pool_scale


def fourier_mixer(f, w_fnet):
    B, L, _ = f.shape
    fg = f.reshape(B, L, N_FNET_GROUPS, FNET_GW).astype(F32)
    z = jnp.fft.fft2(fg, axes=(1, 3)).real * (1.0 / math.sqrt(L * FNET_GW))
    y = jnp.einsum('blgc,gcd->blgd', z.astype(f.dtype), w_fnet)
    return y.reshape(B, L, D_FNET)


def wkv_step(S, inp):
    r, w, k, v, a, b = inp
    sa = jnp.einsum('dbhvk,dbhk->dbhv', S, a)
    S = S * w[..., None, :] + sa[..., :, None] * b[..., None, :] + v[..., :, None] * k[..., None, :]
    y = jnp.einsum('dbhvk,dbhk->dbhv', S, r)
    return S, y


def rwkv_mixer(u, s0, w0, w_up, a0, a_up, g_up, k_k, k_a, r_k, ln_w, ln_b):
    B, L, _ = u.shape
    H, N = N_HEADS_A, HEAD_A
    uf = u.astype(F32)
    r = uf[..., R_OFF:R_OFF + D_RWKV]
    k = uf[..., K_OFF:K_OFF + D_RWKV]
    v = uf[..., V_OFF:V_OFF + D_RWKV]
    wd = uf[..., WD_OFF:WD_OFF + N_DIR * DECAY_LORA].reshape(B, L, N_DIR, DECAY_LORA)
    ad = uf[..., AD_OFF:AD_OFF + N_DIR * AAA_LORA].reshape(B, L, N_DIR, AAA_LORA)
    gd = uf[..., GD_OFF:GD_OFF + GATE_LORA]
    w_log = -jax.nn.softplus(-(w0.astype(F32) + jnp.einsum('bldr,drc->bldc', jnp.tanh(wd), w_up.astype(F32)))) - 0.5
    decay = jnp.exp(-jnp.exp(w_log))
    a = jax.nn.sigmoid(a0.astype(F32) + jnp.einsum('bldr,drc->bldc', ad, a_up.astype(F32)))
    g = jnp.einsum('blr,rc->blc', jax.nn.sigmoid(gd), g_up.astype(F32))
    kk = (k * k_k.astype(F32)).reshape(B, L, H, N)
    kk = kk / jnp.maximum(jnp.sqrt(jnp.sum(kk * kk, axis=-1, keepdims=True)), 1e-12)
    kd = k[:, :, None] * (1.0 + (a - 1.0) * k_a.astype(F32))
    a_h = a.reshape(B, L, N_DIR, H, N)
    kd_h = kd.reshape(B, L, N_DIR, H, N)
    dec_h = decay.reshape(B, L, N_DIR, H, N)
    r_h = r.reshape(B, L, H, N)
    v_h = v.reshape(B, L, H, N)
    b_h = kk[:, :, None] * a_h

    def both(z):
        return jnp.broadcast_to(z[:, :, None], (B, L, N_DIR, H, N))

    def time_major(z):
        z = jnp.stack([z[:, :, 0], z[:, ::-1, 1]], axis=0)
        return jnp.transpose(z, (2, 0, 1, 3, 4))

    xs = (time_major(both(r_h)), time_major(dec_h), time_major(kd_h),
          time_major(both(v_h)), time_major(both(-kk)), time_major(b_h))
    s_init = jnp.transpose(s0.astype(F32), (1, 0, 2, 3, 4))
    s_fin, ys = lax.scan(wkv_step, s_init, xs)
    ys = jnp.transpose(ys, (1, 2, 0, 3, 4))
    y = ys[0] + ys[1][:, ::-1]
    mean = jnp.mean(y, axis=-1, keepdims=True)
    var = jnp.mean(jnp.square(y - mean), axis=-1, keepdims=True)
    y = ((y - mean) * lax.rsqrt(var + GN_EPS)).reshape(B, L, D_RWKV)
    y = y * ln_w.astype(F32) + ln_b.astype(F32)
    bonus = jnp.einsum('blhn,bldhn,hn->blh', r_h, kd_h, r_k.astype(F32))[..., None] * v_h
    out = (y + bonus.reshape(B, L, D_RWKV)) * g
    return out.astype(u.dtype), jnp.transpose(s_fin, (1, 0, 2, 3, 4))


def trunk_layer(x, mod, s0, grid, norm1_g, w_in, mu_shift, w0, w_up, a0, a_up, g_up, k_k, k_a, r_k,
                ln_x_w, ln_x_b, w_fnet, w_pool, pool_scale, w_out, norm2_g, w_ff1, w_ff2):
    shift1, scale1, gate1, shift2, scale2, gate2 = jnp.split(mod, 6, axis=-1)
    h = rms_norm(x, norm1_g) * (1 + scale1) + shift1
    proj = jnp.einsum('bld,de->ble', h, w_in)
    u = centred_shift(proj[..., :SHIFT_END], mu_shift)
    y_a, s_fin = rwkv_mixer(u, s0, w0, w_up, a0, a_up, g_up, k_k, k_a, r_k, ln_x_w, ln_x_b)
    y_b = fourier_mixer(proj[..., F_OFF:F_OFF + D_FNET], w_fnet)
    y_c = pool_mixer(proj[..., P_OFF:P_OFF + D_POOL], w_pool, pool_scale, grid)
    mixed = jnp.concatenate([y_a, y_b, y_c], axis=-1)
    x = x + gate1 * jnp.einsum('blc,cd->bld', mixed, w_out)
    h = rms_norm(x, norm2_g) * (1 + scale2) + shift2
    ff = jnp.square(jax.nn.relu(jnp.einsum('bld,df->blf', h, w_ff1)))
    x = x + gate2 * jnp.einsum('blf,fd->bld', ff, w_ff2)
    return x, s_fin


def setup_inputs(seed: int = 0) -> dict:
    key = jax.random.key(seed)
    ks = jax.random.split(key, 32)
    nrm = lambda k, shape, s: jax.random.normal(k, shape, F32) * s
    D = D_MODEL
    return {
        "x_prompt": nrm(ks[0], (BATCH, SEQ, D), 1.0),
        "x_sample": nrm(ks[1], (DEC_BATCH, DEC_SEQ, D), 1.0),
        "state_wkv": nrm(ks[2], (DEC_BATCH, DEPTH, N_DIR, N_HEADS_A, HEAD_A, HEAD_A), 0.5),
        "c": nrm(ks[3], (DEC_BATCH, D), 1.0),
        "c_ctx": nrm(ks[4], (D,), 1.0),
        "w_ada": nrm(ks[5], (DEPTH, D, 6 * D), 0.5 * D ** -0.5),
        "b_ada": nrm(ks[6], (DEPTH, 6 * D), 0.02),
        "norm1_g": 1.0 + nrm(ks[7], (DEPTH, D), 0.05),
        "w_in": nrm(ks[8], (DEPTH, D, D_IN), D ** -0.5),
        "mu_shift": jax.random.uniform(ks[9], (DEPTH, SHIFT_END), F32),
        "w0": jax.random.uniform(ks[10], (DEPTH, N_DIR, D_RWKV), F32, -6.0, -1.0),
        "w_up": nrm(ks[11], (DEPTH, N_DIR, DECAY_LORA, D_RWKV), 0.1 * DECAY_LORA ** -0.5),
        "a0": nrm(ks[12], (DEPTH, N_DIR, D_RWKV), 0.1),
        "a_up": nrm(ks[13], (DEPTH, N_DIR, AAA_LORA, D_RWKV), 0.1 * AAA_LORA ** -0.5),
        "g_up": nrm(ks[14], (DEPTH, GATE_LORA, D_RWKV), GATE_LORA ** -0.5),
        "k_k": 0.85 + nrm(ks[15], (DEPTH, D_RWKV), 0.05),
        "k_a": 1.0 + nrm(ks[16], (DEPTH, D_RWKV), 0.05),
        "r_k": nrm(ks[17], (DEPTH, N_HEADS_A, HEAD_A), 0.1),
        "ln_x_w": 1.0 + nrm(ks[18], (DEPTH, D_RWKV), 0.05),
        "ln_x_b": nrm(ks[19], (DEPTH, D_RWKV), 0.02),
        "w_fnet": nrm(ks[20], (DEPTH, N_FNET_GROUPS, FNET_GW, FNET_GW), FNET_GW ** -0.5),
        "w_pool": nrm(ks[21], (DEPTH, len(POOL_WINDOWS), POOL_GW, POOL_GW), POOL_GW ** -0.5),
        "pool_scale": 1.0 + nrm(ks[22], (DEPTH, D_POOL), 0.05),
        "w_out": nrm(ks[23], (DEPTH, D_MIX, D), D_MIX ** -0.5),
        "norm2_g": 1.0 + nrm(ks[24], (DEPTH, D), 0.05),
        "w_ff1": nrm(ks[25], (DEPTH, D, D_FF), D ** -0.5),
        "w_ff2": nrm(ks[26], (DEPTH, D_FF, D), D_FF ** -0.5),
        "final_norm_g": 1.0 + nrm(ks[27], (D,), 0.05),
    }


def reference(x_prompt, x_sample, state_wkv, c, c_ctx, w_ada, b_ada, norm1_g, w_in, mu_shift, w0, w_up,
              a0, a_up, g_up, k_k, k_a, r_k, ln_x_w, ln_x_b, w_fnet, w_pool, pool_scale, w_out, norm2_g,
              w_ff1, w_ff2, final_norm_g):
    xp = x_prompt
    xs = x_sample
    s_zero = jnp.zeros((xp.shape[0], N_DIR, N_HEADS_A, HEAD_A, HEAD_A), F32)
    new_states = []
    for l in range(DEPTH):
        lw = (norm1_g[l], w_in[l], mu_shift[l], w0[l], w_up[l], a0[l], a_up[l], g_up[l], k_k[l], k_a[l],
              r_k[l], ln_x_w[l], ln_x_b[l], w_fnet[l], w_pool[l], pool_scale[l], w_out[l], norm2_g[l],
              w_ff1[l], w_ff2[l])
        mod_ctx = (jax.nn.silu(c_ctx) @ w_ada[l] + b_ada[l])[None, None, :]
        mod_lat = (jax.nn.silu(c) @ w_ada[l] + b_ada[l])[:, None, :]
        xp, s_ctx = trunk_layer(xp, mod_ctx, s_zero, False, *lw)
        new_states.append(s_ctx)
        xs, _ = trunk_layer(xs, mod_lat, state_wkv[:, l], True, *lw)
    new_state_wkv = jnp.stack(new_states, axis=1).astype(x_prompt.dtype)
    y_prompt = rms_norm(xp, final_norm_g)
    y_sample = rms_norm(xs, final_norm_g)
    return (y_prompt, y_sample, new_state_wkv)
```

```python
import functools
import math

import numpy as np
import jax
import jax.numpy as jnp
from jax import lax
from jax.experimental import pallas as pl
from jax.experimental.pallas import tpu as pltpu

F32 = jnp.float32
BF16 = jnp.bfloat16

D_MODEL = 1024
DEPTH = 2
GRID_W = 64
D_RWKV = 512
HEAD = 64
N_HEADS = 8
LANES = 128
N_PAIRS = D_RWKV // LANES
D_FNET = 256
FNET_GW = 64
D_POOL = 256
POOL_GW = 64
POOL_WINDOWS = (2, 4, 8, 16)
LORA = 64
GATE_LORA = 128
SHIFT_END = 3 * D_RWKV + 2 * LORA + 2 * LORA + GATE_LORA
LORA_W = SHIFT_END - 3 * D_RWKV
D_IN = SHIFT_END + D_FNET + D_POOL
D_FF = 4 * D_MODEL
RMS_EPS = 1e-6
GN_EPS = 64e-5
CHUNK = 64
VMEM_LIMIT = 56 * 1024 * 1024


def _dotf(a, b):
    return jnp.dot(a, b, preferred_element_type=F32, precision=lax.Precision.HIGHEST)


def _dotf_nt(a, b):
    return lax.dot_general(a, b, (((1,), (1,)), ((), ())), preferred_element_type=F32,
                           precision=lax.Precision.HIGHEST)


def _dotb(a, b):
    return jnp.dot(a.astype(BF16), b.astype(BF16), preferred_element_type=F32)


def _sigmoid(x):
    return 1.0 / (1.0 + jnp.exp(-x))


def _softplus(x):
    return jnp.maximum(x, 0.0) + jnp.log(1.0 + jnp.exp(-jnp.abs(x)))


def _rms(x, g):
    ms = jnp.mean(x * x, axis=-1, keepdims=True)
    return x * lax.rsqrt(ms + RMS_EPS) * g


def _ada_kernel(c_ref, w_ref, b_ref, o_ref):
    c = c_ref[...]
    s = c * _sigmoid(c)
    o_ref[0] = _dotf(s, w_ref[0]) + b_ref[0]


def _ada(cond, w_ada, b_ada):
    tn = 1536
    nrow = cond.shape[0]
    return pl.pallas_call(
        _ada_kernel,
        out_shape=jax.ShapeDtypeStruct((DEPTH, nrow, 6 * D_MODEL), F32),
        grid=(DEPTH, 6 * D_MODEL // tn),
        in_specs=[
            pl.BlockSpec((nrow, D_MODEL), lambda l, j: (0, 0)),
            pl.BlockSpec((1, D_MODEL, tn), lambda l, j: (l, 0, j)),
            pl.BlockSpec((1, 1, tn), lambda l, j: (l, 0, j)),
        ],
        out_specs=pl.BlockSpec((1, nrow, tn), lambda l, j: (l, 0, j)),
        compiler_params=pltpu.CompilerParams(
            dimension_semantics=("arbitrary", "arbitrary"), vmem_limit_bytes=VMEM_LIMIT),
        name="ada",
    )(cond, w_ada, b_ada.reshape(DEPTH, 1, 6 * D_MODEL))


def _pre_kernel(x_ref, mod_ref, g_ref, w_ref, oa_ref, of_ref, op_ref):
    mod = mod_ref[0]
    shift = mod[:, 0:D_MODEL]
    scale = mod[:, D_MODEL:2 * D_MODEL]
    h = _rms(x_ref[...], g_ref[...]) * (1.0 + scale) + shift
    p = jnp.dot(h.astype(BF16), w_ref[...], preferred_element_type=F32)
    oa_ref[...] = p[:, :SHIFT_END]
    of_ref[...] = p[:, SHIFT_END:SHIFT_END + D_FNET]
    op_ref[...] = p[:, SHIFT_END + D_FNET:]


def _pre(x, mod, mod_row, g, w_in_b, tm):
    t = x.shape[0]
    return pl.pallas_call(
        _pre_kernel,
        out_shape=(jax.ShapeDtypeStruct((t, SHIFT_END), F32),
                   jax.ShapeDtypeStruct((t, D_FNET), F32),
                   jax.ShapeDtypeStruct((t, D_POOL), F32)),
        grid=(t // tm,),
        in_specs=[
            pl.BlockSpec((tm, D_MODEL), lambda i: (i, 0)),
            pl.BlockSpec((1, 1, 6 * D_MODEL), lambda i: (mod_row(i), 0, 0)),
            pl.BlockSpec((1, D_MODEL), lambda i: (0, 0)),
            pl.BlockSpec((D_MODEL, D_IN), lambda i: (0, 0)),
        ],
        out_specs=(pl.BlockSpec((tm, SHIFT_END), lambda i: (i, 0)),
                   pl.BlockSpec((tm, D_FNET), lambda i: (i, 0)),
                   pl.BlockSpec((tm, D_POOL), lambda i: (i, 0))),
        compiler_params=pltpu.CompilerParams(
            dimension_semantics=("arbitrary",), vmem_limit_bytes=VMEM_LIMIT),
        name="pre",
    )(x, mod, g.reshape(1, D_MODEL), w_in_b)


def _shift_mix(p, mu):
    n = p.shape[0]
    row = lax.broadcasted_iota(jnp.int32, p.shape, 0)
    prev = jnp.where(row == 0, 0.0, pltpu.roll(p, 1, 0))
    nxt = jnp.where(row == n - 1, 0.0, pltpu.roll(p, n - 1, 0))
    return p + mu * (0.5 * (prev + nxt) - p)


def _rwkv_kernel(seq, has_state, *refs):
    (r_ref, k_ref, v_ref, lo_ref, mur_ref, muk_ref, muv_ref, mul_ref, w0_ref, wup_ref, a0_ref,
     aup_ref, gup_ref, kk_ref, ka_ref, rk_ref, lnw_ref, lnb_ref) = refs[:18]
    pos = 18
    s0_ref = None
    if has_state:
        s0_ref = refs[pos]
        pos += 1
    y_ref, sfin_ref = refs[pos:pos + 2]
    lw_s, kd_s, b_s, r_s, v_s, kn_s, st_s = refs[pos + 2:]
    nc = seq // CHUNK
    C = CHUNK

    lane1 = lax.broadcasted_iota(jnp.int32, (1, LANES), 1)
    head0 = lane1 < HEAD
    ri = lax.broadcasted_iota(jnp.int32, (LANES, LANES), 0)
    ci = lax.broadcasted_iota(jnp.int32, (LANES, LANES), 1)
    same_head = (ri // HEAD) == (ci // HEAD)
    seg_ones = same_head.astype(F32)
    eye = (ri == ci).astype(F32)

    r = _shift_mix(r_ref[0], mur_ref[...])
    k = _shift_mix(k_ref[0], muk_ref[...])
    v = _shift_mix(v_ref[0], muv_ref[...])
    lo = _shift_mix(lo_ref[0], mul_ref[...])
    wd = jnp.tanh(lo[:, 0:2 * LORA])
    ad = lo[:, 2 * LORA:4 * LORA]
    gd = _sigmoid(lo[:, 4 * LORA:])
    g = _dotb(gd, gup_ref[...])
    kx = k * kk_ref[...]
    ss = _dotf(kx * kx, seg_ones)
    kn = kx / jnp.maximum(jnp.sqrt(ss), 1e-12)
    zeros_up = jnp.zeros((LORA, LANES), F32)
    kd_sum = jnp.zeros_like(k)
    for d in range(2):
        if d == 0:
            wup = jnp.concatenate([wup_ref[0], zeros_up], axis=0)
            aup = jnp.concatenate([aup_ref[0], zeros_up], axis=0)
        else:
            wup = jnp.concatenate([zeros_up, wup_ref[1]], axis=0)
            aup = jnp.concatenate([zeros_up, aup_ref[1]], axis=0)
        zw = w0_ref[d:d + 1, :] + _dotb(wd, wup)
        w_log = -_softplus(-zw) - 0.5
        lw_s[d] = -jnp.exp(w_log)
        a_sig = _sigmoid(a0_ref[d:d + 1, :] + _dotb(ad, aup))
        kd = k * (1.0 + (a_sig - 1.0) * ka_ref[...])
        kd_s[d] = kd
        b_s[d] = kn * a_sig
        kd_sum = kd_sum + kd
    r_s[...] = r
    v_s[...] = v
    kn_s[...] = kn
    bonus = _dotf(r * kd_sum * rk_ref[...], seg_ones) * v
    y_ref[0] = jnp.zeros((seq, LANES), F32)

    if has_state:
        rj = lax.broadcasted_iota(jnp.int32, (LANES, HEAD), 0)
        cj = lax.broadcasted_iota(jnp.int32, (LANES, HEAD), 1)
        for d in range(2):
            st = jnp.zeros((LANES, LANES), F32)
            for j in range(2):
                place = (rj == cj + j * HEAD).astype(F32)
                t1 = _dotf_nt(place, s0_ref[0, d, j])
                st = st + _dotf_nt(t1, place)
            st_s[d] = st
    else:
        st_s[...] = jnp.zeros((2, LANES, LANES), F32)

    tr = lax.broadcasted_iota(jnp.int32, (C, C), 0)
    tc = lax.broadcasted_iota(jnp.int32, (C, C), 1)
    sr = lax.broadcasted_iota(jnp.int32, (2 * C, 2 * C), 0)
    sc = lax.broadcasted_iota(jnp.int32, (2 * C, 2 * C), 1)
    t_idx = sr % C
    s_idx = sc % C
    incl = (sr >= C).astype(jnp.int32)
    lane_c = lax.broadcasted_iota(jnp.int32, (C, LANES), 1)
    lo_half = lane_c < HEAD
    zero_cl = jnp.zeros((C, LANES), F32)
    tris = ((tc <= tr).astype(F32), (tc >= tr).astype(F32))
    sc_masks = (s_idx < t_idx + incl, s_idx > t_idx - incl)

    def chunk_step(i, carry):
        for d in range(2):
            c = i if d == 0 else nc - 1 - i
            off = pl.multiple_of(c * C, C)
            sl = pl.ds(off, C)
            lw = lw_s[d, sl, :]
            kd = kd_s[d, sl, :]
            b = b_s[d, sl, :]
            rc = r_s[sl, :]
            vc = v_s[sl, :]
            knc = kn_s[sl, :]
            sc_mask = sc_masks[d]
            cl = _dotf(tris[d], lw)
            tot = cl[C - 1:C, :] if d == 0 else cl[0:1, :]
            at = -knc * jnp.exp(cl - lw)
            rt = rc * jnp.exp(cl)
            e_inv = jnp.exp(-cl)
            bt = b * e_inv
            kt = kd * e_inv
            e_end = jnp.exp(tot - cl)
            bh = b * e_end
            kh = kd * e_end
            p_c = jnp.exp(tot)
            rhs_s = jnp.concatenate([bt, kt], axis=0)
            vv = jnp.concatenate([vc, vc], axis=0)
            xa = []
            xv = []
            bots = []
            for j in range(2):
                hm = head0 if j == 0 else jnp.logical_not(head0)
                lhs = jnp.concatenate([jnp.where(hm, at, 0.0), jnp.where(hm, rt, 0.0)], axis=0)
                scm = jnp.where(sc_mask, _dotf_nt(lhs, rhs_s), 0.0)
                topm = scm[:C]
                bots.append(scm[C:])
                lp = jnp.where(lo_half, topm, 0.0)
                lak = jnp.where(lo_half, 0.0, topm)
                x_a = at
                x_v = _dotf(lak, vv)
                for lvl in range(6):
                    z = jnp.concatenate([lp, x_a, x_v], axis=1)
                    w = _dotf(lp[:, :C], z)
                    lp = w[:, :LANES]
                    x_a = x_a + w[:, LANES:2 * LANES]
                    x_v = x_v + w[:, 2 * LANES:]
                xa.append(x_a)
                xv.append(x_v)
            abar = jnp.where(head0, xa[0], xa[1])
            ubar = jnp.where(head0, xv[0], xv[1])
            rhs2 = jnp.concatenate([jnp.concatenate([abar, ubar], axis=1),
                                    jnp.concatenate([zero_cl, vc], axis=1)], axis=0)
            o2 = _dotf(jnp.concatenate(bots, axis=0), rhs2)
            rbar = rt + jnp.where(head0, o2[:C, :LANES], o2[C:, :LANES])
            ybar = jnp.where(head0, o2[:C, LANES:], o2[C:, LANES:])
            gh = _dotf(jnp.concatenate([bh, kh], axis=0).T, rhs2)
            gt = jnp.where(same_head, gh[:, :LANES], 0.0) + eye * p_c
            ht = jnp.where(same_head, gh[:, LANES:], 0.0)
            st = st_s[d]
            y_ref[0, sl, :] += _dotf(rbar, st) + ybar
            st_s[d] = _dotf(gt, st) + ht
        return carry

    lax.fori_loop(0, nc, chunk_step, 0)

    for d in range(2):
        stt = st_s[d].T
        rj2 = lax.broadcasted_iota(jnp.int32, (HEAD, LANES), 0)
        cj2 = lax.broadcasted_iota(jnp.int32, (HEAD, LANES), 1)
        for j in range(2):
            sel = (cj2 == rj2 + j * HEAD).astype(F32)
            sfin_ref[0, d, j] = _dotf_nt(_dotf(sel, stt), sel)

    y = y_ref[0]
    mean = _dotf(y, seg_ones) * (1.0 / HEAD)
    yc = y - mean
    var = _dotf(yc * yc, seg_ones) * (1.0 / HEAD)
    yn = yc * lax.rsqrt(var + GN_EPS) * lnw_ref[...] + lnb_ref[...]
    y_ref[0] = (yn + bonus) * g


def _rwkv(pa, s0, mu, w0, w_up, a0, a_up, g_up, k_k, k_a, r_k, ln_w, ln_b):
    bsz, seq, _ = pa.shape
    has_state = s0 is not None
    mu2 = mu.reshape(1, SHIFT_END)
    row = lambda a: a.reshape(1, D_RWKV)
    col = lambda o: (lambda b, p: (b, 0, o + p))
    vec = lambda b, p: (0, p)
    in_specs = [
        pl.BlockSpec((1, seq, LANES), col(0)),
        pl.BlockSpec((1, seq, LANES), col(N_PAIRS)),
        pl.BlockSpec((1, seq, LANES), col(2 * N_PAIRS)),
        pl.BlockSpec((1, seq, LORA_W), lambda b, p: (b, 0, 3 * D_RWKV // LORA_W)),
        pl.BlockSpec((1, LANES), lambda b, p: (0, p)),
        pl.BlockSpec((1, LANES), lambda b, p: (0, N_PAIRS + p)),
        pl.BlockSpec((1, LANES), lambda b, p: (0, 2 * N_PAIRS + p)),
        pl.BlockSpec((1, LORA_W), lambda b, p: (0, 3 * D_RWKV // LORA_W)),
        pl.BlockSpec((2, LANES), vec),
        pl.BlockSpec((2, LORA, LANES), lambda b, p: (0, 0, p)),
        pl.BlockSpec((2, LANES), vec),
        pl.BlockSpec((2, LORA, LANES), lambda b, p: (0, 0, p)),
        pl.BlockSpec((GATE_LORA, LANES), vec),
        pl.BlockSpec((1, LANES), vec),
        pl.BlockSpec((1, LANES), vec),
        pl.BlockSpec((1, LANES), vec),
        pl.BlockSpec((1, LANES), vec),
        pl.BlockSpec((1, LANES), vec),
    ]
    args = [pa, pa, pa, pa, mu2, mu2, mu2, mu2, w0, w_up, a0, a_up, g_up, row(k_k), row(k_a),
            row(r_k), row(ln_w), row(ln_b)]
    state_spec = pl.BlockSpec((1, 2, 2, HEAD, HEAD), lambda b, p: (b, 0, p, 0, 0))
    if has_state:
        in_specs.append(state_spec)
        args.append(s0)
    return pl.pallas_call(
        functools.partial(_rwkv_kernel, seq, has_state),
        out_shape=(jax.ShapeDtypeStruct((bsz, seq, D_RWKV), F32),
                   jax.ShapeDtypeStruct((bsz, 2, N_HEADS, HEAD, HEAD), F32)),
        grid=(bsz, N_PAIRS),
        in_specs=in_specs,
        out_specs=(pl.BlockSpec((1, seq, LANES), lambda b, p: (b, 0, p)), state_spec),
        scratch_shapes=[
            pltpu.VMEM((2, seq, LANES), F32),
            pltpu.VMEM((2, seq, LANES), F32),
            pltpu.VMEM((2, seq, LANES), F32),
            pltpu.VMEM((seq, LANES), F32),
            pltpu.VMEM((seq, LANES), F32),
            pltpu.VMEM((seq, LANES), F32),
            pltpu.VMEM((2, LANES, LANES), F32),
        ],
        compiler_params=pltpu.CompilerParams(
            dimension_semantics=("arbitrary", "arbitrary"), vmem_limit_bytes=VMEM_LIMIT),
        name="rwkv",
    )(*args)


def _fpool_kernel(seq, f_ref, p_ref, cs_ref, c64_ref, s64_ref, wf_ref, band_ref, inv_ref, wp_ref,
                  ps_ref, yb_ref, yc_ref):
    f = f_ref[0]
    wf = wf_ref[...]
    g1 = _dotf(f, _dotf(c64_ref[...], wf))
    g2 = _dotf(f, _dotf(s64_ref[...], wf))
    yb_ref[0] = _dotf(cs_ref[...], jnp.concatenate([g1, -g2], axis=0))

    p = p_ref[0]
    p_hi = p.astype(BF16)
    p_lo = (p - p_hi.astype(F32)).astype(BF16)
    lane = lax.broadcasted_iota(jnp.int32, (seq, D_POOL), 1)
    d = jnp.zeros((seq, D_POOL), F32)
    for i in range(len(POOL_WINDOWS)):
        band = band_ref[i]
        s = (jnp.dot(band, p_hi, preferred_element_type=F32)
             + jnp.dot(band, p_lo, preferred_element_type=F32))
        d = jnp.where((lane // POOL_GW) == i, s, d)
    d = d * inv_ref[...] - p
    yc_ref[0] = _dotb(d, wp_ref[...]) * ps_ref[...]


def _fpool(pf, pp, consts, wf_bd, wp_bd, pool_scale):
    bsz, seq, _ = pf.shape
    cs, c64, s64, band, inv = consts
    full2 = lambda b: (0, 0)
    return pl.pallas_call(
        functools.partial(_fpool_kernel, seq),
        out_shape=(jax.ShapeDtypeStruct((bsz, seq, D_FNET), F32),
                   jax.ShapeDtypeStruct((bsz, seq, D_POOL), F32)),
        grid=(bsz,),
        in_specs=[
            pl.BlockSpec((1, seq, D_FNET), lambda b: (b, 0, 0)),
            pl.BlockSpec((1, seq, D_POOL), lambda b: (b, 0, 0)),
            pl.BlockSpec((seq, 2 * seq), full2),
            pl.BlockSpec((D_FNET, D_FNET), full2),
            pl.BlockSpec((D_FNET, D_FNET), full2),
            pl.BlockSpec((D_FNET, D_FNET), full2),
            pl.BlockSpec((len(POOL_WINDOWS), seq, seq), lambda b: (0, 0, 0)),
            pl.BlockSpec((seq, D_POOL), full2),
            pl.BlockSpec((D_POOL, D_POOL), full2),
            pl.BlockSpec((1, D_POOL), full2),
        ],
        out_specs=(pl.BlockSpec((1, seq, D_FNET), lambda b: (b, 0, 0)),
                   pl.BlockSpec((1, seq, D_POOL), lambda b: (b, 0, 0))),
        compiler_params=pltpu.CompilerParams(
            dimension_semantics=("arbitrary",), vmem_limit_bytes=VMEM_LIMIT),
        name="fpool",
    )(pf, pp, cs, c64, s64, wf_bd, band, inv, wp_bd, pool_scale.reshape(1, D_POOL))


def _window_matrix(n, w):
    t = np.arange(n)
    lo = np.clip(t - w // 2, 0, n)
    hi = np.clip(t + w - w // 2, 0, n)
    s = np.arange(n)
    return ((s[None, :] >= lo[:, None]) & (s[None, :] < hi[:, None])).astype(np.float64)


def _mixer_constants(seq, grid):
    n = np.arange(seq)
    ang = 2.0 * np.pi * ((n[:, None] * n[None, :]) % seq) / seq
    scale = 1.0 / math.sqrt(seq * FNET_GW)
    cs = np.concatenate([np.cos(ang), np.sin(ang)], axis=1) * scale
    m = np.arange(FNET_GW)
    ang64 = 2.0 * np.pi * ((m[:, None] * m[None, :]) % FNET_GW) / FNET_GW
    groups = D_FNET // FNET_GW
    c64 = np.kron(np.eye(groups), np.cos(ang64))
    s64 = np.kron(np.eye(groups), np.sin(ang64))
    bands = []
    invs = []
    for w in POOL_WINDOWS:
        if grid:
            rows = seq // GRID_W
            band = np.kron(_window_matrix(rows, w), _window_matrix(GRID_W, w))
        else:
            band = _window_matrix(seq, w)
        bands.append(band)
        invs.append(np.repeat((1.0 / band.sum(axis=1))[:, None], POOL_GW, axis=1))
    band = jnp.asarray(np.stack(bands), dtype=BF16)
    inv = jnp.asarray(np.concatenate(invs, axis=1), dtype=F32)
    return (jnp.asarray(cs, dtype=F32), jnp.asarray(c64, dtype=F32), jnp.asarray(s64, dtype=F32),
            band, inv)


def _block_diag(w):
    g, n, _ = w.shape
    out = jnp.zeros((g * n, g * n), w.dtype)
    for i in range(g):
        out = out.at[i * n:(i + 1) * n, i * n:(i + 1) * n].set(w[i])
    return out


def _post_kernel(final, x_ref, ya_ref, yb_ref, yc_ref, mod_ref, wo_ref, g2_ref, w1_ref, w2_ref,
                 fg_ref, o_ref):
    mod = mod_ref[0]
    gate1 = mod[:, 2 * D_MODEL:3 * D_MODEL]
    shift2 = mod[:, 3 * D_MODEL:4 * D_MODEL]
    scale2 = mod[:, 4 * D_MODEL:5 * D_MODEL]
    gate2 = mod[:, 5 * D_MODEL:]
    mix = (jnp.dot(ya_ref[...].astype(BF16), wo_ref[0:D_RWKV, :], preferred_element_type=F32)
           + jnp.dot(yb_ref[...].astype(BF16), wo_ref[D_RWKV:D_RWKV + D_FNET, :],
                     preferred_element_type=F32)
           + jnp.dot(yc_ref[...].astype(BF16), wo_ref[D_RWKV + D_FNET:, :],
                     preferred_element_type=F32))
    x1 = x_ref[...] + gate1 * mix
    h2 = (_rms(x1, g2_ref[...]) * (1.0 + scale2) + shift2).astype(BF16)
    tf = 1024
    acc = jnp.zeros_like(x1)
    for j in range(D_FF // tf):
        ff = jnp.dot(h2, w1_ref[:, j * tf:(j + 1) * tf], preferred_element_type=F32)
        ff = jnp.square(jnp.maximum(ff, 0.0))
        acc = acc + jnp.dot(ff.astype(BF16), w2_ref[j * tf:(j + 1) * tf, :],
                            preferred_element_type=F32)
    x2 = x1 + gate2 * acc
    if final:
        x2 = _rms(x2, fg_ref[...])
    o_ref[...] = x2


def _post(x, ya, yb, yc, mod, mod_row, w_out_b, g2, w1_b, w2_b, fg, final, tm):
    t = x.shape[0]
    tok = lambda i: (i, 0)
    full = lambda i: (0, 0)
    return pl.pallas_call(
        functools.partial(_post_kernel, final),
        out_shape=jax.ShapeDtypeStruct((t, D_MODEL), F32),
        grid=(t // tm,),
        in_specs=[
            pl.BlockSpec((tm, D_MODEL), tok),
            pl.BlockSpec((tm, D_RWKV), tok),
            pl.BlockSpec((tm, D_FNET), tok),
            pl.BlockSpec((tm, D_POOL), tok),
            pl.BlockSpec((1, 1, 6 * D_MODEL), lambda i: (mod_row(i), 0, 0)),
            pl.BlockSpec((D_MODEL, D_MODEL), full),
            pl.BlockSpec((1, D_MODEL), full),
            pl.BlockSpec((D_MODEL, D_FF), full),
            pl.BlockSpec((D_FF, D_MODEL), full),
            pl.BlockSpec((1, D_MODEL), full),
        ],
        out_specs=pl.BlockSpec((tm, D_MODEL), tok),
        compiler_params=pltpu.CompilerParams(
            dimension_semantics=("arbitrary",), vmem_limit_bytes=VMEM_LIMIT),
        name="post",
    )(x, ya, yb, yc, mod, w_out_b, g2.reshape(1, D_MODEL), w1_b, w2_b, fg.reshape(1, D_MODEL))


def _layer(x, bsz, seq, mod, mod_row, s0, consts, lw, final, fg, tm):
    (norm1_g, w_in_b, mu_shift, w0, w_up, a0, a_up, g_up, k_k, k_a, r_k, ln_w, ln_b, wf_bd, wp_bd,
     pool_scale, w_out_b, norm2_g, w1_b, w2_b) = lw
    pa, pf, pp = _pre(x, mod, mod_row, norm1_g, w_in_b, tm)
    ya, s_fin = _rwkv(pa.reshape(bsz, seq, SHIFT_END), s0, mu_shift, w0, w_up, a0, a_up, g_up,
                      k_k, k_a, r_k, ln_w, ln_b)
    yb, yc = _fpool(pf.reshape(bsz, seq, D_FNET), pp.reshape(bsz, seq, D_POOL), consts, wf_bd,
                    wp_bd, pool_scale)
    x = _post(x, ya.reshape(bsz * seq, D_RWKV), yb.reshape(bsz * seq, D_FNET),
              yc.reshape(bsz * seq, D_POOL), mod, mod_row, w_out_b, norm2_g, w1_b, w2_b, fg, final, tm)
    return x, s_fin


def kernel(x_prompt, x_sample, state_wkv, c, c_ctx, w_ada, b_ada, norm1_g, w_in, mu_shift, w0, w_up, a0, a_up, g_up, k_k, k_a, r_k, ln_x_w, ln_x_b, w_fnet, w_pool, pool_scale, w_out, norm2_g, w_ff1, w_ff2, final_norm_g):
    bp, lp, _ = x_prompt.shape
    bs, ls, _ = x_sample.shape
    tm = 512
    n_rows = 8
    cond = jnp.zeros((n_rows, D_MODEL), F32).at[0].set(c_ctx).at[1:1 + bs].set(c)
    mod_all = _ada(cond, w_ada, b_ada)
    consts_p = _mixer_constants(lp, False)
    consts_s = _mixer_constants(ls, True)
    xp = x_prompt.reshape(bp * lp, D_MODEL)
    xs = x_sample.reshape(bs * ls, D_MODEL)
    tiles_per_seq = ls // tm
    row_p = lambda i: 0
    row_s = lambda i: 1 + i // tiles_per_seq
    new_states = []
    for l in range(DEPTH):
        lw = (norm1_g[l], w_in[l].astype(BF16), mu_shift[l], w0[l], w_up[l], a0[l], a_up[l], g_up[l],
              k_k[l], k_a[l], r_k[l], ln_x_w[l], ln_x_b[l], _block_diag(w_fnet[l]),
              _block_diag(w_pool[l]), pool_scale[l], w_out[l].astype(BF16), norm2_g[l],
              w_ff1[l].astype(BF16), w_ff2[l].astype(BF16))
        mod = mod_all[l].reshape(n_rows, 1, 6 * D_MODEL)
        final = l == DEPTH - 1
        xp, s_ctx = _layer(xp, bp, lp, mod, row_p, None, consts_p, lw, final, final_norm_g, tm)
        new_states.append(s_ctx)
        xs, _ = _layer(xs, bs, ls, mod, row_s, state_wkv[:, l], consts_s, lw, final, final_norm_g, tm)
    new_state_wkv = jnp.stack(new_states, axis=1)
    return (xp.reshape(bp, lp, D_MODEL), xs.reshape(bs, ls, D_MODEL), new_state_wkv)
```

```python
import functools
import math

import numpy as np
import jax
import jax.numpy as jnp
from jax import lax
from jax.experimental import pallas as pl
from jax.experimental.pallas import tpu as pltpu

F32 = jnp.float32
BF16 = jnp.bfloat16

D_MODEL = 1024
DEPTH = 2
GRID_W = 64
D_RWKV = 512
HEAD = 64
N_HEADS = 8
LANES = 128
N_PAIRS = D_RWKV // LANES
D_FNET = 256
FNET_GW = 64
D_POOL = 256
POOL_GW = 64
POOL_WINDOWS = (2, 4, 8, 16)
LORA = 64
GATE_LORA = 128
SHIFT_END = 3 * D_RWKV + 2 * LORA + 2 * LORA + GATE_LORA
LORA_W = SHIFT_END - 3 * D_RWKV
D_IN = SHIFT_END + D_FNET + D_POOL
D_FF = 4 * D_MODEL
RMS_EPS = 1e-6
GN_EPS = 64e-5
CHUNK = 64
VMEM_LIMIT = 56 * 1024 * 1024


def _dotf(a, b):
    return jnp.dot(a, b, preferred_element_type=F32, precision=lax.Precision.HIGHEST)


def _dotf_nt(a, b):
    return lax.dot_general(a, b, (((1,), (1,)), ((), ())), preferred_element_type=F32,
                           precision=lax.Precision.HIGHEST)


def _split(x, n):
    terms = []
    for _ in range(n - 1):
        hi = x.astype(BF16)
        terms.append(hi)
        x = x - hi.astype(F32)
    terms.append(x.astype(BF16))
    return terms


def _dots(a, b, na, nb, nt=False):
    dims = (((1,), (1,)), ((), ())) if nt else (((1,), (0,)), ((), ()))
    ta = _split(a, na)
    tb = _split(b, nb)
    out = None
    for i in range(na):
        for j in range(nb):
            if i + j < max(na, nb):
                t = lax.dot_general(ta[i], tb[j], dims, preferred_element_type=F32)
                out = t if out is None else out + t
    return out


def _dotb(a, b):
    return jnp.dot(a.astype(BF16), b.astype(BF16), preferred_element_type=F32)


def _sigmoid(x):
    return 1.0 / (1.0 + jnp.exp(-x))


def _softplus(x):
    return jnp.maximum(x, 0.0) + jnp.log(1.0 + jnp.exp(-jnp.abs(x)))


def _rms(x, g):
    ms = jnp.mean(x * x, axis=-1, keepdims=True)
    return x * lax.rsqrt(ms + RMS_EPS) * g


def _ada_kernel(c_ref, w_ref, b_ref, o_ref):
    c = c_ref[...]
    s = c * _sigmoid(c)
    o_ref[0] = _dotf(s, w_ref[0]) + b_ref[0]


def _ada(cond, w_ada, b_ada):
    tn = 1536
    nrow = cond.shape[0]
    return pl.pallas_call(
        _ada_kernel,
        out_shape=jax.ShapeDtypeStruct((DEPTH, nrow, 6 * D_MODEL), F32),
        grid=(DEPTH, 6 * D_MODEL // tn),
        in_specs=[
            pl.BlockSpec((nrow, D_MODEL), lambda l, j: (0, 0)),
            pl.BlockSpec((1, D_MODEL, tn), lambda l, j: (l, 0, j)),
            pl.BlockSpec((1, 1, tn), lambda l, j: (l, 0, j)),
        ],
        out_specs=pl.BlockSpec((1, nrow, tn), lambda l, j: (l, 0, j)),
        compiler_params=pltpu.CompilerParams(
            dimension_semantics=("arbitrary", "arbitrary"), vmem_limit_bytes=VMEM_LIMIT),
        name="ada",
    )(cond, w_ada, b_ada.reshape(DEPTH, 1, 6 * D_MODEL))


def _pre_kernel(x_ref, mod_ref, g_ref, w_ref, oa_ref, of_ref, op_ref):
    mod = mod_ref[0]
    shift = mod[:, 0:D_MODEL]
    scale = mod[:, D_MODEL:2 * D_MODEL]
    h = _rms(x_ref[...], g_ref[...]) * (1.0 + scale) + shift
    p = jnp.dot(h.astype(BF16), w_ref[...], preferred_element_type=F32)
    oa_ref[...] = p[:, :SHIFT_END]
    of_ref[...] = p[:, SHIFT_END:SHIFT_END + D_FNET]
    op_ref[...] = p[:, SHIFT_END + D_FNET:]


def _pre(x, mod, mod_row, g, w_in_b, tm):
    t = x.shape[0]
    return pl.pallas_call(
        _pre_kernel,
        out_shape=(jax.ShapeDtypeStruct((t, SHIFT_END), F32),
                   jax.ShapeDtypeStruct((t, D_FNET), F32),
                   jax.ShapeDtypeStruct((t, D_POOL), F32)),
        grid=(t // tm,),
        in_specs=[
            pl.BlockSpec((tm, D_MODEL), lambda i: (i, 0)),
            pl.BlockSpec((1, 1, 6 * D_MODEL), lambda i: (mod_row(i), 0, 0)),
            pl.BlockSpec((1, D_MODEL), lambda i: (0, 0)),
            pl.BlockSpec((D_MODEL, D_IN), lambda i: (0, 0)),
        ],
        out_specs=(pl.BlockSpec((tm, SHIFT_END), lambda i: (i, 0)),
                   pl.BlockSpec((tm, D_FNET), lambda i: (i, 0)),
                   pl.BlockSpec((tm, D_POOL), lambda i: (i, 0))),
        compiler_params=pltpu.CompilerParams(
            dimension_semantics=("arbitrary",), vmem_limit_bytes=VMEM_LIMIT),
        name="pre",
    )(x, mod, g.reshape(1, D_MODEL), w_in_b)


def _shift_mix(p, mu):
    n = p.shape[0]
    row = lax.broadcasted_iota(jnp.int32, p.shape, 0)
    prev = jnp.where(row == 0, 0.0, pltpu.roll(p, 1, 0))
    nxt = jnp.where(row == n - 1, 0.0, pltpu.roll(p, n - 1, 0))
    return p + mu * (0.5 * (prev + nxt) - p)


PREC = dict(cum=(1, 2), seg=(2, 1), score=(1, 1), lv=(1, 1), lvl=(1, 1), app=(1, 1), o2=(1, 1), gh=(1, 1),
            y=(1, 1), st=(1, 1), place=(1, 3))
GROUP = 4


def _rwkv_kernel(seq, has_state, *refs):
    (r_ref, k_ref, v_ref, lo_ref, mur_ref, muk_ref, muv_ref, mul_ref, w0_ref, wup_ref, a0_ref,
     aup_ref, gup_ref, kk_ref, ka_ref, rk_ref, lnw_ref, lnb_ref) = refs[:18]
    pos = 18
    s0_ref = None
    if has_state:
        s0_ref = refs[pos]
        pos += 1
    y_ref, sfin_ref = refs[pos:pos + 2]
    lw_s, kd_s, b_s, r_s, v_s, kn_s, st_s, rb_s, yb_s, gt_s, ht_s = refs[pos + 2:]
    nc = seq // CHUNK
    C = CHUNK

    lane1 = lax.broadcasted_iota(jnp.int32, (1, LANES), 1)
    head0 = lane1 < HEAD
    ri = lax.broadcasted_iota(jnp.int32, (LANES, LANES), 0)
    ci = lax.broadcasted_iota(jnp.int32, (LANES, LANES), 1)
    same_head = (ri // HEAD) == (ci // HEAD)
    seg_ones = same_head.astype(F32)
    eye = (ri == ci).astype(F32)

    r = _shift_mix(r_ref[0], mur_ref[...])
    k = _shift_mix(k_ref[0], muk_ref[...])
    v = _shift_mix(v_ref[0], muv_ref[...])
    lo = _shift_mix(lo_ref[0], mul_ref[...])
    wd = jnp.tanh(lo[:, 0:2 * LORA])
    ad = lo[:, 2 * LORA:4 * LORA]
    gd = _sigmoid(lo[:, 4 * LORA:])
    g = _dotb(gd, gup_ref[...])
    kx = k * kk_ref[...]
    ss = _dots(kx * kx, seg_ones, *PREC["seg"])
    kn = kx / jnp.maximum(jnp.sqrt(ss), 1e-12)
    zeros_up = jnp.zeros((LORA, LANES), F32)
    kd_sum = jnp.zeros_like(k)
    for d in range(2):
        if d == 0:
            wup = jnp.concatenate([wup_ref[0], zeros_up], axis=0)
            aup = jnp.concatenate([aup_ref[0], zeros_up], axis=0)
        else:
            wup = jnp.concatenate([zeros_up, wup_ref[1]], axis=0)
            aup = jnp.concatenate([zeros_up, aup_ref[1]], axis=0)
        zw = w0_ref[d:d + 1, :] + _dotb(wd, wup)
        w_log = -_softplus(-zw) - 0.5
        lw_s[d] = -jnp.exp(w_log)
        a_sig = _sigmoid(a0_ref[d:d + 1, :] + _dotb(ad, aup))
        kd = k * (1.0 + (a_sig - 1.0) * ka_ref[...])
        kd_s[d] = kd
        b_s[d] = kn * a_sig
        kd_sum = kd_sum + kd
    r_s[...] = r
    v_s[...] = v
    kn_s[...] = kn
    bonus = _dots(r * kd_sum * rk_ref[...], seg_ones, *PREC["seg"]) * v

    if has_state:
        rj = lax.broadcasted_iota(jnp.int32, (LANES, HEAD), 0)
        cj = lax.broadcasted_iota(jnp.int32, (LANES, HEAD), 1)
        for d in range(2):
            st = jnp.zeros((LANES, LANES), F32)
            for j in range(2):
                place = (rj == cj + j * HEAD).astype(F32)
                t1 = _dots(place, s0_ref[0, d, j], *PREC["place"], nt=True)
                st = st + _dots(t1, place, *PREC["place"][::-1], nt=True)
            st_s[d] = st
    else:
        st_s[...] = jnp.zeros((2, LANES, LANES), F32)

    tr = lax.broadcasted_iota(jnp.int32, (C, C), 0)
    tc = lax.broadcasted_iota(jnp.int32, (C, C), 1)
    sr = lax.broadcasted_iota(jnp.int32, (2 * C, 2 * C), 0)
    sc = lax.broadcasted_iota(jnp.int32, (2 * C, 2 * C), 1)
    t_idx = sr % C
    s_idx = sc % C
    incl = (sr >= C).astype(jnp.int32)
    lane_c = lax.broadcasted_iota(jnp.int32, (C, LANES), 1)
    lo_half = lane_c < HEAD
    zero_cl = jnp.zeros((C, LANES), F32)
    tris = ((tc <= tr).astype(F32), (tc >= tr).astype(F32))
    eye_c = (tr == tc).astype(F32)
    txs = tr ^ tc
    lvl_map = sum(((txs >= (1 << bit)).astype(jnp.int32) for bit in range(1, 6)),
                  jnp.where(txs == 0, -1, 0))
    sc_masks = (s_idx < t_idx + incl, s_idx > t_idx - incl)

    def chunk_algebra(chunks):
        units = [(c, d) for c in chunks for d in range(2)]
        u = []
        for c, d in units:
            sl = pl.ds(c * C, C) if isinstance(c, int) else pl.ds(pl.multiple_of(c * C, C), C)
            lw = lw_s[d, sl, :]
            u.append(dict(c=c, d=d, sl=sl, lw=lw, cl=_dots(tris[d], lw, *PREC["cum"])))
        for q in u:
            d, sl, cl = q["d"], q["sl"], q["cl"]
            kd = kd_s[d, sl, :]
            b = b_s[d, sl, :]
            tot = cl[C - 1:C, :] if d == 0 else cl[0:1, :]
            q["at"] = -kn_s[sl, :] * jnp.exp(cl - q["lw"])
            q["rt"] = r_s[sl, :] * jnp.exp(cl)
            e_inv = jnp.exp(-cl)
            e_end = jnp.exp(tot - cl)
            q["rhs_s"] = jnp.concatenate([b * e_inv, kd * e_inv], axis=0)
            q["lhs_t"] = jnp.concatenate([b * e_end, kd * e_end], axis=0).T
            q["p_c"] = jnp.exp(tot)
            q["vc"] = v_s[sl, :]
        heads = [(q, j) for q in u for j in range(2)]
        h = []
        for q, j in heads:
            hm = head0 if j == 0 else jnp.logical_not(head0)
            lhs = jnp.concatenate([jnp.where(hm, q["at"], 0.0), jnp.where(hm, q["rt"], 0.0)], axis=0)
            scm = jnp.where(sc_masks[q["d"]], _dots(lhs, q["rhs_s"], *PREC["score"], nt=True), 0.0)
            h.append(dict(top=scm[:C], bot=scm[C:]))
        for (q, j), e in zip(heads, h):
            lak = jnp.where(lo_half, 0.0, e["top"])
            e["lab"] = e["top"][:, :C]
            e["xv"] = _dots(lak, jnp.concatenate([q["vc"], q["vc"]], axis=0), *PREC["lv"])
            e["t"] = eye_c + jnp.where(lvl_map == 0, e["lab"], 0.0)
        for lvl in range(1, 6):
            for e in h:
                e["w"] = _dots(jnp.where(lvl_map == lvl, e["lab"], 0.0), e["t"], *PREC["lvl"])
            for e in h:
                e["t"] = e["t"] + _dots(e["t"], e["w"], *PREC["lvl"])
        for (q, j), e in zip(heads, h):
            x = _dots(e["t"], jnp.concatenate([q["at"], e["xv"]], axis=1), *PREC["app"])
            e["xa"] = x[:, :LANES]
            e["xv"] = x[:, LANES:]
        for i, q in enumerate(u):
            e0, e1 = h[2 * i], h[2 * i + 1]
            abar = jnp.where(head0, e0["xa"], e1["xa"])
            ubar = jnp.where(head0, e0["xv"], e1["xv"])
            q["rhs2"] = jnp.concatenate([jnp.concatenate([abar, ubar], axis=1),
                                         jnp.concatenate([zero_cl, q["vc"]], axis=1)], axis=0)
            q["o2"] = _dots(jnp.concatenate([e0["bot"], e1["bot"]], axis=0), q["rhs2"], *PREC["o2"])
        for q in u:
            q["gh"] = _dots(q["lhs_t"], q["rhs2"], *PREC["gh"])
        for q in u:
            c, d, sl, o2, gh = q["c"], q["d"], q["sl"], q["o2"], q["gh"]
            rb_s[d, sl, :] = q["rt"] + jnp.where(head0, o2[:C, :LANES], o2[C:, :LANES])
            yb_s[d, sl, :] = jnp.where(head0, o2[:C, LANES:], o2[C:, LANES:])
            gt_s[d, c] = jnp.where(same_head, gh[:, :LANES], 0.0) + eye * q["p_c"]
            ht_s[d, c] = jnp.where(same_head, gh[:, LANES:], 0.0)

    group = min(GROUP, nc)

    def algebra_step(i, carry):
        chunk_algebra([i * group + gidx for gidx in range(group)])
        return carry

    if nc == group:
        algebra_step(0, 0)
    else:
        lax.fori_loop(0, nc // group, algebra_step, 0)

    y_ref[0] = jnp.zeros((seq, LANES), F32)

    def state_step(i, carry):
        for d in range(2):
            c = i if d == 0 else nc - 1 - i
            sl = pl.ds(pl.multiple_of(c * C, C), C)
            st = st_s[d]
            y_ref[0, sl, :] += _dots(rb_s[d, sl, :], st, *PREC["y"]) + yb_s[d, sl, :]
            st_s[d] = _dots(gt_s[d, c], st, *PREC["st"]) + ht_s[d, c]
        return carry

    lax.fori_loop(0, nc, state_step, 0)

    rj2 = lax.broadcasted_iota(jnp.int32, (HEAD, LANES), 0)
    cj2 = lax.broadcasted_iota(jnp.int32, (HEAD, LANES), 1)
    for d in range(2):
        stt = st_s[d].T
        for j in range(2):
            sel = (cj2 == rj2 + j * HEAD).astype(F32)
            sfin_ref[0, d, j] = _dots(_dots(sel, stt, *PREC["place"]), sel, *PREC["place"][::-1], nt=True)

    y = y_ref[0]
    mean = _dots(y, seg_ones, *PREC["seg"]) * (1.0 / HEAD)
    yc = y - mean
    var = _dots(yc * yc, seg_ones, *PREC["seg"]) * (1.0 / HEAD)
    yn = yc * lax.rsqrt(var + GN_EPS) * lnw_ref[...] + lnb_ref[...]
    y_ref[0] = (yn + bonus) * g


def _rwkv(pa, s0, mu, w0, w_up, a0, a_up, g_up, k_k, k_a, r_k, ln_w, ln_b):
    bsz, seq, _ = pa.shape
    has_state = s0 is not None
    mu2 = mu.reshape(1, SHIFT_END)
    row = lambda a: a.reshape(1, D_RWKV)
    col = lambda o: (lambda b, p: (b, 0, o + p))
    vec = lambda b, p: (0, p)
    in_specs = [
        pl.BlockSpec((1, seq, LANES), col(0)),
        pl.BlockSpec((1, seq, LANES), col(N_PAIRS)),
        pl.BlockSpec((1, seq, LANES), col(2 * N_PAIRS)),
        pl.BlockSpec((1, seq, LORA_W), lambda b, p: (b, 0, 3 * D_RWKV // LORA_W)),
        pl.BlockSpec((1, LANES), lambda b, p: (0, p)),
        pl.BlockSpec((1, LANES), lambda b, p: (0, N_PAIRS + p)),
        pl.BlockSpec((1, LANES), lambda b, p: (0, 2 * N_PAIRS + p)),
        pl.BlockSpec((1, LORA_W), lambda b, p: (0, 3 * D_RWKV // LORA_W)),
        pl.BlockSpec((2, LANES), vec),
        pl.BlockSpec((2, LORA, LANES), lambda b, p: (0, 0, p)),
        pl.BlockSpec((2, LANES), vec),
        pl.BlockSpec((2, LORA, LANES), lambda b, p: (0, 0, p)),
        pl.BlockSpec((GATE_LORA, LANES), vec),
        pl.BlockSpec((1, LANES), vec),
        pl.BlockSpec((1, LANES), vec),
        pl.BlockSpec((1, LANES), vec),
        pl.BlockSpec((1, LANES), vec),
        pl.BlockSpec((1, LANES), vec),
    ]
    args = [pa, pa, pa, pa, mu2, mu2, mu2, mu2, w0, w_up, a0, a_up, g_up, row(k_k), row(k_a),
            row(r_k), row(ln_w), row(ln_b)]
    state_spec = pl.BlockSpec((1, 2, 2, HEAD, HEAD), lambda b, p: (b, 0, p, 0, 0))
    if has_state:
        in_specs.append(state_spec)
        args.append(s0)
    return pl.pallas_call(
        functools.partial(_rwkv_kernel, seq, has_state),
        out_shape=(jax.ShapeDtypeStruct((bsz, seq, D_RWKV), F32),
                   jax.ShapeDtypeStruct((bsz, 2, N_HEADS, HEAD, HEAD), F32)),
        grid=(bsz, N_PAIRS),
        in_specs=in_specs,
        out_specs=(pl.BlockSpec((1, seq, LANES), lambda b, p: (b, 0, p)), state_spec),
        scratch_shapes=[
            pltpu.VMEM((2, seq, LANES), F32),
            pltpu.VMEM((2, seq, LANES), F32),
            pltpu.VMEM((2, seq, LANES), F32),
            pltpu.VMEM((seq, LANES), F32),
            pltpu.VMEM((seq, LANES), F32),
            pltpu.VMEM((seq, LANES), F32),
            pltpu.VMEM((2, LANES, LANES), F32),
            pltpu.VMEM((2, seq, LANES), F32),
            pltpu.VMEM((2, seq, LANES), F32),
            pltpu.VMEM((2, seq // CHUNK, LANES, LANES), F32),
            pltpu.VMEM((2, seq // CHUNK, LANES, LANES), F32),
        ],
        compiler_params=pltpu.CompilerParams(
            dimension_semantics=("arbitrary", "arbitrary"), vmem_limit_bytes=VMEM_LIMIT),
        name="rwkv",
    )(*args)


def _fpool_kernel(seq, f_ref, p_ref, cs_ref, c64_ref, s64_ref, wf_ref, band_ref, inv_ref, wp_ref,
                  ps_ref, yb_ref, yc_ref):
    f = f_ref[0]
    wf = wf_ref[...]
    g1 = _dotf(f, _dotf(c64_ref[...], wf))
    g2 = _dotf(f, _dotf(s64_ref[...], wf))
    yb_ref[0] = _dotf(cs_ref[...], jnp.concatenate([g1, -g2], axis=0))

    p = p_ref[0]
    p_hi = p.astype(BF16)
    p_lo = (p - p_hi.astype(F32)).astype(BF16)
    lane = lax.broadcasted_iota(jnp.int32, (seq, D_POOL), 1)
    d = jnp.zeros((seq, D_POOL), F32)
    for i in range(len(POOL_WINDOWS)):
        band = band_ref[i]
        s = (jnp.dot(band, p_hi, preferred_element_type=F32)
             + jnp.dot(band, p_lo, preferred_element_type=F32))
        d = jnp.where((lane // POOL_GW) == i, s, d)
    d = d * inv_ref[...] - p
    yc_ref[0] = _dotb(d, wp_ref[...]) * ps_ref[...]


def _fpool(pf, pp, consts, wf_bd, wp_bd, pool_scale):
    bsz, seq, _ = pf.shape
    cs, c64, s64, band, inv = consts
    full2 = lambda b: (0, 0)
    return pl.pallas_call(
        functools.partial(_fpool_kernel, seq),
        out_shape=(jax.ShapeDtypeStruct((bsz, seq, D_FNET), F32),
                   jax.ShapeDtypeStruct((bsz, seq, D_POOL), F32)),
        grid=(bsz,),
        in_specs=[
            pl.BlockSpec((1, seq, D_FNET), lambda b: (b, 0, 0)),
            pl.BlockSpec((1, seq, D_POOL), lambda b: (b, 0, 0)),
            pl.BlockSpec((seq, 2 * seq), full2),
            pl.BlockSpec((D_FNET, D_FNET), full2),
            pl.BlockSpec((D_FNET, D_FNET), full2),
            pl.BlockSpec((D_FNET, D_FNET), full2),
            pl.BlockSpec((len(POOL_WINDOWS), seq, seq), lambda b: (0, 0, 0)),
            pl.BlockSpec((seq, D_POOL), full2),
            pl.BlockSpec((D_POOL, D_POOL), full2),
            pl.BlockSpec((1, D_POOL), full2),
        ],
        out_specs=(pl.BlockSpec((1, seq, D_FNET), lambda b: (b, 0, 0)),
                   pl.BlockSpec((1, seq, D_POOL), lambda b: (b, 0, 0))),
        compiler_params=pltpu.CompilerParams(
            dimension_semantics=("arbitrary",), vmem_limit_bytes=VMEM_LIMIT),
        name="fpool",
    )(pf, pp, cs, c64, s64, wf_bd, band, inv, wp_bd, pool_scale.reshape(1, D_POOL))


def _window_matrix(n, w):
    t = np.arange(n)
    lo = np.clip(t - w // 2, 0, n)
    hi = np.clip(t + w - w // 2, 0, n)
    s = np.arange(n)
    return ((s[None, :] >= lo[:, None]) & (s[None, :] < hi[:, None])).astype(np.float64)


def _mixer_constants(seq, grid):
    n = np.arange(seq)
    ang = 2.0 * np.pi * ((n[:, None] * n[None, :]) % seq) / seq
    scale = 1.0 / math.sqrt(seq * FNET_GW)
    cs = np.concatenate([np.cos(ang), np.sin(ang)], axis=1) * scale
    m = np.arange(FNET_GW)
    ang64 = 2.0 * np.pi * ((m[:, None] * m[None, :]) % FNET_GW) / FNET_GW
    groups = D_FNET // FNET_GW
    c64 = np.kron(np.eye(groups), np.cos(ang64))
    s64 = np.kron(np.eye(groups), np.sin(ang64))
    bands = []
    invs = []
    for w in POOL_WINDOWS:
        if grid:
            rows = seq // GRID_W
            band = np.kron(_window_matrix(rows, w), _window_matrix(GRID_W, w))
        else:
            band = _window_matrix(seq, w)
        bands.append(band)
        invs.append(np.repeat((1.0 / band.sum(axis=1))[:, None], POOL_GW, axis=1))
    band = jnp.asarray(np.stack(bands), dtype=BF16)
    inv = jnp.asarray(np.concatenate(invs, axis=1), dtype=F32)
    return (jnp.asarray(cs, dtype=F32), jnp.asarray(c64, dtype=F32), jnp.asarray(s64, dtype=F32),
            band, inv)


def _block_diag(w):
    g, n, _ = w.shape
    out = jnp.zeros((g * n, g * n), w.dtype)
    for i in range(g):
        out = out.at[i * n:(i + 1) * n, i * n:(i + 1) * n].set(w[i])
    return out


def _post_kernel(final, x_ref, ya_ref, yb_ref, yc_ref, mod_ref, wo_ref, g2_ref, w1_ref, w2_ref,
                 fg_ref, o_ref):
    mod = mod_ref[0]
    gate1 = mod[:, 2 * D_MODEL:3 * D_MODEL]
    shift2 = mod[:, 3 * D_MODEL:4 * D_MODEL]
    scale2 = mod[:, 4 * D_MODEL:5 * D_MODEL]
    gate2 = mod[:, 5 * D_MODEL:]
    mix = (jnp.dot(ya_ref[...].astype(BF16), wo_ref[0:D_RWKV, :], preferred_element_type=F32)
           + jnp.dot(yb_ref[...].astype(BF16), wo_ref[D_RWKV:D_RWKV + D_FNET, :],
                     preferred_element_type=F32)
           + jnp.dot(yc_ref[...].astype(BF16), wo_ref[D_RWKV + D_FNET:, :],
                     preferred_element_type=F32))
    x1 = x_ref[...] + gate1 * mix
    h2 = (_rms(x1, g2_ref[...]) * (1.0 + scale2) + shift2).astype(BF16)
    tf = 1024
    acc = jnp.zeros_like(x1)
    for j in range(D_FF // tf):
        ff = jnp.dot(h2, w1_ref[:, j * tf:(j + 1) * tf], preferred_element_type=F32)
        ff = jnp.square(jnp.maximum(ff, 0.0))
        acc = acc + jnp.dot(ff.astype(BF16), w2_ref[j * tf:(j + 1) * tf, :],
                            preferred_element_type=F32)
    x2 = x1 + gate2 * acc
    if final:
        x2 = _rms(x2, fg_ref[...])
    o_ref[...] = x2


def _post(x, ya, yb, yc, mod, mod_row, w_out_b, g2, w1_b, w2_b, fg, final, tm):
    t = x.shape[0]
    tok = lambda i: (i, 0)
    full = lambda i: (0, 0)
    return pl.pallas_call(
        functools.partial(_post_kernel, final),
        out_shape=jax.ShapeDtypeStruct((t, D_MODEL), F32),
        grid=(t // tm,),
        in_specs=[
            pl.BlockSpec((tm, D_MODEL), tok),
            pl.BlockSpec((tm, D_RWKV), tok),
            pl.BlockSpec((tm, D_FNET), tok),
            pl.BlockSpec((tm, D_POOL), tok),
            pl.BlockSpec((1, 1, 6 * D_MODEL), lambda i: (mod_row(i), 0, 0)),
            pl.BlockSpec((D_MODEL, D_MODEL), full),
            pl.BlockSpec((1, D_MODEL), full),
            pl.BlockSpec((D_MODEL, D_FF), full),
            pl.BlockSpec((D_FF, D_MODEL), full),
            pl.BlockSpec((1, D_MODEL), full),
        ],
        out_specs=pl.BlockSpec((tm, D_MODEL), tok),
        compiler_params=pltpu.CompilerParams(
            dimension_semantics=("arbitrary",), vmem_limit_bytes=VMEM_LIMIT),
        name="post",
    )(x, ya, yb, yc, mod, w_out_b, g2.reshape(1, D_MODEL), w1_b, w2_b, fg.reshape(1, D_MODEL))


def _layer(x, bsz, seq, mod, mod_row, s0, consts, lw, final, fg, tm):
    (norm1_g, w_in_b, mu_shift, w0, w_up, a0, a_up, g_up, k_k, k_a, r_k, ln_w, ln_b, wf_bd, wp_bd,
     pool_scale, w_out_b, norm2_g, w1_b, w2_b) = lw
    pa, pf, pp = _pre(x, mod, mod_row, norm1_g, w_in_b, tm)
    ya, s_fin = _rwkv(pa.reshape(bsz, seq, SHIFT_END), s0, mu_shift, w0, w_up, a0, a_up, g_up,
                      k_k, k_a, r_k, ln_w, ln_b)
    yb, yc = _fpool(pf.reshape(bsz, seq, D_FNET), pp.reshape(bsz, seq, D_POOL), consts, wf_bd,
                    wp_bd, pool_scale)
    x = _post(x, ya.reshape(bsz * seq, D_RWKV), yb.reshape(bsz * seq, D_FNET),
              yc.reshape(bsz * seq, D_POOL), mod, mod_row, w_out_b, norm2_g, w1_b, w2_b, fg, final, tm)
    return x, s_fin


def kernel(x_prompt, x_sample, state_wkv, c, c_ctx, w_ada, b_ada, norm1_g, w_in, mu_shift, w0, w_up, a0, a_up, g_up, k_k, k_a, r_k, ln_x_w, ln_x_b, w_fnet, w_pool, pool_scale, w_out, norm2_g, w_ff1, w_ff2, final_norm_g):
    bp, lp, _ = x_prompt.shape
    bs, ls, _ = x_sample.shape
    tm = 512
    n_rows = 8
    cond = jnp.zeros((n_rows, D_MODEL), F32).at[0].set(c_ctx).at[1:1 + bs].set(c)
    mod_all = _ada(cond, w_ada, b_ada)
    consts_p = _mixer_constants(lp, False)
    consts_s = _mixer_constants(ls, True)
    xp = x_prompt.reshape(bp * lp, D_MODEL)
    xs = x_sample.reshape(bs * ls, D_MODEL)
    tiles_per_seq = ls // tm
    row_p = lambda i: 0
    row_s = lambda i: 1 + i // tiles_per_seq
    new_states = []
    for l in range(DEPTH):
        lw = (norm1_g[l], w_in[l].astype(BF16), mu_shift[l], w0[l], w_up[l], a0[l], a_up[l], g_up[l],
              k_k[l], k_a[l], r_k[l], ln_x_w[l], ln_x_b[l], _block_diag(w_fnet[l]),
              _block_diag(w_pool[l]), pool_scale[l], w_out[l].astype(BF16), norm2_g[l],
              w_ff1[l].astype(BF16), w_ff2[l].astype(BF16))
        mod = mod_all[l].reshape(n_rows, 1, 6 * D_MODEL)
        final = l == DEPTH - 1
        xp, s_ctx = _layer(xp, bp, lp, mod, row_p, None, consts_p, lw, final, final_norm_g, tm)
        new_states.append(s_ctx)
        xs, _ = _layer(xs, bs, ls, mod, row_s, state_wkv[:, l], consts_s, lw, final, final_norm_g, tm)
    new_state_wkv = jnp.stack(new_states, axis=1)
    return (xp.reshape(bp, lp, D_MODEL), xs.reshape(bs, ls, D_MODEL), new_state_wkv)
```

```python
import functools
import math

import numpy as np
import jax
import jax.numpy as jnp
from jax import lax
from jax.experimental import pallas as pl
from jax.experimental.pallas import tpu as pltpu

F32 = jnp.float32
BF16 = jnp.bfloat16

D_MODEL = 1024
DEPTH = 2
GRID_W = 64
D_RWKV = 512
HEAD = 64
N_HEADS = 8
LANES = 128
N_PAIRS = D_RWKV // LANES
D_FNET = 256
FNET_GW = 64
D_POOL = 256
POOL_GW = 64
POOL_WINDOWS = (2, 4, 8, 16)
LORA = 64
GATE_LORA = 128
SHIFT_END = 3 * D_RWKV + 2 * LORA + 2 * LORA + GATE_LORA
LORA_W = SHIFT_END - 3 * D_RWKV
D_IN = SHIFT_END + D_FNET + D_POOL
D_FF = 4 * D_MODEL
RMS_EPS = 1e-6
GN_EPS = 64e-5
LOG2E = 1.4426950408889634
CHUNK = 64
VMEM_LIMIT = 56 * 1024 * 1024
STREAM_UNITS = 16
STREAM_LAG = 8
ITER_UNITS = 32


def _dotf(a, b):
    return jnp.dot(a, b, preferred_element_type=F32, precision=lax.Precision.HIGHEST)


def _split(x, n):
    terms = []
    for _ in range(n - 1):
        hi = x.astype(BF16)
        terms.append(hi)
        x = x - hi.astype(F32)
    terms.append(x.astype(BF16))
    return terms


def _dots(a, b, na, nb, nt=False):
    dims = (((1,), (1,)), ((), ())) if nt else (((1,), (0,)), ((), ()))
    ta = _split(a, na)
    tb = _split(b, nb)
    out = None
    for i in range(na):
        for j in range(nb):
            if i + j < max(na, nb):
                t = lax.dot_general(ta[i], tb[j], dims, preferred_element_type=F32)
                out = t if out is None else out + t
    return out


def _mm(a, b):
    return jnp.dot(a.astype(BF16), b.astype(BF16), preferred_element_type=F32)


def _sigmoid(x):
    return 1.0 / (1.0 + jnp.exp(-x))


def _softplus(x):
    return jnp.maximum(x, 0.0) + jnp.log(1.0 + jnp.exp(-jnp.abs(x)))


def _rms(x, g):
    ms = jnp.mean(x * x, axis=-1, keepdims=True)
    return x * lax.rsqrt(ms + RMS_EPS) * g


_DONE = object()


def _interleave(streams, lag):
    pending = list(streams)
    active = []
    step = 0
    while pending or active:
        if pending and step % lag == 0:
            active.append(pending.pop(0))
        for s in list(active):
            if next(s, _DONE) is _DONE:
                active.remove(s)
        step += 1


def _ada_kernel(c_ref, w_ref, b_ref, o_ref):
    c = c_ref[...]
    s = c * _sigmoid(c)
    o_ref[0] = _dotf(s, w_ref[0]) + b_ref[0]


def _ada(cond, w_ada, b_ada):
    tn = 1536
    nrow = cond.shape[0]
    return pl.pallas_call(
        _ada_kernel,
        out_shape=jax.ShapeDtypeStruct((DEPTH, nrow, 6 * D_MODEL), F32),
        grid=(DEPTH, 6 * D_MODEL // tn),
        in_specs=[
            pl.BlockSpec((nrow, D_MODEL), lambda l, j: (0, 0)),
            pl.BlockSpec((1, D_MODEL, tn), lambda l, j: (l, 0, j)),
            pl.BlockSpec((1, 1, tn), lambda l, j: (l, 0, j)),
        ],
        out_specs=pl.BlockSpec((1, nrow, tn), lambda l, j: (l, 0, j)),
        compiler_params=pltpu.CompilerParams(
            dimension_semantics=("arbitrary", "arbitrary"), vmem_limit_bytes=VMEM_LIMIT),
        name="ada",
    )(cond, w_ada, b_ada.reshape(DEPTH, 1, 6 * D_MODEL))


def _pre_kernel(x_ref, mod_ref, g_ref, w_ref, oa_ref, of_ref, op_ref):
    mod = mod_ref[0]
    shift = mod[:, 0:D_MODEL]
    scale = mod[:, D_MODEL:2 * D_MODEL]
    h = _rms(x_ref[...], g_ref[...]) * (1.0 + scale) + shift
    p = jnp.dot(h.astype(BF16), w_ref[...], preferred_element_type=F32)
    oa_ref[...] = p[:, :SHIFT_END]
    of_ref[...] = p[:, SHIFT_END:SHIFT_END + D_FNET]
    op_ref[...] = p[:, SHIFT_END + D_FNET:]


def _pre(x, mod, mod_row, g, w_in_b, tm):
    t = x.shape[0]
    return pl.pallas_call(
        _pre_kernel,
        out_shape=(jax.ShapeDtypeStruct((t, SHIFT_END), F32),
                   jax.ShapeDtypeStruct((t, D_FNET), F32),
                   jax.ShapeDtypeStruct((t, D_POOL), F32)),
        grid=(t // tm,),
        in_specs=[
            pl.BlockSpec((tm, D_MODEL), lambda i: (i, 0)),
            pl.BlockSpec((1, 1, 6 * D_MODEL), lambda i: (mod_row(i), 0, 0)),
            pl.BlockSpec((1, D_MODEL), lambda i: (0, 0)),
            pl.BlockSpec((D_MODEL, D_IN), lambda i: (0, 0)),
        ],
        out_specs=(pl.BlockSpec((tm, SHIFT_END), lambda i: (i, 0)),
                   pl.BlockSpec((tm, D_FNET), lambda i: (i, 0)),
                   pl.BlockSpec((tm, D_POOL), lambda i: (i, 0))),
        compiler_params=pltpu.CompilerParams(
            dimension_semantics=("arbitrary",), vmem_limit_bytes=VMEM_LIMIT),
        name="pre",
    )(x, mod, g.reshape(1, D_MODEL), w_in_b)


def _shift_mix(p, mu):
    n = p.shape[0]
    row = lax.broadcasted_iota(jnp.int32, p.shape, 0)
    prev = jnp.where(row == 0, 0.0, pltpu.roll(p, 1, 0))
    nxt = jnp.where(row == n - 1, 0.0, pltpu.roll(p, n - 1, 0))
    return p + mu * (0.5 * (prev + nxt) - p)


def _rwkv_kernel(seq, pp, has_state, want_state, *refs):
    (r_ref, k_ref, v_ref, lo_ref, mur_ref, muk_ref, muv_ref, mul_ref, w0_ref, wup_ref, a0_ref,
     aup_ref, gup_ref, kk_ref, ka_ref, rk_ref, lnw_ref, lnb_ref) = refs[:18]
    n_in = 18 + has_state
    s0_ref = refs[18] if has_state else None
    y_ref = refs[n_in]
    sfin_ref = refs[n_in + 1] if want_state else None
    lw_s, kd_s, b_s, r_s, v_s, kn_s, st_s, rb_s, yb_s, gt_s, ht_s = refs[n_in + 1 + want_state:]
    nc = seq // CHUNK
    C = CHUNK
    tiles = [slice(i * LANES, (i + 1) * LANES) for i in range(pp)]

    lane1 = lax.broadcasted_iota(jnp.int32, (1, LANES), 1)
    head0 = lane1 < HEAD
    head1 = jnp.logical_not(head0)
    ri = lax.broadcasted_iota(jnp.int32, (LANES, LANES), 0)
    ci = lax.broadcasted_iota(jnp.int32, (LANES, LANES), 1)
    same_head = (ri // HEAD) == (ci // HEAD)
    seg_ones = same_head.astype(F32)
    eye = (ri == ci).astype(F32)

    def seg_sum(x):
        return jnp.concatenate([_dots(x[:, t], seg_ones, 2, 1) for t in tiles], axis=1)

    r = _shift_mix(r_ref[0], mur_ref[...])
    k = _shift_mix(k_ref[0], muk_ref[...])
    v = _shift_mix(v_ref[0], muv_ref[...])
    lo = _shift_mix(lo_ref[0], mul_ref[...])
    wd = jnp.tanh(lo[:, 0:2 * LORA])
    ad = lo[:, 2 * LORA:4 * LORA]
    gd = _sigmoid(lo[:, 4 * LORA:])
    g = _mm(gd, gup_ref[...])
    kx = k * kk_ref[...]
    kn = kx / jnp.maximum(jnp.sqrt(seg_sum(kx * kx)), 1e-12)
    zeros_up = jnp.zeros((LORA, pp * LANES), F32)
    kd_sum = jnp.zeros_like(k)
    for d in range(2):
        if d == 0:
            wup = jnp.concatenate([wup_ref[0], zeros_up], axis=0)
            aup = jnp.concatenate([aup_ref[0], zeros_up], axis=0)
        else:
            wup = jnp.concatenate([zeros_up, wup_ref[1]], axis=0)
            aup = jnp.concatenate([zeros_up, aup_ref[1]], axis=0)
        zw = w0_ref[d:d + 1, :] + _mm(wd, wup)
        w_log = -_softplus(-zw) - 0.5
        lw2 = -jnp.exp(w_log) * LOG2E
        lw_s[d] = lw2
        a_sig = _sigmoid(a0_ref[d:d + 1, :] + _mm(ad, aup))
        kd = k * (1.0 + (a_sig - 1.0) * ka_ref[...])
        kd_s[d] = kd
        b_s[d] = kn * a_sig
        kn_s[d] = -kn * jnp.exp2(-lw2)
        kd_sum = kd_sum + kd
    r_s[...] = r
    v_s[...] = v
    bonus = seg_sum(r * kd_sum * rk_ref[...]) * v

    if has_state:
        rj = lax.broadcasted_iota(jnp.int32, (LANES, HEAD), 0)
        cj = lax.broadcasted_iota(jnp.int32, (LANES, HEAD), 1)
        place = [(rj == cj + j * HEAD).astype(F32) for j in range(2)]
        keys = [(d, pi, j) for d in range(2) for pi in range(pp) for j in range(2)]
        t1 = {key: _dots(place[key[2]], s0_ref[0, key[0], 2 * key[1] + key[2]], 1, 3, nt=True)
              for key in keys}
        t2 = {key: _dots(t1[key], place[key[2]], 3, 1, nt=True) for key in keys}
        for d in range(2):
            for pi in range(pp):
                st_s[d, pi] = t2[(d, pi, 0)] + t2[(d, pi, 1)]
    else:
        st_s[...] = jnp.zeros(st_s.shape, F32)

    tr = lax.broadcasted_iota(jnp.int32, (C, C), 0)
    tc = lax.broadcasted_iota(jnp.int32, (C, C), 1)
    t_p = lax.broadcasted_iota(jnp.int32, (C, LANES), 0)
    s_p = lax.broadcasted_iota(jnp.int32, (C, LANES), 1) % C
    tris = ((tc <= tr).astype(F32), (tc >= tr).astype(F32))
    strict = (s_p < t_p, s_p > t_p)
    inclusive = (s_p <= t_p, s_p >= t_p)
    eye_p = (s_p == t_p).astype(F32)
    txs = t_p ^ s_p
    lvl_map = sum(((txs >= (1 << bit)).astype(jnp.int32) for bit in range(1, 6)),
                  jnp.where(txs == 0, -1, 0))
    zero_b = jnp.zeros((C, LANES), BF16)

    def bd(x):
        return jnp.concatenate([jnp.where(head0, x, zero_b), jnp.where(head1, x, zero_b)], axis=0)

    def bdx(x):
        return jnp.concatenate([jnp.where(head1, x, zero_b), jnp.where(head0, x, zero_b)], axis=0)

    def mm(a, b):
        return jnp.dot(a, b, preferred_element_type=F32)

    def rows(c):
        return pl.ds(c * C, C) if isinstance(c, int) else pl.ds(pl.multiple_of(c * C, C), C)

    def algebra(units):
        u = []
        for c, pi, d in units:
            sl, ln = rows(c), tiles[pi]
            lw = lw_s[d, sl, ln]
            u.append(dict(c=c, pi=pi, d=d, sl=sl, ln=ln, lw=lw, cl=_dots(tris[d], lw, 1, 2)))
        yield
        for q in u:
            d, sl, ln, cl = q["d"], q["sl"], q["ln"], q["cl"]
            tot = cl[C - 1:C, :] if d == 0 else cl[0:1, :]
            e_in = jnp.exp2(cl)
            e_inv = jnp.exp2(-cl)
            p_c = jnp.exp2(tot)
            q["at"] = (kn_s[d, sl, ln] * e_in).astype(BF16)
            q["rt"] = r_s[sl, ln] * e_in
            bt = b_s[d, sl, ln] * e_inv
            kt = kd_s[d, sl, ln] * e_inv
            q["lhs_t"] = jnp.concatenate([bt * p_c, kt * p_c], axis=0).T.astype(BF16)
            bt = bt.astype(BF16)
            kt = kt.astype(BF16)
            q["rhs_nt"] = jnp.concatenate([jnp.where(head0, bt, zero_b), jnp.where(head1, bt, zero_b),
                                           jnp.where(head1, kt, zero_b), jnp.where(head0, kt, zero_b)], axis=0)
            q["p_c"] = p_c
            q["vc"] = v_s[sl, ln].astype(BF16)
            yield
        for q in u:
            lhs = jnp.concatenate([q["at"], q["rt"].astype(BF16)], axis=0)
            q["sc"] = lax.dot_general(lhs, q["rhs_nt"], (((1,), (1,)), ((), ())), preferred_element_type=F32)
        yield
        for q in u:
            d, sc = q["d"], q["sc"]
            q["lab"] = jnp.where(strict[d], sc[:C, :LANES], 0.0)
            lakx = jnp.where(strict[d], sc[:C, LANES:], 0.0)
            q["mr"] = jnp.concatenate([jnp.where(inclusive[d], sc[C:, :LANES], 0.0),
                                       jnp.where(inclusive[d], sc[C:, LANES:], 0.0)], axis=1).astype(BF16)
            q["xv"] = mm(lakx.astype(BF16), bdx(q["vc"]))
            q["t"] = eye_p + jnp.where(lvl_map == 0, q["lab"], 0.0)
        yield
        for lvl in range(1, 6):
            for q in u:
                q["tb"] = q["t"].astype(BF16)
                q["w"] = mm(jnp.where(lvl_map == lvl, q["lab"], 0.0).astype(BF16), bd(q["tb"]))
            yield
            for q in u:
                q["t"] = q["t"] + mm(q["tb"], bd(q["w"].astype(BF16)))
            yield
        for q in u:
            x = mm(q["t"].astype(BF16),
                   jnp.concatenate([bd(q["at"]), bd(q["xv"].astype(BF16))], axis=1))
            q["ab"] = x[:, :LANES].astype(BF16)
            q["ub"] = x[:, LANES:].astype(BF16)
        yield
        for q in u:
            rhs = jnp.concatenate([jnp.concatenate([bd(q["ab"]), bd(q["ub"])], axis=1),
                                   jnp.concatenate([jnp.concatenate([zero_b, zero_b], axis=0), bdx(q["vc"])],
                                                   axis=1)], axis=0)
            q["o2"] = mm(q["mr"], rhs)
            rhs2 = jnp.concatenate([jnp.concatenate([q["ab"], q["ub"]], axis=1),
                                    jnp.concatenate([zero_b, q["vc"]], axis=1)], axis=0)
            q["gh"] = mm(q["lhs_t"], rhs2)
        yield
        for q in u:
            c, pi, d, sl, ln, o2, gh = q["c"], q["pi"], q["d"], q["sl"], q["ln"], q["o2"], q["gh"]
            rb_s[d, sl, ln] = (q["rt"] + o2[:, :LANES]).astype(BF16)
            yb_s[d, sl, ln] = o2[:, LANES:]
            gt_s[d, pi, c] = (jnp.where(same_head, gh[:, :LANES], 0.0) + eye * q["p_c"]).astype(BF16)
            ht_s[d, pi, c] = jnp.where(same_head, gh[:, LANES:], 0.0)
        yield

    group = min(nc, max(1, ITER_UNITS // (2 * pp)))

    def algebra_step(i, carry):
        units = [(i * group + gi, pi, d) for gi in range(group) for pi in range(pp) for d in range(2)]
        streams = [algebra(units[s:s + STREAM_UNITS]) for s in range(0, len(units), STREAM_UNITS)]
        _interleave(streams, STREAM_LAG)
        return carry

    if nc == group:
        algebra_step(0, 0)
    else:
        lax.fori_loop(0, nc // group, algebra_step, 0)

    y_ref[0] = jnp.zeros(y_ref.shape[1:], F32)

    def state_step(i, carry):
        items = []
        for pi in range(pp):
            for d in range(2):
                c = i if d == 0 else nc - 1 - i
                sl, ln = rows(c), tiles[pi]
                lhs = jnp.concatenate([rb_s[d, sl, ln], gt_s[d, pi, c]], axis=0)
                items.append((pi, d, c, sl, ln, mm(lhs, st_s[d, pi].astype(BF16))))
        for pi, d, c, sl, ln, o in items:
            y_ref[0, sl, ln] += o[:C] + yb_s[d, sl, ln]
            st_s[d, pi] = o[C:] + ht_s[d, pi, c]
        return carry

    lax.fori_loop(0, nc, state_step, 0)

    if want_state:
        rj2 = lax.broadcasted_iota(jnp.int32, (HEAD, LANES), 0)
        cj2 = lax.broadcasted_iota(jnp.int32, (HEAD, LANES), 1)
        sel = [(cj2 == rj2 + j * HEAD).astype(F32) for j in range(2)]
        keys = [(d, pi, j) for d in range(2) for pi in range(pp) for j in range(2)]
        stt = {(d, pi): st_s[d, pi].T for d in range(2) for pi in range(pp)}
        t1 = {key: _dots(sel[key[2]], stt[key[:2]], 1, 3) for key in keys}
        for d, pi, j in keys:
            sfin_ref[0, d, 2 * pi + j] = _dots(t1[(d, pi, j)], sel[j], 3, 1, nt=True)

    y = y_ref[0]
    yc = y - seg_sum(y) * (1.0 / HEAD)
    var = seg_sum(yc * yc) * (1.0 / HEAD)
    yn = yc * lax.rsqrt(var + GN_EPS) * lnw_ref[...] + lnb_ref[...]
    y_ref[0] = (yn + bonus) * g


def _rwkv(pa, s0, want_state, pp, mu, w0, w_up, a0, a_up, g_up, k_k, k_a, r_k, ln_w, ln_b):
    bsz, seq, _ = pa.shape
    has_state = s0 is not None
    width = pp * LANES
    nblk = D_RWKV // width
    nc = seq // CHUNK
    mu2 = mu.reshape(1, SHIFT_END)
    row = lambda a: a.reshape(1, D_RWKV)
    col = lambda o: (lambda b, p: (b, 0, o * nblk + p))
    vec = lambda b, p: (0, p)
    lora_blk = 3 * D_RWKV // LORA_W
    in_specs = [
        pl.BlockSpec((1, seq, width), col(0)),
        pl.BlockSpec((1, seq, width), col(1)),
        pl.BlockSpec((1, seq, width), col(2)),
        pl.BlockSpec((1, seq, LORA_W), lambda b, p: (b, 0, lora_blk)),
        pl.BlockSpec((1, width), lambda b, p: (0, p)),
        pl.BlockSpec((1, width), lambda b, p: (0, nblk + p)),
        pl.BlockSpec((1, width), lambda b, p: (0, 2 * nblk + p)),
        pl.BlockSpec((1, LORA_W), lambda b, p: (0, lora_blk)),
        pl.BlockSpec((2, width), vec),
        pl.BlockSpec((2, LORA, width), lambda b, p: (0, 0, p)),
        pl.BlockSpec((2, width), vec),
        pl.BlockSpec((2, LORA, width), lambda b, p: (0, 0, p)),
        pl.BlockSpec((GATE_LORA, width), vec),
        pl.BlockSpec((1, width), vec),
        pl.BlockSpec((1, width), vec),
        pl.BlockSpec((1, width), vec),
        pl.BlockSpec((1, width), vec),
        pl.BlockSpec((1, width), vec),
    ]
    args = [pa, pa, pa, pa, mu2, mu2, mu2, mu2, w0, w_up, a0, a_up, g_up, row(k_k), row(k_a),
            row(r_k), row(ln_w), row(ln_b)]
    state_spec = pl.BlockSpec((1, 2, 2 * pp, HEAD, HEAD), lambda b, p: (b, 0, p, 0, 0))
    if has_state:
        in_specs.append(state_spec)
        args.append(s0)
    out_shape = [jax.ShapeDtypeStruct((bsz, seq, D_RWKV), F32)]
    out_specs = [pl.BlockSpec((1, seq, width), lambda b, p: (b, 0, p))]
    if want_state:
        out_shape.append(jax.ShapeDtypeStruct((bsz, 2, N_HEADS, HEAD, HEAD), F32))
        out_specs.append(state_spec)
    outs = pl.pallas_call(
        functools.partial(_rwkv_kernel, seq, pp, has_state, want_state),
        out_shape=tuple(out_shape),
        grid=(bsz, nblk),
        in_specs=in_specs,
        out_specs=tuple(out_specs),
        scratch_shapes=[
            pltpu.VMEM((2, seq, width), F32),
            pltpu.VMEM((2, seq, width), F32),
            pltpu.VMEM((2, seq, width), F32),
            pltpu.VMEM((seq, width), F32),
            pltpu.VMEM((seq, width), F32),
            pltpu.VMEM((2, seq, width), F32),
            pltpu.VMEM((2, pp, LANES, LANES), F32),
            pltpu.VMEM((2, seq, width), BF16),
            pltpu.VMEM((2, seq, width), F32),
            pltpu.VMEM((2, pp, nc, LANES, LANES), BF16),
            pltpu.VMEM((2, pp, nc, LANES, LANES), F32),
        ],
        compiler_params=pltpu.CompilerParams(
            dimension_semantics=("arbitrary", "arbitrary"), vmem_limit_bytes=VMEM_LIMIT),
        name="rwkv",
    )(*args)
    return (outs[0], outs[1]) if want_state else (outs[0], None)


def _fpool_kernel(seq, f_ref, p_ref, cs_ref, c64_ref, s64_ref, wf_ref, band_ref, inv_ref, wp_ref,
                  ps_ref, yb_ref, yc_ref):
    f = f_ref[0]
    wf = wf_ref[...]
    g1 = _dotf(f, _dotf(c64_ref[...], wf))
    g2 = _dotf(f, _dotf(s64_ref[...], wf))
    yb_ref[0] = _dotf(cs_ref[...], jnp.concatenate([g1, -g2], axis=0))

    p = p_ref[0]
    p_hi = p.astype(BF16)
    p_lo = (p - p_hi.astype(F32)).astype(BF16)
    lane = lax.broadcasted_iota(jnp.int32, (seq, D_POOL), 1)
    d = jnp.zeros((seq, D_POOL), F32)
    for i in range(len(POOL_WINDOWS)):
        band = band_ref[i]
        s = (jnp.dot(band, p_hi, preferred_element_type=F32)
             + jnp.dot(band, p_lo, preferred_element_type=F32))
        d = jnp.where((lane // POOL_GW) == i, s, d)
    d = d * inv_ref[...] - p
    yc_ref[0] = _mm(d, wp_ref[...]) * ps_ref[...]


def _fpool(pf, pp, consts, wf_bd, wp_bd, pool_scale):
    bsz, seq, _ = pf.shape
    cs, c64, s64, band, inv = consts
    full2 = lambda b: (0, 0)
    return pl.pallas_call(
        functools.partial(_fpool_kernel, seq),
        out_shape=(jax.ShapeDtypeStruct((bsz, seq, D_FNET), F32),
                   jax.ShapeDtypeStruct((bsz, seq, D_POOL), F32)),
        grid=(bsz,),
        in_specs=[
            pl.BlockSpec((1, seq, D_FNET), lambda b: (b, 0, 0)),
            pl.BlockSpec((1, seq, D_POOL), lambda b: (b, 0, 0)),
            pl.BlockSpec((seq, 2 * seq), full2),
            pl.BlockSpec((D_FNET, D_FNET), full2),
            pl.BlockSpec((D_FNET, D_FNET), full2),
            pl.BlockSpec((D_FNET, D_FNET), full2),
            pl.BlockSpec((len(POOL_WINDOWS), seq, seq), lambda b: (0, 0, 0)),
            pl.BlockSpec((seq, D_POOL), full2),
            pl.BlockSpec((D_POOL, D_POOL), full2),
            pl.BlockSpec((1, D_POOL), full2),
        ],
        out_specs=(pl.BlockSpec((1, seq, D_FNET), lambda b: (b, 0, 0)),
                   pl.BlockSpec((1, seq, D_POOL), lambda b: (b, 0, 0))),
        compiler_params=pltpu.CompilerParams(
            dimension_semantics=("arbitrary",), vmem_limit_bytes=VMEM_LIMIT),
        name="fpool",
    )(pf, pp, cs, c64, s64, wf_bd, band, inv, wp_bd, pool_scale.reshape(1, D_POOL))


def _window_matrix(n, w):
    t = np.arange(n)
    lo = np.clip(t - w // 2, 0, n)
    hi = np.clip(t + w - w // 2, 0, n)
    s = np.arange(n)
    return ((s[None, :] >= lo[:, None]) & (s[None, :] < hi[:, None])).astype(np.float64)


def _mixer_constants(seq, grid):
    n = np.arange(seq)
    ang = 2.0 * np.pi * ((n[:, None] * n[None, :]) % seq) / seq
    scale = 1.0 / math.sqrt(seq * FNET_GW)
    cs = np.concatenate([np.cos(ang), np.sin(ang)], axis=1) * scale
    m = np.arange(FNET_GW)
    ang64 = 2.0 * np.pi * ((m[:, None] * m[None, :]) % FNET_GW) / FNET_GW
    groups = D_FNET // FNET_GW
    c64 = np.kron(np.eye(groups), np.cos(ang64))
    s64 = np.kron(np.eye(groups), np.sin(ang64))
    bands = []
    invs = []
    for w in POOL_WINDOWS:
        if grid:
            rows = seq // GRID_W
            band = np.kron(_window_matrix(rows, w), _window_matrix(GRID_W, w))
        else:
            band = _window_matrix(seq, w)
        bands.append(band)
        invs.append(np.repeat((1.0 / band.sum(axis=1))[:, None], POOL_GW, axis=1))
    band = jnp.asarray(np.stack(bands), dtype=BF16)
    inv = jnp.asarray(np.concatenate(invs, axis=1), dtype=F32)
    return (jnp.asarray(cs, dtype=F32), jnp.asarray(c64, dtype=F32), jnp.asarray(s64, dtype=F32),
            band, inv)


def _block_diag(w):
    g, n, _ = w.shape
    out = jnp.zeros((g * n, g * n), w.dtype)
    for i in range(g):
        out = out.at[i * n:(i + 1) * n, i * n:(i + 1) * n].set(w[i])
    return out


def _post_kernel(final, x_ref, ya_ref, yb_ref, yc_ref, mod_ref, wo_ref, g2_ref, w1_ref, w2_ref,
                 fg_ref, o_ref):
    mod = mod_ref[0]
    gate1 = mod[:, 2 * D_MODEL:3 * D_MODEL]
    shift2 = mod[:, 3 * D_MODEL:4 * D_MODEL]
    scale2 = mod[:, 4 * D_MODEL:5 * D_MODEL]
    gate2 = mod[:, 5 * D_MODEL:]
    mix = (jnp.dot(ya_ref[...].astype(BF16), wo_ref[0:D_RWKV, :], preferred_element_type=F32)
           + jnp.dot(yb_ref[...].astype(BF16), wo_ref[D_RWKV:D_RWKV + D_FNET, :],
                     preferred_element_type=F32)
           + jnp.dot(yc_ref[...].astype(BF16), wo_ref[D_RWKV + D_FNET:, :],
                     preferred_element_type=F32))
    x1 = x_ref[...] + gate1 * mix
    h2 = (_rms(x1, g2_ref[...]) * (1.0 + scale2) + shift2).astype(BF16)
    tf = 1024
    acc = jnp.zeros_like(x1)
    for j in range(D_FF // tf):
        ff = jnp.dot(h2, w1_ref[:, j * tf:(j + 1) * tf], preferred_element_type=F32)
        ff = jnp.square(jnp.maximum(ff, 0.0))
        acc = acc + jnp.dot(ff.astype(BF16), w2_ref[j * tf:(j + 1) * tf, :],
                            preferred_element_type=F32)
    x2 = x1 + gate2 * acc
    if final:
        x2 = _rms(x2, fg_ref[...])
    o_ref[...] = x2


def _post(x, ya, yb, yc, mod, mod_row, w_out_b, g2, w1_b, w2_b, fg, final, tm):
    t = x.shape[0]
    tok = lambda i: (i, 0)
    full = lambda i: (0, 0)
    return pl.pallas_call(
        functools.partial(_post_kernel, final),
        out_shape=jax.ShapeDtypeStruct((t, D_MODEL), F32),
        grid=(t // tm,),
        in_specs=[
            pl.BlockSpec((tm, D_MODEL), tok),
            pl.BlockSpec((tm, D_RWKV), tok),
            pl.BlockSpec((tm, D_FNET), tok),
            pl.BlockSpec((tm, D_POOL), tok),
            pl.BlockSpec((1, 1, 6 * D_MODEL), lambda i: (mod_row(i), 0, 0)),
            pl.BlockSpec((D_MODEL, D_MODEL), full),
            pl.BlockSpec((1, D_MODEL), full),
            pl.BlockSpec((D_MODEL, D_FF), full),
            pl.BlockSpec((D_FF, D_MODEL), full),
            pl.BlockSpec((1, D_MODEL), full),
        ],
        out_specs=pl.BlockSpec((tm, D_MODEL), tok),
        compiler_params=pltpu.CompilerParams(
            dimension_semantics=("arbitrary",), vmem_limit_bytes=VMEM_LIMIT),
        name="post",
    )(x, ya, yb, yc, mod, w_out_b, g2.reshape(1, D_MODEL), w1_b, w2_b, fg.reshape(1, D_MODEL))


def _layer(x, bsz, seq, mod, mod_row, s0, want_state, pairs, consts, lw, final, fg, tm):
    (norm1_g, w_in_b, mu_shift, w0, w_up, a0, a_up, g_up, k_k, k_a, r_k, ln_w, ln_b, wf_bd, wp_bd,
     pool_scale, w_out_b, norm2_g, w1_b, w2_b) = lw
    pa, pf, pp = _pre(x, mod, mod_row, norm1_g, w_in_b, tm)
    ya, s_fin = _rwkv(pa.reshape(bsz, seq, SHIFT_END), s0, want_state, pairs, mu_shift, w0, w_up, a0,
                      a_up, g_up, k_k, k_a, r_k, ln_w, ln_b)
    yb, yc = _fpool(pf.reshape(bsz, seq, D_FNET), pp.reshape(bsz, seq, D_POOL), consts, wf_bd,
                    wp_bd, pool_scale)
    x = _post(x, ya.reshape(bsz * seq, D_RWKV), yb.reshape(bsz * seq, D_FNET),
              yc.reshape(bsz * seq, D_POOL), mod, mod_row, w_out_b, norm2_g, w1_b, w2_b, fg, final, tm)
    return x, s_fin


def kernel(x_prompt, x_sample, state_wkv, c, c_ctx, w_ada, b_ada, norm1_g, w_in, mu_shift, w0, w_up, a0, a_up, g_up, k_k, k_a, r_k, ln_x_w, ln_x_b, w_fnet, w_pool, pool_scale, w_out, norm2_g, w_ff1, w_ff2, final_norm_g):
    bp, lp, _ = x_prompt.shape
    bs, ls, _ = x_sample.shape
    tm = 512
    n_rows = 8
    cond = jnp.zeros((n_rows, D_MODEL), F32).at[0].set(c_ctx).at[1:1 + bs].set(c)
    mod_all = _ada(cond, w_ada, b_ada)
    consts_p = _mixer_constants(lp, False)
    consts_s = _mixer_constants(ls, True)
    xp = x_prompt.reshape(bp * lp, D_MODEL)
    xs = x_sample.reshape(bs * ls, D_MODEL)
    tiles_per_seq = ls // tm
    row_p = lambda i: 0
    row_s = lambda i: 1 + i // tiles_per_seq
    new_states = []
    for l in range(DEPTH):
        lw = (norm1_g[l], w_in[l].astype(BF16), mu_shift[l], w0[l], w_up[l], a0[l], a_up[l], g_up[l],
              k_k[l], k_a[l], r_k[l], ln_x_w[l], ln_x_b[l], _block_diag(w_fnet[l]),
              _block_diag(w_pool[l]), pool_scale[l], w_out[l].astype(BF16), norm2_g[l],
              w_ff1[l].astype(BF16), w_ff2[l].astype(BF16))
        mod = mod_all[l].reshape(n_rows, 1, 6 * D_MODEL)
        final = l == DEPTH - 1
        xp, s_ctx = _layer(xp, bp, lp, mod, row_p, None, True, 4, consts_p, lw, final, final_norm_g, tm)
        new_states.append(s_ctx)
        xs, _ = _layer(xs, bs, ls, mod, row_s, state_wkv[:, l], False, 2, consts_s, lw, final,
                       final_norm_g, tm)
    new_state_wkv = jnp.stack(new_states, axis=1)
    return (xp.reshape(bp, lp, D_MODEL), xs.reshape(bs, ls, D_MODEL), new_state_wkv)
```

```python
import functools
import math

import numpy as np
import jax
import jax.numpy as jnp
from jax import lax
from jax.experimental import pallas as pl
from jax.experimental.pallas import tpu as pltpu

F32 = jnp.float32
BF16 = jnp.bfloat16

D_MODEL = 1024
DEPTH = 2
GRID_W = 64
D_RWKV = 512
HEAD = 64
N_HEADS = 8
LANES = 128
N_PAIRS = D_RWKV // LANES
D_FNET = 256
FNET_GW = 64
D_POOL = 256
POOL_GW = 64
POOL_WINDOWS = (2, 4, 8, 16)
LORA = 64
GATE_LORA = 128
SHIFT_END = 3 * D_RWKV + 2 * LORA + 2 * LORA + GATE_LORA
LORA_W = SHIFT_END - 3 * D_RWKV
D_IN = SHIFT_END + D_FNET + D_POOL
D_FF = 4 * D_MODEL
RMS_EPS = 1e-6
GN_EPS = 64e-5
LOG2E = 1.4426950408889634
CHUNK = 64
VMEM_LIMIT = 56 * 1024 * 1024
STREAM_UNITS = 16
STREAM_LAG = 8
ITER_UNITS = 32


def _dotf(a, b):
    return jnp.dot(a, b, preferred_element_type=F32, precision=lax.Precision.HIGHEST)


def _split(x, n):
    terms = []
    for _ in range(n - 1):
        hi = x.astype(BF16)
        terms.append(hi)
        x = x - hi.astype(F32)
    terms.append(x.astype(BF16))
    return terms


def _dots(a, b, na, nb, nt=False):
    dims = (((1,), (1,)), ((), ())) if nt else (((1,), (0,)), ((), ()))
    ta = _split(a, na)
    tb = _split(b, nb)
    out = None
    for i in range(na):
        for j in range(nb):
            if i + j < max(na, nb):
                t = lax.dot_general(ta[i], tb[j], dims, preferred_element_type=F32)
                out = t if out is None else out + t
    return out


def _mm(a, b):
    return jnp.dot(a.astype(BF16), b.astype(BF16), preferred_element_type=F32)


def _sigmoid(x):
    return 0.5 + 0.5 * jnp.tanh(0.5 * x)


def _rms(x, g):
    ms = jnp.mean(x * x, axis=-1, keepdims=True)
    return x * lax.rsqrt(ms + RMS_EPS) * g


_DONE = object()


def _interleave(streams, lag):
    pending = list(streams)
    active = []
    step = 0
    while pending or active:
        if pending and step % lag == 0:
            active.append(pending.pop(0))
        for s in list(active):
            if next(s, _DONE) is _DONE:
                active.remove(s)
        step += 1


def _ada_kernel(c_ref, w_ref, b_ref, o_ref):
    c = c_ref[...]
    s = c * _sigmoid(c)
    o_ref[0] = _dotf(s, w_ref[0]) + b_ref[0]


def _ada(cond, w_ada, b_ada):
    tn = 1536
    nrow = cond.shape[0]
    return pl.pallas_call(
        _ada_kernel,
        out_shape=jax.ShapeDtypeStruct((DEPTH, nrow, 6 * D_MODEL), F32),
        grid=(DEPTH, 6 * D_MODEL // tn),
        in_specs=[
            pl.BlockSpec((nrow, D_MODEL), lambda l, j: (0, 0)),
            pl.BlockSpec((1, D_MODEL, tn), lambda l, j: (l, 0, j)),
            pl.BlockSpec((1, 1, tn), lambda l, j: (l, 0, j)),
        ],
        out_specs=pl.BlockSpec((1, nrow, tn), lambda l, j: (l, 0, j)),
        compiler_params=pltpu.CompilerParams(
            dimension_semantics=("arbitrary", "arbitrary"), vmem_limit_bytes=VMEM_LIMIT),
        name="ada",
    )(cond, w_ada, b_ada.reshape(DEPTH, 1, 6 * D_MODEL))


def _pre_kernel(x_ref, mod_ref, g_ref, w_ref, oa_ref, of_ref, op_ref):
    mod = mod_ref[0]
    shift = mod[:, 0:D_MODEL]
    scale = mod[:, D_MODEL:2 * D_MODEL]
    h = _rms(x_ref[...], g_ref[...]) * (1.0 + scale) + shift
    p = jnp.dot(h.astype(BF16), w_ref[...], preferred_element_type=F32)
    oa_ref[...] = p[:, :SHIFT_END]
    of_ref[...] = p[:, SHIFT_END:SHIFT_END + D_FNET]
    op_ref[...] = p[:, SHIFT_END + D_FNET:]


def _pre(x, mod, mod_row, g, w_in_b, tm):
    t = x.shape[0]
    return pl.pallas_call(
        _pre_kernel,
        out_shape=(jax.ShapeDtypeStruct((t, SHIFT_END), F32),
                   jax.ShapeDtypeStruct((t, D_FNET), F32),
                   jax.ShapeDtypeStruct((t, D_POOL), F32)),
        grid=(t // tm,),
        in_specs=[
            pl.BlockSpec((tm, D_MODEL), lambda i: (i, 0)),
            pl.BlockSpec((1, 1, 6 * D_MODEL), lambda i: (mod_row(i), 0, 0)),
            pl.BlockSpec((1, D_MODEL), lambda i: (0, 0)),
            pl.BlockSpec((D_MODEL, D_IN), lambda i: (0, 0)),
        ],
        out_specs=(pl.BlockSpec((tm, SHIFT_END), lambda i: (i, 0)),
                   pl.BlockSpec((tm, D_FNET), lambda i: (i, 0)),
                   pl.BlockSpec((tm, D_POOL), lambda i: (i, 0))),
        compiler_params=pltpu.CompilerParams(
            dimension_semantics=("arbitrary",), vmem_limit_bytes=VMEM_LIMIT),
        name="pre",
    )(x, mod, g.reshape(1, D_MODEL), w_in_b)


def _shift_mix(p, mu):
    n = p.shape[0]
    row = lax.broadcasted_iota(jnp.int32, p.shape, 0)
    prev = jnp.where(row == 0, 0.0, pltpu.roll(p, 1, 0))
    nxt = jnp.where(row == n - 1, 0.0, pltpu.roll(p, n - 1, 0))
    return p * (1.0 - mu) + (0.5 * mu) * (prev + nxt)


def _rwkv_kernel(seq, pp, has_state, want_state, *refs):
    (r_ref, k_ref, v_ref, lo_ref, mur_ref, muk_ref, muv_ref, mul_ref, w0_ref, wup_ref, a0_ref,
     aup_ref, gup_ref, kk_ref, ka_ref, rk_ref, lnw_ref, lnb_ref) = refs[:18]
    n_in = 18 + has_state
    s0_ref = refs[18] if has_state else None
    y_ref = refs[n_in]
    sfin_ref = refs[n_in + 1] if want_state else None
    lw_s, kd_s, b_s, r_s, v_s, kn_s, st_s, rb_s, yb_s, gt_s, ht_s = refs[n_in + 1 + want_state:]
    nc = seq // CHUNK
    C = CHUNK
    tiles = [slice(i * LANES, (i + 1) * LANES) for i in range(pp)]

    lane1 = lax.broadcasted_iota(jnp.int32, (1, LANES), 1)
    head0 = lane1 < HEAD
    head1 = jnp.logical_not(head0)
    ri = lax.broadcasted_iota(jnp.int32, (LANES, LANES), 0)
    ci = lax.broadcasted_iota(jnp.int32, (LANES, LANES), 1)
    same_head = (ri // HEAD) == (ci // HEAD)
    seg_ones = same_head.astype(F32)
    eye = (ri == ci).astype(F32)

    def seg_sum(x):
        return jnp.concatenate([_mm(x[:, t], seg_ones) for t in tiles], axis=1)

    r = _shift_mix(r_ref[0], mur_ref[...])
    k = _shift_mix(k_ref[0], muk_ref[...])
    v = _shift_mix(v_ref[0], muv_ref[...])
    lo = _shift_mix(lo_ref[0], mul_ref[...])
    wd = jnp.tanh(lo[:, 0:2 * LORA])
    ad = lo[:, 2 * LORA:4 * LORA]
    gd = _sigmoid(lo[:, 4 * LORA:])
    g = _mm(gd, gup_ref[...])
    kx = k * kk_ref[...]
    kn = kx * lax.rsqrt(jnp.maximum(seg_sum(kx * kx), 1e-24))
    zeros_up = jnp.zeros((LORA, pp * LANES), F32)
    kd_sum = jnp.zeros_like(k)
    for d in range(2):
        if d == 0:
            wup = jnp.concatenate([wup_ref[0], zeros_up], axis=0)
            aup = jnp.concatenate([aup_ref[0], zeros_up], axis=0)
        else:
            wup = jnp.concatenate([zeros_up, wup_ref[1]], axis=0)
            aup = jnp.concatenate([zeros_up, aup_ref[1]], axis=0)
        zw = w0_ref[d:d + 1, :] + _mm(wd, wup)
        lw2 = (-LOG2E * math.exp(-0.5)) * _sigmoid(zw)
        lw_s[d] = lw2
        a_sig = _sigmoid(a0_ref[d:d + 1, :] + _mm(ad, aup))
        kd = k * (1.0 + (a_sig - 1.0) * ka_ref[...])
        kd_s[d] = kd
        b_s[d] = kn * a_sig
        kn_s[d] = -kn * jnp.exp2(-lw2)
        kd_sum = kd_sum + kd
    r_s[...] = r
    v_s[...] = v
    bonus = seg_sum(r * kd_sum * rk_ref[...]) * v

    if has_state:
        rj = lax.broadcasted_iota(jnp.int32, (LANES, HEAD), 0)
        cj = lax.broadcasted_iota(jnp.int32, (LANES, HEAD), 1)
        place = [(rj == cj + j * HEAD).astype(F32) for j in range(2)]
        keys = [(d, pi, j) for d in range(2) for pi in range(pp) for j in range(2)]
        t1 = {key: _dots(place[key[2]], s0_ref[0, key[0], 2 * key[1] + key[2]], 1, 3, nt=True)
              for key in keys}
        t2 = {key: _dots(t1[key], place[key[2]], 3, 1, nt=True) for key in keys}
        for d in range(2):
            for pi in range(pp):
                st_s[d, pi] = t2[(d, pi, 0)] + t2[(d, pi, 1)]
    else:
        st_s[...] = jnp.zeros(st_s.shape, F32)

    tr = lax.broadcasted_iota(jnp.int32, (C, C), 0)
    tc = lax.broadcasted_iota(jnp.int32, (C, C), 1)
    t_p = lax.broadcasted_iota(jnp.int32, (C, LANES), 0)
    s_p = lax.broadcasted_iota(jnp.int32, (C, LANES), 1) % C
    tris = ((tc <= tr).astype(F32), (tc >= tr).astype(F32))
    strict = (s_p < t_p, s_p > t_p)
    inclusive = (s_p <= t_p, s_p >= t_p)
    eye_p = (s_p == t_p).astype(F32)
    txs = t_p ^ s_p
    lvl_map = sum(((txs >= (1 << bit)).astype(jnp.int32) for bit in range(1, 6)),
                  jnp.where(txs == 0, -1, 0))
    zero_b = jnp.zeros((C, LANES), BF16)

    def bd(x):
        return jnp.concatenate([jnp.where(head0, x, zero_b), jnp.where(head1, x, zero_b)], axis=0)

    def bdx(x):
        return jnp.concatenate([jnp.where(head1, x, zero_b), jnp.where(head0, x, zero_b)], axis=0)

    def mm(a, b):
        return jnp.dot(a, b, preferred_element_type=F32)

    def rows(c):
        return pl.ds(c * C, C) if isinstance(c, int) else pl.ds(pl.multiple_of(c * C, C), C)

    def algebra(units):
        u = []
        for c, pi, d in units:
            sl, ln = rows(c), tiles[pi]
            lw = lw_s[d, sl, ln]
            u.append(dict(c=c, pi=pi, d=d, sl=sl, ln=ln, lw=lw, cl=_dots(tris[d], lw, 1, 2)))
        yield
        for q in u:
            d, sl, ln, cl = q["d"], q["sl"], q["ln"], q["cl"]
            tot = cl[C - 1:C, :] if d == 0 else cl[0:1, :]
            e_in = jnp.exp2(cl)
            e_inv = jnp.exp2(-cl)
            p_c = jnp.exp2(tot)
            q["at"] = (kn_s[d, sl, ln] * e_in).astype(BF16)
            q["rt"] = r_s[sl, ln] * e_in
            bt = b_s[d, sl, ln] * e_inv
            kt = kd_s[d, sl, ln] * e_inv
            q["lhs_t"] = jnp.concatenate([bt * p_c, kt * p_c], axis=0).T.astype(BF16)
            bt = bt.astype(BF16)
            kt = kt.astype(BF16)
            q["rhs_nt"] = jnp.concatenate([jnp.where(head0, bt, zero_b), jnp.where(head1, bt, zero_b),
                                           jnp.where(head1, kt, zero_b), jnp.where(head0, kt, zero_b)], axis=0)
            q["p_c"] = p_c
            q["vc"] = v_s[sl, ln].astype(BF16)
            yield
        for q in u:
            lhs = jnp.concatenate([q["at"], q["rt"].astype(BF16)], axis=0)
            q["sc"] = lax.dot_general(lhs, q["rhs_nt"], (((1,), (1,)), ((), ())), preferred_element_type=F32)
        yield
        for q in u:
            d, sc = q["d"], q["sc"]
            q["lab"] = jnp.where(strict[d], sc[:C, :LANES], 0.0)
            lakx = jnp.where(strict[d], sc[:C, LANES:], 0.0)
            q["mr"] = jnp.concatenate([jnp.where(inclusive[d], sc[C:, :LANES], 0.0),
                                       jnp.where(inclusive[d], sc[C:, LANES:], 0.0)], axis=1).astype(BF16)
            q["xv"] = mm(lakx.astype(BF16), bdx(q["vc"]))
            q["t"] = eye_p + jnp.where(lvl_map == 0, q["lab"], 0.0)
        yield
        for lvl in range(1, 6):
            for q in u:
                q["tb"] = q["t"].astype(BF16)
                q["w"] = mm(jnp.where(lvl_map == lvl, q["lab"], 0.0).astype(BF16), bd(q["tb"]))
            yield
            for q in u:
                q["t"] = q["t"] + mm(q["tb"], bd(q["w"].astype(BF16)))
            yield
        for q in u:
            x = mm(q["t"].astype(BF16),
                   jnp.concatenate([bd(q["at"]), bd(q["xv"].astype(BF16))], axis=1))
            q["ab"] = x[:, :LANES].astype(BF16)
            q["ub"] = x[:, LANES:].astype(BF16)
        yield
        for q in u:
            rhs = jnp.concatenate([jnp.concatenate([bd(q["ab"]), bd(q["ub"])], axis=1),
                                   jnp.concatenate([jnp.concatenate([zero_b, zero_b], axis=0), bdx(q["vc"])],
                                                   axis=1)], axis=0)
            q["o2"] = mm(q["mr"], rhs)
            rhs2 = jnp.concatenate([jnp.concatenate([q["ab"], q["ub"]], axis=1),
                                    jnp.concatenate([zero_b, q["vc"]], axis=1)], axis=0)
            q["gh"] = mm(q["lhs_t"], rhs2)
        yield
        for q in u:
            c, pi, d, sl, ln, o2, gh = q["c"], q["pi"], q["d"], q["sl"], q["ln"], q["o2"], q["gh"]
            rb_s[d, sl, ln] = (q["rt"] + o2[:, :LANES]).astype(BF16)
            yb_s[d, sl, ln] = o2[:, LANES:]
            gt_s[d, pi, c] = (jnp.where(same_head, gh[:, :LANES], 0.0) + eye * q["p_c"]).astype(BF16)
            ht_s[d, pi, c] = jnp.where(same_head, gh[:, LANES:], 0.0)
        yield

    group = min(nc, max(1, ITER_UNITS // (2 * pp)))

    def algebra_step(i, carry):
        units = [(i * group + gi, pi, d) for gi in range(group) for pi in range(pp) for d in range(2)]
        streams = [algebra(units[s:s + STREAM_UNITS]) for s in range(0, len(units), STREAM_UNITS)]
        _interleave(streams, STREAM_LAG)
        return carry

    if nc == group:
        algebra_step(0, 0)
    else:
        lax.fori_loop(0, nc // group, algebra_step, 0)

    y_ref[0] = jnp.zeros(y_ref.shape[1:], F32)

    def state_step(i, carry):
        items = []
        for pi in range(pp):
            for d in range(2):
                c = i if d == 0 else nc - 1 - i
                sl, ln = rows(c), tiles[pi]
                lhs = jnp.concatenate([rb_s[d, sl, ln], gt_s[d, pi, c]], axis=0)
                items.append((pi, d, c, sl, ln, mm(lhs, st_s[d, pi].astype(BF16))))
        for pi, d, c, sl, ln, o in items:
            y_ref[0, sl, ln] += o[:C] + yb_s[d, sl, ln]
            st_s[d, pi] = o[C:] + ht_s[d, pi, c]
        return carry

    lax.fori_loop(0, nc, state_step, 0)

    if want_state:
        for d in range(2):
            for pi in range(pp):
                stt = st_s[d, pi].T
                sfin_ref[0, d, 2 * pi] = stt[:HEAD, :HEAD]
                sfin_ref[0, d, 2 * pi + 1] = pltpu.roll(stt[HEAD:], HEAD, 1)[:, :HEAD]

    y = y_ref[0]
    yc = y - seg_sum(y) * (1.0 / HEAD)
    var = seg_sum(yc * yc) * (1.0 / HEAD)
    yn = yc * lax.rsqrt(var + GN_EPS) * lnw_ref[...] + lnb_ref[...]
    y_ref[0] = (yn + bonus) * g


def _rwkv(pa, s0, want_state, pp, mu, w0, w_up, a0, a_up, g_up, k_k, k_a, r_k, ln_w, ln_b):
    bsz, seq, _ = pa.shape
    has_state = s0 is not None
    width = pp * LANES
    nblk = D_RWKV // width
    nc = seq // CHUNK
    mu2 = mu.reshape(1, SHIFT_END)
    row = lambda a: a.reshape(1, D_RWKV)
    col = lambda o: (lambda b, p: (b, 0, o * nblk + p))
    vec = lambda b, p: (0, p)
    lora_blk = 3 * D_RWKV // LORA_W
    in_specs = [
        pl.BlockSpec((1, seq, width), col(0)),
        pl.BlockSpec((1, seq, width), col(1)),
        pl.BlockSpec((1, seq, width), col(2)),
        pl.BlockSpec((1, seq, LORA_W), lambda b, p: (b, 0, lora_blk)),
        pl.BlockSpec((1, width), lambda b, p: (0, p)),
        pl.BlockSpec((1, width), lambda b, p: (0, nblk + p)),
        pl.BlockSpec((1, width), lambda b, p: (0, 2 * nblk + p)),
        pl.BlockSpec((1, LORA_W), lambda b, p: (0, lora_blk)),
        pl.BlockSpec((2, width), vec),
        pl.BlockSpec((2, LORA, width), lambda b, p: (0, 0, p)),
        pl.BlockSpec((2, width), vec),
        pl.BlockSpec((2, LORA, width), lambda b, p: (0, 0, p)),
        pl.BlockSpec((GATE_LORA, width), vec),
        pl.BlockSpec((1, width), vec),
        pl.BlockSpec((1, width), vec),
        pl.BlockSpec((1, width), vec),
        pl.BlockSpec((1, width), vec),
        pl.BlockSpec((1, width), vec),
    ]
    args = [pa, pa, pa, pa, mu2, mu2, mu2, mu2, w0, w_up, a0, a_up, g_up, row(k_k), row(k_a),
            row(r_k), row(ln_w), row(ln_b)]
    state_spec = pl.BlockSpec((1, 2, 2 * pp, HEAD, HEAD), lambda b, p: (b, 0, p, 0, 0))
    if has_state:
        in_specs.append(state_spec)
        args.append(s0)
    out_shape = [jax.ShapeDtypeStruct((bsz, seq, D_RWKV), F32)]
    out_specs = [pl.BlockSpec((1, seq, width), lambda b, p: (b, 0, p))]
    if want_state:
        out_shape.append(jax.ShapeDtypeStruct((bsz, 2, N_HEADS, HEAD, HEAD), F32))
        out_specs.append(state_spec)
    outs = pl.pallas_call(
        functools.partial(_rwkv_kernel, seq, pp, has_state, want_state),
        out_shape=tuple(out_shape),
        grid=(bsz, nblk),
        in_specs=in_specs,
        out_specs=tuple(out_specs),
        scratch_shapes=[
            pltpu.VMEM((2, seq, width), F32),
            pltpu.VMEM((2, seq, width), F32),
            pltpu.VMEM((2, seq, width), F32),
            pltpu.VMEM((seq, width), F32),
            pltpu.VMEM((seq, width), F32),
            pltpu.VMEM((2, seq, width), F32),
            pltpu.VMEM((2, pp, LANES, LANES), F32),
            pltpu.VMEM((2, seq, width), BF16),
            pltpu.VMEM((2, seq, width), F32),
            pltpu.VMEM((2, pp, nc, LANES, LANES), BF16),
            pltpu.VMEM((2, pp, nc, LANES, LANES), F32),
        ],
        compiler_params=pltpu.CompilerParams(
            dimension_semantics=("arbitrary", "arbitrary"), vmem_limit_bytes=VMEM_LIMIT),
        name="rwkv",
    )(*args)
    return (outs[0], outs[1]) if want_state else (outs[0], None)


def _fpool_kernel(seq, f_ref, p_ref, cs_ref, c64_ref, s64_ref, wf_ref, band_ref, inv_ref, wp_ref,
                  ps_ref, yb_ref, yc_ref, cw_s, cs_s):
    @pl.when(pl.program_id(0) == 0)
    def _():
        wf = wf_ref[...]
        cw_s[:, :D_FNET] = _dotf(c64_ref[...], wf).astype(BF16)
        cw_s[:, D_FNET:] = (-_dotf(s64_ref[...], wf)).astype(BF16)
        cs_s[...] = cs_ref[...].astype(BF16)

    g = jnp.dot(f_ref[0].astype(BF16), cw_s[...], preferred_element_type=F32).astype(BF16)
    yb_ref[0] = jnp.dot(cs_s[...], jnp.concatenate([g[:, :D_FNET], g[:, D_FNET:]], axis=0),
                        preferred_element_type=F32)

    p = p_ref[0]
    p_b = p.astype(BF16)
    lane = lax.broadcasted_iota(jnp.int32, (seq, D_POOL), 1)
    d = jnp.zeros((seq, D_POOL), F32)
    for i in range(len(POOL_WINDOWS)):
        s = jnp.dot(band_ref[i], p_b, preferred_element_type=F32)
        d = jnp.where((lane // POOL_GW) == i, s, d)
    d = d * inv_ref[...] - p
    yc_ref[0] = _mm(d, wp_ref[...]) * ps_ref[...]


def _fpool(pf, pp, consts, wf_bd, wp_bd, pool_scale):
    bsz, seq, _ = pf.shape
    cs, c64, s64, band, inv = consts
    full2 = lambda b: (0, 0)
    return pl.pallas_call(
        functools.partial(_fpool_kernel, seq),
        out_shape=(jax.ShapeDtypeStruct((bsz, seq, D_FNET), F32),
                   jax.ShapeDtypeStruct((bsz, seq, D_POOL), F32)),
        grid=(bsz,),
        in_specs=[
            pl.BlockSpec((1, seq, D_FNET), lambda b: (b, 0, 0)),
            pl.BlockSpec((1, seq, D_POOL), lambda b: (b, 0, 0)),
            pl.BlockSpec((seq, 2 * seq), full2),
            pl.BlockSpec((D_FNET, D_FNET), full2),
            pl.BlockSpec((D_FNET, D_FNET), full2),
            pl.BlockSpec((D_FNET, D_FNET), full2),
            pl.BlockSpec((len(POOL_WINDOWS), seq, seq), lambda b: (0, 0, 0)),
            pl.BlockSpec((seq, D_POOL), full2),
            pl.BlockSpec((D_POOL, D_POOL), full2),
            pl.BlockSpec((1, D_POOL), full2),
        ],
        out_specs=(pl.BlockSpec((1, seq, D_FNET), lambda b: (b, 0, 0)),
                   pl.BlockSpec((1, seq, D_POOL), lambda b: (b, 0, 0))),
        scratch_shapes=[pltpu.VMEM((D_FNET, 2 * D_FNET), BF16),
                        pltpu.VMEM((seq, 2 * seq), BF16)],
        compiler_params=pltpu.CompilerParams(
            dimension_semantics=("arbitrary",), vmem_limit_bytes=VMEM_LIMIT),
        name="fpool",
    )(pf, pp, cs, c64, s64, wf_bd, band, inv, wp_bd, pool_scale.reshape(1, D_POOL))


def _window_matrix(n, w):
    t = np.arange(n)
    lo = np.clip(t - w // 2, 0, n)
    hi = np.clip(t + w - w // 2, 0, n)
    s = np.arange(n)
    return ((s[None, :] >= lo[:, None]) & (s[None, :] < hi[:, None])).astype(np.float64)


def _mixer_constants(seq, grid):
    n = np.arange(seq)
    ang = 2.0 * np.pi * ((n[:, None] * n[None, :]) % seq) / seq
    scale = 1.0 / math.sqrt(seq * FNET_GW)
    cs = np.concatenate([np.cos(ang), np.sin(ang)], axis=1) * scale
    m = np.arange(FNET_GW)
    ang64 = 2.0 * np.pi * ((m[:, None] * m[None, :]) % FNET_GW) / FNET_GW
    groups = D_FNET // FNET_GW
    c64 = np.kron(np.eye(groups), np.cos(ang64))
    s64 = np.kron(np.eye(groups), np.sin(ang64))
    bands = []
    invs = []
    for w in POOL_WINDOWS:
        if grid:
            rows = seq // GRID_W
            band = np.kron(_window_matrix(rows, w), _window_matrix(GRID_W, w))
        else:
            band = _window_matrix(seq, w)
        bands.append(band)
        invs.append(np.repeat((1.0 / band.sum(axis=1))[:, None], POOL_GW, axis=1))
    band = jnp.asarray(np.stack(bands), dtype=BF16)
    inv = jnp.asarray(np.concatenate(invs, axis=1), dtype=F32)
    return (jnp.asarray(cs, dtype=F32), jnp.asarray(c64, dtype=F32), jnp.asarray(s64, dtype=F32),
            band, inv)


def _block_diag(w):
    g, n, _ = w.shape
    return (w[:, :, None, :] * jnp.eye(g, dtype=w.dtype)[:, None, :, None]).reshape(g * n, g * n)


def _post_kernel(final, x_ref, ya_ref, yb_ref, yc_ref, mod_ref, wo_ref, g2_ref, w1_ref, w2_ref,
                 fg_ref, o_ref):
    mod = mod_ref[0]
    gate1 = mod[:, 2 * D_MODEL:3 * D_MODEL]
    shift2 = mod[:, 3 * D_MODEL:4 * D_MODEL]
    scale2 = mod[:, 4 * D_MODEL:5 * D_MODEL]
    gate2 = mod[:, 5 * D_MODEL:]
    mix = (jnp.dot(ya_ref[...].astype(BF16), wo_ref[0:D_RWKV, :], preferred_element_type=F32)
           + jnp.dot(yb_ref[...].astype(BF16), wo_ref[D_RWKV:D_RWKV + D_FNET, :],
                     preferred_element_type=F32)
           + jnp.dot(yc_ref[...].astype(BF16), wo_ref[D_RWKV + D_FNET:, :],
                     preferred_element_type=F32))
    x1 = x_ref[...] + gate1 * mix
    h2 = (_rms(x1, g2_ref[...]) * (1.0 + scale2) + shift2).astype(BF16)
    tf = 1024
    acc = jnp.zeros_like(x1)
    for j in range(D_FF // tf):
        ff = jnp.dot(h2, w1_ref[:, j * tf:(j + 1) * tf], preferred_element_type=F32)
        ff = jnp.square(jnp.maximum(ff, 0.0))
        acc = acc + jnp.dot(ff.astype(BF16), w2_ref[j * tf:(j + 1) * tf, :],
                            preferred_element_type=F32)
    x2 = x1 + gate2 * acc
    if final:
        x2 = _rms(x2, fg_ref[...])
    o_ref[...] = x2


def _post(x, ya, yb, yc, mod, mod_row, w_out_b, g2, w1_b, w2_b, fg, final, tm):
    t = x.shape[0]
    tok = lambda i: (i, 0)
    full = lambda i: (0, 0)
    return pl.pallas_call(
        functools.partial(_post_kernel, final),
        out_shape=jax.ShapeDtypeStruct((t, D_MODEL), F32),
        grid=(t // tm,),
        in_specs=[
            pl.BlockSpec((tm, D_MODEL), tok),
            pl.BlockSpec((tm, D_RWKV), tok),
            pl.BlockSpec((tm, D_FNET), tok),
            pl.BlockSpec((tm, D_POOL), tok),
            pl.BlockSpec((1, 1, 6 * D_MODEL), lambda i: (mod_row(i), 0, 0)),
            pl.BlockSpec((D_MODEL, D_MODEL), full),
            pl.BlockSpec((1, D_MODEL), full),
            pl.BlockSpec((D_MODEL, D_FF), full),
            pl.BlockSpec((D_FF, D_MODEL), full),
            pl.BlockSpec((1, D_MODEL), full),
        ],
        out_specs=pl.BlockSpec((tm, D_MODEL), tok),
        compiler_params=pltpu.CompilerParams(
            dimension_semantics=("arbitrary",), vmem_limit_bytes=VMEM_LIMIT),
        name="post",
    )(x, ya, yb, yc, mod, w_out_b, g2.reshape(1, D_MODEL), w1_b, w2_b, fg.reshape(1, D_MODEL))


def _layer(x, bsz, seq, mod, mod_row, s0, want_state, pairs, consts, lw, final, fg, tm):
    (norm1_g, w_in_b, mu_shift, w0, w_up, a0, a_up, g_up, k_k, k_a, r_k, ln_w, ln_b, wf_bd, wp_bd,
     pool_scale, w_out_b, norm2_g, w1_b, w2_b) = lw
    pa, pf, pp = _pre(x, mod, mod_row, norm1_g, w_in_b, tm)
    ya, s_fin = _rwkv(pa.reshape(bsz, seq, SHIFT_END), s0, want_state, pairs, mu_shift, w0, w_up, a0,
                      a_up, g_up, k_k, k_a, r_k, ln_w, ln_b)
    yb, yc = _fpool(pf.reshape(bsz, seq, D_FNET), pp.reshape(bsz, seq, D_POOL), consts, wf_bd,
                    wp_bd, pool_scale)
    x = _post(x, ya.reshape(bsz * seq, D_RWKV), yb.reshape(bsz * seq, D_FNET),
              yc.reshape(bsz * seq, D_POOL), mod, mod_row, w_out_b, norm2_g, w1_b, w2_b, fg, final, tm)
    return x, s_fin


def kernel(x_prompt, x_sample, state_wkv, c, c_ctx, w_ada, b_ada, norm1_g, w_in, mu_shift, w0, w_up, a0, a_up, g_up, k_k, k_a, r_k, ln_x_w, ln_x_b, w_fnet, w_pool, pool_scale, w_out, norm2_g, w_ff1, w_ff2, final_norm_g):
    bp, lp, _ = x_prompt.shape
    bs, ls, _ = x_sample.shape
    tm = 512
    n_rows = 8
    cond = jnp.concatenate([c_ctx[None], c, jnp.zeros((n_rows - 1 - bs, D_MODEL), F32)], axis=0)
    mod_all = _ada(cond, w_ada, b_ada)
    consts_p = _mixer_constants(lp, False)
    consts_s = _mixer_constants(ls, True)
    xp = x_prompt.reshape(bp * lp, D_MODEL)
    xs = x_sample.reshape(bs * ls, D_MODEL)
    tiles_per_seq = ls // tm
    row_p = lambda i: 0
    row_s = lambda i: 1 + i // tiles_per_seq
    new_states = []
    for l in range(DEPTH):
        lw = (norm1_g[l], w_in[l].astype(BF16), mu_shift[l], w0[l], w_up[l], a0[l], a_up[l], g_up[l],
              k_k[l], k_a[l], r_k[l], ln_x_w[l], ln_x_b[l], _block_diag(w_fnet[l]),
              _block_diag(w_pool[l]), pool_scale[l], w_out[l].astype(BF16), norm2_g[l],
              w_ff1[l].astype(BF16), w_ff2[l].astype(BF16))
        mod = mod_all[l].reshape(n_rows, 1, 6 * D_MODEL)
        final = l == DEPTH - 1
        xp, s_ctx = _layer(xp, bp, lp, mod, row_p, None, True, 4, consts_p, lw, final, final_norm_g, tm)
        new_states.append(s_ctx)
        xs, _ = _layer(xs, bs, ls, mod, row_s, state_wkv[:, l], False, 2, consts_s, lw, final,
                       final_norm_g, tm)
    new_state_wkv = jnp.stack(new_states, axis=1)
    return (xp.reshape(bp, lp, D_MODEL), xs.reshape(bs, ls, D_MODEL), new_state_wkv)
```

```python
import functools
import math

import numpy as np
import jax
import jax.numpy as jnp
from jax import lax
from jax.experimental import pallas as pl
from jax.experimental.pallas import tpu as pltpu

F32 = jnp.float32
BF16 = jnp.bfloat16

D_MODEL = 1024
DEPTH = 2
GRID_W = 64
D_RWKV = 512
HEAD = 64
N_HEADS = 8
LANES = 128
N_PAIRS = D_RWKV // LANES
D_FNET = 256
FNET_GW = 64
D_POOL = 256
POOL_GW = 64
POOL_WINDOWS = (2, 4, 8, 16)
LORA = 64
GATE_LORA = 128
SHIFT_END = 3 * D_RWKV + 2 * LORA + 2 * LORA + GATE_LORA
LORA_W = SHIFT_END - 3 * D_RWKV
D_IN = SHIFT_END + D_FNET + D_POOL
D_FF = 4 * D_MODEL
RMS_EPS = 1e-6
GN_EPS = 64e-5
LOG2E = 1.4426950408889634
CHUNK = 64
VMEM_LIMIT = 56 * 1024 * 1024
STREAM_UNITS = 16
STREAM_LAG = 8
ITER_UNITS = 32


def _dotf(a, b):
    return jnp.dot(a, b, preferred_element_type=F32, precision=lax.Precision.HIGHEST)


def _split(x, n):
    terms = []
    for _ in range(n - 1):
        hi = x.astype(BF16)
        terms.append(hi)
        x = x - hi.astype(F32)
    terms.append(x.astype(BF16))
    return terms


def _dots(a, b, na, nb, nt=False):
    dims = (((1,), (1,)), ((), ())) if nt else (((1,), (0,)), ((), ()))
    ta = _split(a, na)
    tb = _split(b, nb)
    out = None
    for i in range(na):
        for j in range(nb):
            if i + j < max(na, nb):
                t = lax.dot_general(ta[i], tb[j], dims, preferred_element_type=F32)
                out = t if out is None else out + t
    return out


def _mm(a, b):
    return jnp.dot(a.astype(BF16), b.astype(BF16), preferred_element_type=F32)


def _sigmoid(x):
    return 0.5 + 0.5 * jnp.tanh(0.5 * x)


def _rms(x, g):
    ms = jnp.mean(x * x, axis=-1, keepdims=True)
    return x * lax.rsqrt(ms + RMS_EPS) * g


_DONE = object()


def _interleave(streams, lag):
    pending = list(streams)
    active = []
    step = 0
    while pending or active:
        if pending and step % lag == 0:
            active.append(pending.pop(0))
        for s in list(active):
            if next(s, _DONE) is _DONE:
                active.remove(s)
        step += 1


def _ada_kernel(c_ref, w_ref, b_ref, o_ref):
    c = c_ref[...]
    s = c * _sigmoid(c)
    o_ref[0] = _dotf(s, w_ref[0]) + b_ref[0]


def _ada(cond, w_ada, b_ada):
    tn = 1536
    nrow = cond.shape[0]
    return pl.pallas_call(
        _ada_kernel,
        out_shape=jax.ShapeDtypeStruct((DEPTH, nrow, 6 * D_MODEL), F32),
        grid=(DEPTH, 6 * D_MODEL // tn),
        in_specs=[
            pl.BlockSpec((nrow, D_MODEL), lambda l, j: (0, 0)),
            pl.BlockSpec((1, D_MODEL, tn), lambda l, j: (l, 0, j)),
            pl.BlockSpec((1, 1, tn), lambda l, j: (l, 0, j)),
        ],
        out_specs=pl.BlockSpec((1, nrow, tn), lambda l, j: (l, 0, j)),
        compiler_params=pltpu.CompilerParams(
            dimension_semantics=("arbitrary", "arbitrary"), vmem_limit_bytes=VMEM_LIMIT),
        name="ada",
    )(cond, w_ada, b_ada.reshape(DEPTH, 1, 6 * D_MODEL))


def _pre_kernel(x_ref, mod_ref, g_ref, w_ref, oa_ref, of_ref, op_ref):
    mod = mod_ref[0]
    shift = mod[:, 0:D_MODEL]
    scale = mod[:, D_MODEL:2 * D_MODEL]
    h = _rms(x_ref[...], g_ref[...]) * (1.0 + scale) + shift
    p = jnp.dot(h.astype(BF16), w_ref[...], preferred_element_type=F32)
    oa_ref[...] = p[:, :SHIFT_END]
    of_ref[...] = p[:, SHIFT_END:SHIFT_END + D_FNET]
    op_ref[...] = p[:, SHIFT_END + D_FNET:]


def _resident(shape, layer):
    zeros = (0,) * len(shape)
    return pl.BlockSpec((None,) + shape, lambda *_: (layer,) + zeros, pipeline_mode=pl.Buffered(1))


def _pre(x, mod, mod_row, layer, g, w_in_b, tm):
    t = x.shape[0]
    return pl.pallas_call(
        _pre_kernel,
        out_shape=(jax.ShapeDtypeStruct((t, SHIFT_END), F32),
                   jax.ShapeDtypeStruct((t, D_FNET), F32),
                   jax.ShapeDtypeStruct((t, D_POOL), F32)),
        grid=(t // tm,),
        in_specs=[
            pl.BlockSpec((tm, D_MODEL), lambda i: (i, 0)),
            pl.BlockSpec((None, 1, 1, 6 * D_MODEL), lambda i: (layer, mod_row(i), 0, 0)),
            _resident((1, D_MODEL), layer),
            _resident((D_MODEL, D_IN), layer),
        ],
        out_specs=(pl.BlockSpec((tm, SHIFT_END), lambda i: (i, 0)),
                   pl.BlockSpec((tm, D_FNET), lambda i: (i, 0)),
                   pl.BlockSpec((tm, D_POOL), lambda i: (i, 0))),
        compiler_params=pltpu.CompilerParams(
            dimension_semantics=("arbitrary",), vmem_limit_bytes=VMEM_LIMIT),
        name="pre",
    )(x, mod, g.reshape(DEPTH, 1, D_MODEL), w_in_b)


def _shift_mix(p, mu):
    n = p.shape[0]
    row = lax.broadcasted_iota(jnp.int32, p.shape, 0)
    prev = jnp.where(row == 0, 0.0, pltpu.roll(p, 1, 0))
    nxt = jnp.where(row == n - 1, 0.0, pltpu.roll(p, n - 1, 0))
    return p * (1.0 - mu) + (0.5 * mu) * (prev + nxt)


def _rwkv_kernel(seq, pp, has_state, want_state, alias_state, layer, *refs):
    (r_ref, k_ref, v_ref, lo_ref, mur_ref, muk_ref, muv_ref, mul_ref, w0_ref, wup_ref, a0_ref,
     aup_ref, gup_ref, kk_ref, ka_ref, rk_ref, lnw_ref, lnb_ref) = refs[:18]
    n_in = 18 + has_state + alias_state
    s0_ref = refs[18] if has_state else None
    y_ref = refs[n_in]
    sfin_ref = refs[n_in + 1] if want_state else None
    lw_s, kd_s, b_s, r_s, v_s, kn_s, st_s, rb_s, yb_s, gt_s, ht_s = refs[n_in + 1 + want_state:]
    nc = seq // CHUNK
    C = CHUNK
    tiles = [slice(i * LANES, (i + 1) * LANES) for i in range(pp)]

    lane1 = lax.broadcasted_iota(jnp.int32, (1, LANES), 1)
    head0 = lane1 < HEAD
    head1 = jnp.logical_not(head0)
    ri = lax.broadcasted_iota(jnp.int32, (LANES, LANES), 0)
    ci = lax.broadcasted_iota(jnp.int32, (LANES, LANES), 1)
    same_head = (ri // HEAD) == (ci // HEAD)
    seg_ones = same_head.astype(F32)
    eye = (ri == ci).astype(F32)

    def seg_sum(x):
        return jnp.concatenate([_mm(x[:, t], seg_ones) for t in tiles], axis=1)

    r = _shift_mix(r_ref[0], mur_ref[...])
    k = _shift_mix(k_ref[0], muk_ref[...])
    v = _shift_mix(v_ref[0], muv_ref[...])
    lo = _shift_mix(lo_ref[0], mul_ref[...])
    wd = jnp.tanh(lo[:, 0:2 * LORA])
    ad = lo[:, 2 * LORA:4 * LORA]
    gd = _sigmoid(lo[:, 4 * LORA:])
    g = _mm(gd, gup_ref[...])
    kx = k * kk_ref[...]
    kn = kx * lax.rsqrt(jnp.maximum(seg_sum(kx * kx), 1e-24))
    zeros_up = jnp.zeros((LORA, pp * LANES), F32)
    kd_sum = jnp.zeros_like(k)
    for d in range(2):
        if d == 0:
            wup = jnp.concatenate([wup_ref[0], zeros_up], axis=0)
            aup = jnp.concatenate([aup_ref[0], zeros_up], axis=0)
        else:
            wup = jnp.concatenate([zeros_up, wup_ref[1]], axis=0)
            aup = jnp.concatenate([zeros_up, aup_ref[1]], axis=0)
        zw = w0_ref[d:d + 1, :] + _mm(wd, wup)
        lw2 = (-LOG2E * math.exp(-0.5)) * _sigmoid(zw)
        lw_s[d] = lw2
        a_sig = _sigmoid(a0_ref[d:d + 1, :] + _mm(ad, aup))
        kd = k * (1.0 + (a_sig - 1.0) * ka_ref[...])
        kd_s[d] = kd
        b_s[d] = kn * a_sig
        kn_s[d] = -kn * jnp.exp2(-lw2)
        kd_sum = kd_sum + kd
    r_s[...] = r
    v_s[...] = v
    bonus = seg_sum(r * kd_sum * rk_ref[...]) * v

    if has_state:
        rj = lax.broadcasted_iota(jnp.int32, (LANES, HEAD), 0)
        cj = lax.broadcasted_iota(jnp.int32, (LANES, HEAD), 1)
        place = [(rj == cj + j * HEAD).astype(F32) for j in range(2)]
        keys = [(d, pi, j) for d in range(2) for pi in range(pp) for j in range(2)]
        t1 = {key: _dots(place[key[2]], s0_ref[0, key[0], 2 * key[1] + key[2]], 1, 3, nt=True)
              for key in keys}
        t2 = {key: _dots(t1[key], place[key[2]], 3, 1, nt=True) for key in keys}
        for d in range(2):
            for pi in range(pp):
                st_s[d, pi] = t2[(d, pi, 0)] + t2[(d, pi, 1)]
    else:
        st_s[...] = jnp.zeros(st_s.shape, F32)

    tr = lax.broadcasted_iota(jnp.int32, (C, C), 0)
    tc = lax.broadcasted_iota(jnp.int32, (C, C), 1)
    t_p = lax.broadcasted_iota(jnp.int32, (C, LANES), 0)
    s_p = lax.broadcasted_iota(jnp.int32, (C, LANES), 1) % C
    tris = ((tc <= tr).astype(F32), (tc >= tr).astype(F32))
    strict = (s_p < t_p, s_p > t_p)
    inclusive = (s_p <= t_p, s_p >= t_p)
    eye_p = (s_p == t_p).astype(F32)
    txs = t_p ^ s_p
    lvl_map = sum(((txs >= (1 << bit)).astype(jnp.int32) for bit in range(1, 6)),
                  jnp.where(txs == 0, -1, 0))
    zero_b = jnp.zeros((C, LANES), BF16)

    def bd(x):
        return jnp.concatenate([jnp.where(head0, x, zero_b), jnp.where(head1, x, zero_b)], axis=0)

    def bdx(x):
        return jnp.concatenate([jnp.where(head1, x, zero_b), jnp.where(head0, x, zero_b)], axis=0)

    def mm(a, b):
        return jnp.dot(a, b, preferred_element_type=F32)

    def rows(c):
        return pl.ds(c * C, C) if isinstance(c, int) else pl.ds(pl.multiple_of(c * C, C), C)

    def algebra(units):
        u = []
        for c, pi, d in units:
            sl, ln = rows(c), tiles[pi]
            lw = lw_s[d, sl, ln]
            u.append(dict(c=c, pi=pi, d=d, sl=sl, ln=ln, lw=lw, cl=_dots(tris[d], lw, 1, 2)))
        yield
        for q in u:
            d, sl, ln, cl = q["d"], q["sl"], q["ln"], q["cl"]
            tot = cl[C - 1:C, :] if d == 0 else cl[0:1, :]
            e_in = jnp.exp2(cl)
            e_inv = jnp.exp2(-cl)
            p_c = jnp.exp2(tot)
            q["at"] = (kn_s[d, sl, ln] * e_in).astype(BF16)
            q["rt"] = r_s[sl, ln] * e_in
            bt = b_s[d, sl, ln] * e_inv
            kt = kd_s[d, sl, ln] * e_inv
            q["lhs_t"] = jnp.concatenate([bt * p_c, kt * p_c], axis=0).T.astype(BF16)
            bt = bt.astype(BF16)
            kt = kt.astype(BF16)
            q["rhs_nt"] = jnp.concatenate([jnp.where(head0, bt, zero_b), jnp.where(head1, bt, zero_b),
                                           jnp.where(head1, kt, zero_b), jnp.where(head0, kt, zero_b)], axis=0)
            q["p_c"] = p_c
            q["vc"] = v_s[sl, ln].astype(BF16)
            yield
        for q in u:
            lhs = jnp.concatenate([q["at"], q["rt"].astype(BF16)], axis=0)
            q["sc"] = lax.dot_general(lhs, q["rhs_nt"], (((1,), (1,)), ((), ())), preferred_element_type=F32)
        yield
        for q in u:
            d, sc = q["d"], q["sc"]
            q["lab"] = jnp.where(strict[d], sc[:C, :LANES], 0.0)
            lakx = jnp.where(strict[d], sc[:C, LANES:], 0.0)
            q["mr"] = jnp.concatenate([jnp.where(inclusive[d], sc[C:, :LANES], 0.0),
                                       jnp.where(inclusive[d], sc[C:, LANES:], 0.0)], axis=1).astype(BF16)
            q["xv"] = mm(lakx.astype(BF16), bdx(q["vc"]))
            q["t"] = eye_p + jnp.where(lvl_map == 0, q["lab"], 0.0)
        yield
        for lvl in range(1, 6):
            for q in u:
                q["tb"] = q["t"].astype(BF16)
                q["w"] = mm(jnp.where(lvl_map == lvl, q["lab"], 0.0).astype(BF16), bd(q["tb"]))
            yield
            for q in u:
                q["t"] = q["t"] + mm(q["tb"], bd(q["w"].astype(BF16)))
            yield
        for q in u:
            x = mm(q["t"].astype(BF16),
                   jnp.concatenate([bd(q["at"]), bd(q["xv"].astype(BF16))], axis=1))
            q["ab"] = x[:, :LANES].astype(BF16)
            q["ub"] = x[:, LANES:].astype(BF16)
        yield
        for q in u:
            rhs = jnp.concatenate([jnp.concatenate([bd(q["ab"]), bd(q["ub"])], axis=1),
                                   jnp.concatenate([jnp.concatenate([zero_b, zero_b], axis=0), bdx(q["vc"])],
                                                   axis=1)], axis=0)
            q["o2"] = mm(q["mr"], rhs)
            rhs2 = jnp.concatenate([jnp.concatenate([q["ab"], q["ub"]], axis=1),
                                    jnp.concatenate([zero_b, q["vc"]], axis=1)], axis=0)
            q["gh"] = mm(q["lhs_t"], rhs2)
        yield
        for q in u:
            c, pi, d, sl, ln, o2, gh = q["c"], q["pi"], q["d"], q["sl"], q["ln"], q["o2"], q["gh"]
            rb_s[d, sl, ln] = (q["rt"] + o2[:, :LANES]).astype(BF16)
            yb_s[d, sl, ln] = o2[:, LANES:]
            gt_s[d, pi, c] = (jnp.where(same_head, gh[:, :LANES], 0.0) + eye * q["p_c"]).astype(BF16)
            ht_s[d, pi, c] = jnp.where(same_head, gh[:, LANES:], 0.0)
        yield

    group = min(nc, max(1, ITER_UNITS // (2 * pp)))

    def algebra_step(i, carry):
        units = [(i * group + gi, pi, d) for gi in range(group) for pi in range(pp) for d in range(2)]
        streams = [algebra(units[s:s + STREAM_UNITS]) for s in range(0, len(units), STREAM_UNITS)]
        _interleave(streams, STREAM_LAG)
        return carry

    if nc == group:
        algebra_step(0, 0)
    else:
        lax.fori_loop(0, nc // group, algebra_step, 0)

    y_ref[0] = jnp.zeros(y_ref.shape[1:], F32)

    def state_step(i, carry):
        items = []
        for pi in range(pp):
            for d in range(2):
                c = i if d == 0 else nc - 1 - i
                sl, ln = rows(c), tiles[pi]
                lhs = jnp.concatenate([rb_s[d, sl, ln], gt_s[d, pi, c]], axis=0)
                items.append((pi, d, c, sl, ln, mm(lhs, st_s[d, pi].astype(BF16))))
        for pi, d, c, sl, ln, o in items:
            y_ref[0, sl, ln] += o[:C] + yb_s[d, sl, ln]
            st_s[d, pi] = o[C:] + ht_s[d, pi, c]
        return carry

    lax.fori_loop(0, nc, state_step, 0)

    if want_state:
        if alias_state:
            out = sfin_ref.at[0]
        else:
            out = sfin_ref.at[0, layer]
            for other in range(DEPTH):
                if other != layer:
                    sfin_ref[0, other] = jnp.zeros(sfin_ref.shape[2:], F32)
        for d in range(2):
            for pi in range(pp):
                stt = st_s[d, pi].T
                out[d, 2 * pi] = stt[:HEAD, :HEAD]
                out[d, 2 * pi + 1] = pltpu.roll(stt[HEAD:], HEAD, 1)[:, :HEAD]

    y = y_ref[0]
    yc = y - seg_sum(y) * (1.0 / HEAD)
    var = seg_sum(yc * yc) * (1.0 / HEAD)
    yn = yc * lax.rsqrt(var + GN_EPS) * lnw_ref[...] + lnb_ref[...]
    y_ref[0] = (yn + bonus) * g


def _rwkv(pa, layer, s0, s_prev, want_state, pp, mu, w0, w_up, a0, a_up, g_up, k_k, k_a, r_k, ln_w, ln_b):
    bsz, seq, _ = pa.shape
    has_state = s0 is not None
    alias_state = want_state and s_prev is not None
    width = pp * LANES
    nblk = D_RWKV // width
    nc = seq // CHUNK
    mu3 = mu.reshape(DEPTH, 1, SHIFT_END)
    row = lambda a: a.reshape(DEPTH, 1, D_RWKV)
    col = lambda o: (lambda b, p: (b, 0, o * nblk + p))
    vec = lambda b, p: (layer, 0, p)
    lora_blk = 3 * D_RWKV // LORA_W
    in_specs = [
        pl.BlockSpec((1, seq, width), col(0)),
        pl.BlockSpec((1, seq, width), col(1)),
        pl.BlockSpec((1, seq, width), col(2)),
        pl.BlockSpec((1, seq, LORA_W), lambda b, p: (b, 0, lora_blk)),
        pl.BlockSpec((None, 1, width), lambda b, p: (layer, 0, p)),
        pl.BlockSpec((None, 1, width), lambda b, p: (layer, 0, nblk + p)),
        pl.BlockSpec((None, 1, width), lambda b, p: (layer, 0, 2 * nblk + p)),
        pl.BlockSpec((None, 1, LORA_W), lambda b, p: (layer, 0, lora_blk)),
        pl.BlockSpec((None, 2, width), vec),
        pl.BlockSpec((None, 2, LORA, width), lambda b, p: (layer, 0, 0, p)),
        pl.BlockSpec((None, 2, width), vec),
        pl.BlockSpec((None, 2, LORA, width), lambda b, p: (layer, 0, 0, p)),
        pl.BlockSpec((None, GATE_LORA, width), vec),
        pl.BlockSpec((None, 1, width), vec),
        pl.BlockSpec((None, 1, width), vec),
        pl.BlockSpec((None, 1, width), vec),
        pl.BlockSpec((None, 1, width), vec),
        pl.BlockSpec((None, 1, width), vec),
    ]
    args = [pa, pa, pa, pa, mu3, mu3, mu3, mu3, w0, w_up, a0, a_up, g_up, row(k_k), row(k_a),
            row(r_k), row(ln_w), row(ln_b)]
    state_spec = pl.BlockSpec((1, None, 2, 2 * pp, HEAD, HEAD), lambda b, p: (b, layer, 0, p, 0, 0))
    if has_state:
        in_specs.append(state_spec)
        args.append(s0)
    aliases = {}
    if alias_state:
        aliases[len(args)] = 1
        in_specs.append(pl.BlockSpec(memory_space=pl.ANY))
        args.append(s_prev)
    out_shape = [jax.ShapeDtypeStruct((bsz, seq, D_RWKV), F32)]
    out_specs = [pl.BlockSpec((1, seq, width), lambda b, p: (b, 0, p))]
    if want_state:
        out_shape.append(jax.ShapeDtypeStruct((bsz, DEPTH, 2, N_HEADS, HEAD, HEAD), F32))
        if alias_state:
            out_specs.append(state_spec)
        else:
            out_specs.append(pl.BlockSpec((1, DEPTH, 2, 2 * pp, HEAD, HEAD), lambda b, p: (b, 0, 0, p, 0, 0)))
    outs = pl.pallas_call(
        functools.partial(_rwkv_kernel, seq, pp, has_state, want_state, alias_state, layer),
        out_shape=tuple(out_shape),
        grid=(bsz, nblk),
        in_specs=in_specs,
        out_specs=tuple(out_specs),
        input_output_aliases=aliases,
        scratch_shapes=[
            pltpu.VMEM((2, seq, width), F32),
            pltpu.VMEM((2, seq, width), F32),
            pltpu.VMEM((2, seq, width), F32),
            pltpu.VMEM((seq, width), F32),
            pltpu.VMEM((seq, width), F32),
            pltpu.VMEM((2, seq, width), F32),
            pltpu.VMEM((2, pp, LANES, LANES), F32),
            pltpu.VMEM((2, seq, width), BF16),
            pltpu.VMEM((2, seq, width), F32),
            pltpu.VMEM((2, pp, nc, LANES, LANES), BF16),
            pltpu.VMEM((2, pp, nc, LANES, LANES), F32),
        ],
        compiler_params=pltpu.CompilerParams(
            dimension_semantics=("arbitrary", "arbitrary"), vmem_limit_bytes=VMEM_LIMIT),
        name="rwkv",
    )(*args)
    return (outs[0], outs[1]) if want_state else (outs[0], None)


def _fpool_kernel(seq, f_ref, p_ref, cs_ref, c64_ref, s64_ref, wf_ref, band_ref, inv_ref, wp_ref,
                  ps_ref, yb_ref, yc_ref, cw_s, cs_s):
    @pl.when(pl.program_id(0) == 0)
    def _():
        wf = wf_ref[...]
        cw_s[:, :D_FNET] = _dotf(c64_ref[...], wf).astype(BF16)
        cw_s[:, D_FNET:] = (-_dotf(s64_ref[...], wf)).astype(BF16)
        cs_s[...] = cs_ref[...].astype(BF16)

    g = jnp.dot(f_ref[0].astype(BF16), cw_s[...], preferred_element_type=F32).astype(BF16)
    yb_ref[0] = jnp.dot(cs_s[...], jnp.concatenate([g[:, :D_FNET], g[:, D_FNET:]], axis=0),
                        preferred_element_type=F32)

    p = p_ref[0]
    p_b = p.astype(BF16)
    lane = lax.broadcasted_iota(jnp.int32, (seq, D_POOL), 1)
    d = jnp.zeros((seq, D_POOL), F32)
    for i in range(len(POOL_WINDOWS)):
        s = jnp.dot(band_ref[i], p_b, preferred_element_type=F32)
        d = jnp.where((lane // POOL_GW) == i, s, d)
    d = d * inv_ref[...] - p
    yc_ref[0] = _mm(d, wp_ref[...]) * ps_ref[...]


def _fpool(pf, pp, consts, wf_bd, wp_bd, pool_scale):
    bsz, seq, _ = pf.shape
    cs, c64, s64, band, inv = consts
    full2 = lambda b: (0, 0)
    return pl.pallas_call(
        functools.partial(_fpool_kernel, seq),
        out_shape=(jax.ShapeDtypeStruct((bsz, seq, D_FNET), F32),
                   jax.ShapeDtypeStruct((bsz, seq, D_POOL), F32)),
        grid=(bsz,),
        in_specs=[
            pl.BlockSpec((1, seq, D_FNET), lambda b: (b, 0, 0)),
            pl.BlockSpec((1, seq, D_POOL), lambda b: (b, 0, 0)),
            pl.BlockSpec((seq, 2 * seq), full2),
            pl.BlockSpec((D_FNET, D_FNET), full2),
            pl.BlockSpec((D_FNET, D_FNET), full2),
            pl.BlockSpec((D_FNET, D_FNET), full2),
            pl.BlockSpec((len(POOL_WINDOWS), seq, seq), lambda b: (0, 0, 0)),
            pl.BlockSpec((seq, D_POOL), full2),
            pl.BlockSpec((D_POOL, D_POOL), full2),
            pl.BlockSpec((1, D_POOL), full2),
        ],
        out_specs=(pl.BlockSpec((1, seq, D_FNET), lambda b: (b, 0, 0)),
                   pl.BlockSpec((1, seq, D_POOL), lambda b: (b, 0, 0))),
        scratch_shapes=[pltpu.VMEM((D_FNET, 2 * D_FNET), BF16),
                        pltpu.VMEM((seq, 2 * seq), BF16)],
        compiler_params=pltpu.CompilerParams(
            dimension_semantics=("arbitrary",), vmem_limit_bytes=VMEM_LIMIT),
        name="fpool",
    )(pf, pp, cs, c64, s64, wf_bd, band, inv, wp_bd, pool_scale.reshape(1, D_POOL))


def _window_matrix(n, w):
    t = np.arange(n)
    lo = np.clip(t - w // 2, 0, n)
    hi = np.clip(t + w - w // 2, 0, n)
    s = np.arange(n)
    return ((s[None, :] >= lo[:, None]) & (s[None, :] < hi[:, None])).astype(np.float64)


def _mixer_constants(seq, grid):
    n = np.arange(seq)
    ang = 2.0 * np.pi * ((n[:, None] * n[None, :]) % seq) / seq
    scale = 1.0 / math.sqrt(seq * FNET_GW)
    cs = np.concatenate([np.cos(ang), np.sin(ang)], axis=1) * scale
    m = np.arange(FNET_GW)
    ang64 = 2.0 * np.pi * ((m[:, None] * m[None, :]) % FNET_GW) / FNET_GW
    groups = D_FNET // FNET_GW
    c64 = np.kron(np.eye(groups), np.cos(ang64))
    s64 = np.kron(np.eye(groups), np.sin(ang64))
    bands = []
    invs = []
    for w in POOL_WINDOWS:
        if grid:
            rows = seq // GRID_W
            band = np.kron(_window_matrix(rows, w), _window_matrix(GRID_W, w))
        else:
            band = _window_matrix(seq, w)
        bands.append(band)
        invs.append(np.repeat((1.0 / band.sum(axis=1))[:, None], POOL_GW, axis=1))
    band = jnp.asarray(np.stack(bands), dtype=BF16)
    inv = jnp.asarray(np.concatenate(invs, axis=1), dtype=F32)
    return (jnp.asarray(cs, dtype=F32), jnp.asarray(c64, dtype=F32), jnp.asarray(s64, dtype=F32),
            band, inv)


def _block_diag(w):
    g, n, _ = w.shape
    return (w[:, :, None, :] * jnp.eye(g, dtype=w.dtype)[:, None, :, None]).reshape(g * n, g * n)


def _post_kernel(final, x_ref, ya_ref, yb_ref, yc_ref, mod_ref, wo_ref, g2_ref, w1_ref, w2_ref,
                 fg_ref, o_ref):
    mod = mod_ref[0]
    gate1 = mod[:, 2 * D_MODEL:3 * D_MODEL]
    shift2 = mod[:, 3 * D_MODEL:4 * D_MODEL]
    scale2 = mod[:, 4 * D_MODEL:5 * D_MODEL]
    gate2 = mod[:, 5 * D_MODEL:]
    mix = (jnp.dot(ya_ref[...].astype(BF16), wo_ref[0:D_RWKV, :], preferred_element_type=F32)
           + jnp.dot(yb_ref[...].astype(BF16), wo_ref[D_RWKV:D_RWKV + D_FNET, :],
                     preferred_element_type=F32)
           + jnp.dot(yc_ref[...].astype(BF16), wo_ref[D_RWKV + D_FNET:, :],
                     preferred_element_type=F32))
    x1 = x_ref[...] + gate1 * mix
    h2 = (_rms(x1, g2_ref[...]) * (1.0 + scale2) + shift2).astype(BF16)
    tf = 1024
    acc = jnp.zeros_like(x1)
    for j in range(D_FF // tf):
        ff = jnp.dot(h2, w1_ref[:, j * tf:(j + 1) * tf], preferred_element_type=F32)
        ff = jnp.square(jnp.maximum(ff, 0.0))
        acc = acc + jnp.dot(ff.astype(BF16), w2_ref[j * tf:(j + 1) * tf, :],
                            preferred_element_type=F32)
    x2 = x1 + gate2 * acc
    if final:
        x2 = _rms(x2, fg_ref[...])
    o_ref[...] = x2


def _post(x, ya, yb, yc, mod, mod_row, layer, w_out_b, g2, w1_b, w2_b, fg, final, tm):
    t = x.shape[0]
    tok = lambda i: (i, 0)
    full = lambda i: (0, 0)
    return pl.pallas_call(
        functools.partial(_post_kernel, final),
        out_shape=jax.ShapeDtypeStruct((t, D_MODEL), F32),
        grid=(t // tm,),
        in_specs=[
            pl.BlockSpec((tm, D_MODEL), tok),
            pl.BlockSpec((tm, D_RWKV), tok),
            pl.BlockSpec((tm, D_FNET), tok),
            pl.BlockSpec((tm, D_POOL), tok),
            pl.BlockSpec((None, 1, 1, 6 * D_MODEL), lambda i: (layer, mod_row(i), 0, 0)),
            _resident((D_MODEL, D_MODEL), layer),
            _resident((1, D_MODEL), layer),
            _resident((D_MODEL, D_FF), layer),
            _resident((D_FF, D_MODEL), layer),
            pl.BlockSpec((1, D_MODEL), full),
        ],
        out_specs=pl.BlockSpec((tm, D_MODEL), tok),
        compiler_params=pltpu.CompilerParams(
            dimension_semantics=("arbitrary",), vmem_limit_bytes=VMEM_LIMIT),
        name="post",
    )(x, ya, yb, yc, mod, w_out_b, g2.reshape(DEPTH, 1, D_MODEL), w1_b, w2_b, fg.reshape(1, D_MODEL))


def _layer(x, bsz, seq, layer, mod, mod_row, s0, s_prev, want_state, pairs, consts, sw, mixw, fg, tm):
    (norm1_g, w_in_b, mu_shift, w0, w_up, a0, a_up, g_up, k_k, k_a, r_k, ln_w, ln_b, w_out_b, norm2_g,
     w1_b, w2_b) = sw
    wf_bd, wp_bd, pool_scale = mixw
    pa, pf, pp = _pre(x, mod, mod_row, layer, norm1_g, w_in_b, tm)
    ya, s_fin = _rwkv(pa.reshape(bsz, seq, SHIFT_END), layer, s0, s_prev, want_state, pairs, mu_shift, w0,
                      w_up, a0, a_up, g_up, k_k, k_a, r_k, ln_w, ln_b)
    yb, yc = _fpool(pf.reshape(bsz, seq, D_FNET), pp.reshape(bsz, seq, D_POOL), consts, wf_bd,
                    wp_bd, pool_scale)
    x = _post(x, ya.reshape(bsz * seq, D_RWKV), yb.reshape(bsz * seq, D_FNET),
              yc.reshape(bsz * seq, D_POOL), mod, mod_row, layer, w_out_b, norm2_g, w1_b, w2_b, fg,
              layer == DEPTH - 1, tm)
    return x, s_fin


def kernel(x_prompt, x_sample, state_wkv, c, c_ctx, w_ada, b_ada, norm1_g, w_in, mu_shift, w0, w_up, a0, a_up, g_up, k_k, k_a, r_k, ln_x_w, ln_x_b, w_fnet, w_pool, pool_scale, w_out, norm2_g, w_ff1, w_ff2, final_norm_g):
    bp, lp, _ = x_prompt.shape
    bs, ls, _ = x_sample.shape
    tm = 512
    n_rows = 8
    cond = jnp.concatenate([c_ctx[None], c, jnp.zeros((n_rows - 1 - bs, D_MODEL), F32)], axis=0)
    mod_all = _ada(cond, w_ada, b_ada)
    consts_p = _mixer_constants(lp, False)
    consts_s = _mixer_constants(ls, True)
    xp = x_prompt.reshape(bp * lp, D_MODEL)
    xs = x_sample.reshape(bs * ls, D_MODEL)
    tiles_per_seq = ls // tm
    row_p = lambda i: 0
    row_s = lambda i: 1 + i // tiles_per_seq
    sw = (norm1_g, w_in.astype(BF16), mu_shift, w0, w_up, a0, a_up, g_up, k_k, k_a, r_k, ln_x_w, ln_x_b,
          w_out.astype(BF16), norm2_g, w_ff1.astype(BF16), w_ff2.astype(BF16))
    mod = mod_all.reshape(DEPTH, n_rows, 1, 6 * D_MODEL)
    new_state_wkv = None
    for l in range(DEPTH):
        mixw = (_block_diag(w_fnet[l]), _block_diag(w_pool[l]), pool_scale[l])
        xp, new_state_wkv = _layer(xp, bp, lp, l, mod, row_p, None, new_state_wkv, True, 4, consts_p, sw,
                                   mixw, final_norm_g, tm)
        xs, _ = _layer(xs, bs, ls, l, mod, row_s, state_wkv, None, False, 2, consts_s, sw, mixw,
                       final_norm_g, tm)
    return (xp.reshape(bp, lp, D_MODEL), xs.reshape(bs, ls, D_MODEL), new_state_wkv)
```

```python
import functools
import math

import numpy as np
import jax
import jax.numpy as jnp
from jax import lax
from jax.experimental import pallas as pl
from jax.experimental.pallas import tpu as pltpu

F32 = jnp.float32
BF16 = jnp.bfloat16

D_MODEL = 1024
DEPTH = 2
GRID_W = 64
D_RWKV = 512
HEAD = 64
N_HEADS = 8
LANES = 128
N_PAIRS = D_RWKV // LANES
D_FNET = 256
FNET_GW = 64
D_POOL = 256
POOL_GW = 64
POOL_WINDOWS = (2, 4, 8, 16)
LORA = 64
GATE_LORA = 128
SHIFT_END = 3 * D_RWKV + 2 * LORA + 2 * LORA + GATE_LORA
LORA_W = SHIFT_END - 3 * D_RWKV
D_IN = SHIFT_END + D_FNET + D_POOL
D_FF = 4 * D_MODEL
RMS_EPS = 1e-6
GN_EPS = 64e-5
LOG2E = 1.4426950408889634
CHUNK = 64
VMEM_LIMIT = 56 * 1024 * 1024
STREAM_UNITS = 16
STREAM_LAG = 8
ITER_UNITS = 32


def _dotf(a, b):
    return jnp.dot(a, b, preferred_element_type=F32, precision=lax.Precision.HIGHEST)


def _split(x, n):
    terms = []
    for _ in range(n - 1):
        hi = x.astype(BF16)
        terms.append(hi)
        x = x - hi.astype(F32)
    terms.append(x.astype(BF16))
    return terms


def _dots(a, b, na, nb, nt=False):
    dims = (((1,), (1,)), ((), ())) if nt else (((1,), (0,)), ((), ()))
    ta = _split(a, na)
    tb = _split(b, nb)
    out = None
    for i in range(na):
        for j in range(nb):
            if i + j < max(na, nb):
                t = lax.dot_general(ta[i], tb[j], dims, preferred_element_type=F32)
                out = t if out is None else out + t
    return out


def _mm(a, b):
    return jnp.dot(a.astype(BF16), b.astype(BF16), preferred_element_type=F32)


def _sigmoid(x):
    return 0.5 + 0.5 * jnp.tanh(0.5 * x)


def _rms(x, g):
    ms = jnp.mean(x * x, axis=-1, keepdims=True)
    return x * lax.rsqrt(ms + RMS_EPS) * g


_DONE = object()


def _interleave(streams, lag):
    pending = list(streams)
    active = []
    step = 0
    while pending or active:
        if pending and step % lag == 0:
            active.append(pending.pop(0))
        for s in list(active):
            if next(s, _DONE) is _DONE:
                active.remove(s)
        step += 1


def _ada_kernel(c_ref, w_ref, b_ref, o_ref):
    c = c_ref[...]
    s = c * _sigmoid(c)
    o_ref[0] = _dotf(s, w_ref[0]) + b_ref[0]


def _ada(cond, w_ada, b_ada):
    tn = 1536
    nrow = cond.shape[0]
    return pl.pallas_call(
        _ada_kernel,
        out_shape=jax.ShapeDtypeStruct((DEPTH, nrow, 6 * D_MODEL), F32),
        grid=(DEPTH, 6 * D_MODEL // tn),
        in_specs=[
            pl.BlockSpec((nrow, D_MODEL), lambda l, j: (0, 0)),
            pl.BlockSpec((1, D_MODEL, tn), lambda l, j: (l, 0, j)),
            pl.BlockSpec((1, 1, tn), lambda l, j: (l, 0, j)),
        ],
        out_specs=pl.BlockSpec((1, nrow, tn), lambda l, j: (l, 0, j)),
        compiler_params=pltpu.CompilerParams(
            dimension_semantics=("arbitrary", "arbitrary"), vmem_limit_bytes=VMEM_LIMIT),
        name="ada",
    )(cond, w_ada, b_ada.reshape(DEPTH, 1, 6 * D_MODEL))


def _shift_mix(p, mu, period):
    n = p.shape[0]
    pos = lax.broadcasted_iota(jnp.int32, p.shape, 0) % period
    prev = jnp.where(pos == 0, 0.0, pltpu.roll(p, 1, 0))
    nxt = jnp.where(pos == period - 1, 0.0, pltpu.roll(p, n - 1, 0))
    return p * (1.0 - mu) + (0.5 * mu) * (prev + nxt)


def _pre_kernel(shift_period, x_ref, mod_ref, g_ref, w_ref, mu_ref, oa_ref, of_ref, op_ref):
    mod = mod_ref[0]
    shift = mod[:, 0:D_MODEL]
    scale = mod[:, D_MODEL:2 * D_MODEL]
    h = _rms(x_ref[...], g_ref[...]) * (1.0 + scale) + shift
    p = jnp.dot(h.astype(BF16), w_ref[...], preferred_element_type=F32)
    pa = p[:, :SHIFT_END]
    if shift_period:
        pa = _shift_mix(pa, mu_ref[...], shift_period)
    oa_ref[...] = pa
    of_ref[...] = p[:, SHIFT_END:SHIFT_END + D_FNET]
    op_ref[...] = p[:, SHIFT_END + D_FNET:]


def _resident(shape, layer):
    zeros = (0,) * len(shape)
    return pl.BlockSpec((None,) + shape, lambda *_: (layer,) + zeros, pipeline_mode=pl.Buffered(1))


def _pre(x, mod, mod_row, layer, g, w_in_b, mu, seq, tm):
    t = x.shape[0]
    shifted = tm % seq == 0
    outs = pl.pallas_call(
        functools.partial(_pre_kernel, seq if shifted else 0),
        out_shape=(jax.ShapeDtypeStruct((t, SHIFT_END), F32),
                   jax.ShapeDtypeStruct((t, D_FNET), F32),
                   jax.ShapeDtypeStruct((t, D_POOL), F32)),
        grid=(t // tm,),
        in_specs=[
            pl.BlockSpec((tm, D_MODEL), lambda i: (i, 0)),
            pl.BlockSpec((None, 1, 1, 6 * D_MODEL), lambda i: (layer, mod_row(i), 0, 0)),
            _resident((1, D_MODEL), layer),
            _resident((D_MODEL, D_IN), layer),
            _resident((1, SHIFT_END), layer),
        ],
        out_specs=(pl.BlockSpec((tm, SHIFT_END), lambda i: (i, 0)),
                   pl.BlockSpec((tm, D_FNET), lambda i: (i, 0)),
                   pl.BlockSpec((tm, D_POOL), lambda i: (i, 0))),
        compiler_params=pltpu.CompilerParams(
            dimension_semantics=("arbitrary",), vmem_limit_bytes=VMEM_LIMIT),
        name="pre",
    )(x, mod, g.reshape(DEPTH, 1, D_MODEL), w_in_b, mu.reshape(DEPTH, 1, SHIFT_END))
    return outs, shifted


def _rwkv_kernel(seq, pp, shifted, has_state, want_state, alias_state, layer, *refs):
    (r_ref, k_ref, v_ref, lo_ref, mur_ref, muk_ref, muv_ref, mul_ref, w0_ref, wup_ref, a0_ref,
     aup_ref, gup_ref, kk_ref, ka_ref, rk_ref, lnw_ref, lnb_ref) = refs[:18]
    n_in = 18 + has_state + alias_state
    s0_ref = refs[18] if has_state else None
    y_ref = refs[n_in]
    sfin_ref = refs[n_in + 1] if want_state else None
    lw_s, kd_s, b_s, r_s, v_s, kn_s, st_s, rb_s, yb_s, gt_s, ht_s = refs[n_in + 1 + want_state:]
    nc = seq // CHUNK
    C = CHUNK
    tiles = [slice(i * LANES, (i + 1) * LANES) for i in range(pp)]

    lane1 = lax.broadcasted_iota(jnp.int32, (1, LANES), 1)
    head0 = lane1 < HEAD
    head1 = jnp.logical_not(head0)
    ri = lax.broadcasted_iota(jnp.int32, (LANES, LANES), 0)
    ci = lax.broadcasted_iota(jnp.int32, (LANES, LANES), 1)
    same_head = (ri // HEAD) == (ci // HEAD)
    seg_ones = same_head.astype(F32)
    eye = (ri == ci).astype(F32)

    def seg_sum(x):
        return jnp.concatenate([_mm(x[:, t], seg_ones) for t in tiles], axis=1)

    if shifted:
        r, k, v, lo = r_ref[0], k_ref[0], v_ref[0], lo_ref[0]
    else:
        r = _shift_mix(r_ref[0], mur_ref[...], seq)
        k = _shift_mix(k_ref[0], muk_ref[...], seq)
        v = _shift_mix(v_ref[0], muv_ref[...], seq)
        lo = _shift_mix(lo_ref[0], mul_ref[...], seq)
    wd = jnp.tanh(lo[:, 0:2 * LORA])
    ad = lo[:, 2 * LORA:4 * LORA]
    gd = _sigmoid(lo[:, 4 * LORA:])
    g = _mm(gd, gup_ref[...])
    kx = k * kk_ref[...]
    kn = kx * lax.rsqrt(jnp.maximum(seg_sum(kx * kx), 1e-24))
    zeros_up = jnp.zeros((LORA, pp * LANES), F32)
    kd_sum = jnp.zeros_like(k)
    for d in range(2):
        if d == 0:
            wup = jnp.concatenate([wup_ref[0], zeros_up], axis=0)
            aup = jnp.concatenate([aup_ref[0], zeros_up], axis=0)
        else:
            wup = jnp.concatenate([zeros_up, wup_ref[1]], axis=0)
            aup = jnp.concatenate([zeros_up, aup_ref[1]], axis=0)
        zw = w0_ref[d:d + 1, :] + _mm(wd, wup)
        lw2 = (-LOG2E * math.exp(-0.5)) * _sigmoid(zw)
        lw_s[d] = lw2
        a_sig = _sigmoid(a0_ref[d:d + 1, :] + _mm(ad, aup))
        kd = k * (1.0 + (a_sig - 1.0) * ka_ref[...])
        kd_s[d] = kd
        b_s[d] = kn * a_sig
        kn_s[d] = -kn * jnp.exp2(-lw2)
        kd_sum = kd_sum + kd
    r_s[...] = r
    v_s[...] = v
    bonus = seg_sum(r * kd_sum * rk_ref[...]) * v

    if has_state:
        rj = lax.broadcasted_iota(jnp.int32, (LANES, HEAD), 0)
        cj = lax.broadcasted_iota(jnp.int32, (LANES, HEAD), 1)
        place = [(rj == cj + j * HEAD).astype(F32) for j in range(2)]
        keys = [(d, pi, j) for d in range(2) for pi in range(pp) for j in range(2)]
        t1 = {key: _dots(place[key[2]], s0_ref[0, key[0], 2 * key[1] + key[2]], 1, 3, nt=True)
              for key in keys}
        t2 = {key: _dots(t1[key], place[key[2]], 3, 1, nt=True) for key in keys}
        for d in range(2):
            for pi in range(pp):
                st_s[d, pi] = t2[(d, pi, 0)] + t2[(d, pi, 1)]
    else:
        st_s[...] = jnp.zeros(st_s.shape, F32)

    tr = lax.broadcasted_iota(jnp.int32, (C, C), 0)
    tc = lax.broadcasted_iota(jnp.int32, (C, C), 1)
    t_p = lax.broadcasted_iota(jnp.int32, (C, LANES), 0)
    s_p = lax.broadcasted_iota(jnp.int32, (C, LANES), 1) % C
    tris = ((tc <= tr).astype(F32), (tc >= tr).astype(F32))
    strict = (s_p < t_p, s_p > t_p)
    inclusive = (s_p <= t_p, s_p >= t_p)
    eye_p = (s_p == t_p).astype(F32)
    txs = t_p ^ s_p
    lvl_map = sum(((txs >= (1 << bit)).astype(jnp.int32) for bit in range(1, 6)),
                  jnp.where(txs == 0, -1, 0))
    zero_b = jnp.zeros((C, LANES), BF16)

    def bd(x):
        return jnp.concatenate([jnp.where(head0, x, zero_b), jnp.where(head1, x, zero_b)], axis=0)

    def bdx(x):
        return jnp.concatenate([jnp.where(head1, x, zero_b), jnp.where(head0, x, zero_b)], axis=0)

    def mm(a, b):
        return jnp.dot(a, b, preferred_element_type=F32)

    def rows(c):
        return pl.ds(c * C, C) if isinstance(c, int) else pl.ds(pl.multiple_of(c * C, C), C)

    def algebra(units):
        u = []
        for c, pi, d in units:
            sl, ln = rows(c), tiles[pi]
            lw = lw_s[d, sl, ln]
            u.append(dict(c=c, pi=pi, d=d, sl=sl, ln=ln, lw=lw, cl=_dots(tris[d], lw, 1, 2)))
        yield
        for q in u:
            d, sl, ln, cl = q["d"], q["sl"], q["ln"], q["cl"]
            tot = cl[C - 1:C, :] if d == 0 else cl[0:1, :]
            e_in = jnp.exp2(cl)
            e_inv = jnp.exp2(-cl)
            p_c = jnp.exp2(tot)
            q["at"] = (kn_s[d, sl, ln] * e_in).astype(BF16)
            q["rt"] = r_s[sl, ln] * e_in
            bt = b_s[d, sl, ln] * e_inv
            kt = kd_s[d, sl, ln] * e_inv
            q["lhs_t"] = jnp.concatenate([bt * p_c, kt * p_c], axis=0).T.astype(BF16)
            bt = bt.astype(BF16)
            kt = kt.astype(BF16)
            q["rhs_nt"] = jnp.concatenate([jnp.where(head0, bt, zero_b), jnp.where(head1, bt, zero_b),
                                           jnp.where(head1, kt, zero_b), jnp.where(head0, kt, zero_b)], axis=0)
            q["p_c"] = p_c
            q["vc"] = v_s[sl, ln].astype(BF16)
            yield
        for q in u:
            lhs = jnp.concatenate([q["at"], q["rt"].astype(BF16)], axis=0)
            q["sc"] = lax.dot_general(lhs, q["rhs_nt"], (((1,), (1,)), ((), ())), preferred_element_type=F32)
        yield
        for q in u:
            d, sc = q["d"], q["sc"]
            q["lab"] = jnp.where(strict[d], sc[:C, :LANES], 0.0)
            lakx = jnp.where(strict[d], sc[:C, LANES:], 0.0)
            q["mr"] = jnp.concatenate([jnp.where(inclusive[d], sc[C:, :LANES], 0.0),
                                       jnp.where(inclusive[d], sc[C:, LANES:], 0.0)], axis=1).astype(BF16)
            q["xv"] = mm(lakx.astype(BF16), bdx(q["vc"]))
            q["t"] = eye_p + jnp.where(lvl_map == 0, q["lab"], 0.0)
        yield
        for lvl in range(1, 6):
            for q in u:
                q["tb"] = q["t"].astype(BF16)
                q["w"] = mm(jnp.where(lvl_map == lvl, q["lab"], 0.0).astype(BF16), bd(q["tb"]))
            yield
            for q in u:
                q["t"] = q["t"] + mm(q["tb"], bd(q["w"].astype(BF16)))
            yield
        for q in u:
            x = mm(q["t"].astype(BF16),
                   jnp.concatenate([bd(q["at"]), bd(q["xv"].astype(BF16))], axis=1))
            q["ab"] = x[:, :LANES].astype(BF16)
            q["ub"] = x[:, LANES:].astype(BF16)
        yield
        for q in u:
            rhs = jnp.concatenate([jnp.concatenate([bd(q["ab"]), bd(q["ub"])], axis=1),
                                   jnp.concatenate([jnp.concatenate([zero_b, zero_b], axis=0), bdx(q["vc"])],
                                                   axis=1)], axis=0)
            q["o2"] = mm(q["mr"], rhs)
            rhs2 = jnp.concatenate([jnp.concatenate([q["ab"], q["ub"]], axis=1),
                                    jnp.concatenate([zero_b, q["vc"]], axis=1)], axis=0)
            q["gh"] = mm(q["lhs_t"], rhs2)
        yield
        for q in u:
            c, pi, d, sl, ln, o2, gh = q["c"], q["pi"], q["d"], q["sl"], q["ln"], q["o2"], q["gh"]
            rb_s[d, sl, ln] = (q["rt"] + o2[:, :LANES]).astype(BF16)
            yb_s[d, sl, ln] = o2[:, LANES:]
            gt_s[d, pi, c] = (jnp.where(same_head, gh[:, :LANES], 0.0) + eye * q["p_c"]).astype(BF16)
            ht_s[d, pi, c] = jnp.where(same_head, gh[:, LANES:], 0.0)
        yield

    group = min(nc, max(1, ITER_UNITS // (2 * pp)))

    def algebra_step(i, carry):
        units = [(i * group + gi, pi, d) for gi in range(group) for pi in range(pp) for d in range(2)]
        streams = [algebra(units[s:s + STREAM_UNITS]) for s in range(0, len(units), STREAM_UNITS)]
        _interleave(streams, STREAM_LAG)
        return carry

    if nc == group:
        algebra_step(0, 0)
    else:
        lax.fori_loop(0, nc // group, algebra_step, 0)

    y_ref[0] = jnp.zeros(y_ref.shape[1:], F32)

    def state_step(i, carry):
        items = []
        for pi in range(pp):
            for d in range(2):
                c = i if d == 0 else nc - 1 - i
                sl, ln = rows(c), tiles[pi]
                lhs = jnp.concatenate([rb_s[d, sl, ln], gt_s[d, pi, c]], axis=0)
                items.append((pi, d, c, sl, ln, mm(lhs, st_s[d, pi].astype(BF16))))
        for pi, d, c, sl, ln, o in items:
            y_ref[0, sl, ln] += o[:C] + yb_s[d, sl, ln]
            st_s[d, pi] = o[C:] + ht_s[d, pi, c]
        return carry

    lax.fori_loop(0, nc, state_step, 0)

    if want_state:
        if alias_state:
            out = sfin_ref.at[0]
        else:
            out = sfin_ref.at[0, layer]
            for other in range(DEPTH):
                if other != layer:
                    sfin_ref[0, other] = jnp.zeros(sfin_ref.shape[2:], F32)
        for d in range(2):
            for pi in range(pp):
                stt = st_s[d, pi].T
                out[d, 2 * pi] = stt[:HEAD, :HEAD]
                out[d, 2 * pi + 1] = pltpu.roll(stt[HEAD:], HEAD, 1)[:, :HEAD]

    y = y_ref[0]
    yc = y - seg_sum(y) * (1.0 / HEAD)
    var = seg_sum(yc * yc) * (1.0 / HEAD)
    yn = yc * lax.rsqrt(var + GN_EPS) * lnw_ref[...] + lnb_ref[...]
    y_ref[0] = (yn + bonus) * g


def _rwkv(pa, shifted, layer, s0, s_prev, want_state, pp, mu, w0, w_up, a0, a_up, g_up, k_k, k_a, r_k, ln_w,
          ln_b):
    bsz, seq, _ = pa.shape
    has_state = s0 is not None
    alias_state = want_state and s_prev is not None
    width = pp * LANES
    nblk = D_RWKV // width
    nc = seq // CHUNK
    mu3 = mu.reshape(DEPTH, 1, SHIFT_END)
    row = lambda a: a.reshape(DEPTH, 1, D_RWKV)
    col = lambda o: (lambda b, p: (b, 0, o * nblk + p))
    vec = lambda b, p: (layer, 0, p)
    lora_blk = 3 * D_RWKV // LORA_W
    in_specs = [
        pl.BlockSpec((1, seq, width), col(0)),
        pl.BlockSpec((1, seq, width), col(1)),
        pl.BlockSpec((1, seq, width), col(2)),
        pl.BlockSpec((1, seq, LORA_W), lambda b, p: (b, 0, lora_blk)),
        pl.BlockSpec((None, 1, width), lambda b, p: (layer, 0, p)),
        pl.BlockSpec((None, 1, width), lambda b, p: (layer, 0, nblk + p)),
        pl.BlockSpec((None, 1, width), lambda b, p: (layer, 0, 2 * nblk + p)),
        pl.BlockSpec((None, 1, LORA_W), lambda b, p: (layer, 0, lora_blk)),
        pl.BlockSpec((None, 2, width), vec),
        pl.BlockSpec((None, 2, LORA, width), lambda b, p: (layer, 0, 0, p)),
        pl.BlockSpec((None, 2, width), vec),
        pl.BlockSpec((None, 2, LORA, width), lambda b, p: (layer, 0, 0, p)),
        pl.BlockSpec((None, GATE_LORA, width), vec),
        pl.BlockSpec((None, 1, width), vec),
        pl.BlockSpec((None, 1, width), vec),
        pl.BlockSpec((None, 1, width), vec),
        pl.BlockSpec((None, 1, width), vec),
        pl.BlockSpec((None, 1, width), vec),
    ]
    args = [pa, pa, pa, pa, mu3, mu3, mu3, mu3, w0, w_up, a0, a_up, g_up, row(k_k), row(k_a),
            row(r_k), row(ln_w), row(ln_b)]
    state_spec = pl.BlockSpec((1, None, 2, 2 * pp, HEAD, HEAD), lambda b, p: (b, layer, 0, p, 0, 0))
    if has_state:
        in_specs.append(state_spec)
        args.append(s0)
    aliases = {}
    if alias_state:
        aliases[len(args)] = 1
        in_specs.append(pl.BlockSpec(memory_space=pl.ANY))
        args.append(s_prev)
    out_shape = [jax.ShapeDtypeStruct((bsz, seq, D_RWKV), F32)]
    out_specs = [pl.BlockSpec((1, seq, width), lambda b, p: (b, 0, p))]
    if want_state:
        out_shape.append(jax.ShapeDtypeStruct((bsz, DEPTH, 2, N_HEADS, HEAD, HEAD), F32))
        if alias_state:
            out_specs.append(state_spec)
        else:
            out_specs.append(pl.BlockSpec((1, DEPTH, 2, 2 * pp, HEAD, HEAD), lambda b, p: (b, 0, 0, p, 0, 0)))
    outs = pl.pallas_call(
        functools.partial(_rwkv_kernel, seq, pp, shifted, has_state, want_state, alias_state, layer),
        out_shape=tuple(out_shape),
        grid=(bsz, nblk),
        in_specs=in_specs,
        out_specs=tuple(out_specs),
        input_output_aliases=aliases,
        scratch_shapes=[
            pltpu.VMEM((2, seq, width), F32),
            pltpu.VMEM((2, seq, width), F32),
            pltpu.VMEM((2, seq, width), F32),
            pltpu.VMEM((seq, width), F32),
            pltpu.VMEM((seq, width), F32),
            pltpu.VMEM((2, seq, width), F32),
            pltpu.VMEM((2, pp, LANES, LANES), F32),
            pltpu.VMEM((2, seq, width), BF16),
            pltpu.VMEM((2, seq, width), F32),
            pltpu.VMEM((2, pp, nc, LANES, LANES), BF16),
            pltpu.VMEM((2, pp, nc, LANES, LANES), F32),
        ],
        compiler_params=pltpu.CompilerParams(
            dimension_semantics=("arbitrary", "arbitrary"), vmem_limit_bytes=VMEM_LIMIT),
        name="rwkv",
    )(*args)
    return (outs[0], outs[1]) if want_state else (outs[0], None)


def _fpool_kernel(seq, nb, f_ref, p_ref, cs_ref, c64_ref, s64_ref, wf_ref, band_ref, inv_ref, wp_ref,
                  ps_ref, yb_ref, yc_ref, cw_s, cs_s):
    @pl.when(pl.program_id(0) == 0)
    def _():
        wf = wf_ref[...]
        cw_s[:, :D_FNET] = _dotf(c64_ref[...], wf).astype(BF16)
        cw_s[:, D_FNET:] = (-_dotf(s64_ref[...], wf)).astype(BF16)
        cs_s[...] = cs_ref[...].astype(BF16)

    f = f_ref[...].reshape(nb * seq, D_FNET)
    g = jnp.dot(f.astype(BF16), cw_s[...], preferred_element_type=F32).astype(BF16)
    stacked = jnp.concatenate(
        [jnp.concatenate([g[i * seq:(i + 1) * seq, :D_FNET], g[i * seq:(i + 1) * seq, D_FNET:]], axis=0)
         for i in range(nb)], axis=1)
    yb = jnp.dot(cs_s[...], stacked, preferred_element_type=F32)
    for i in range(nb):
        yb_ref[i] = yb[:, i * D_FNET:(i + 1) * D_FNET]

    p = jnp.concatenate([p_ref[i] for i in range(nb)], axis=1)
    p_b = p.astype(BF16)
    group = (lax.broadcasted_iota(jnp.int32, p.shape, 1) % D_POOL) // POOL_GW
    d = jnp.zeros(p.shape, F32)
    for i in range(len(POOL_WINDOWS)):
        s = jnp.dot(band_ref[i], p_b, preferred_element_type=F32)
        d = jnp.where(group == i, s, d)
    d = d * jnp.concatenate([inv_ref[...]] * nb, axis=1) - p
    d = jnp.concatenate([d[:, i * D_POOL:(i + 1) * D_POOL] for i in range(nb)], axis=0)
    yc_ref[...] = (_mm(d, wp_ref[...]) * ps_ref[...]).reshape(nb, seq, D_POOL)


def _fpool(pf, pp, consts, wf_bd, wp_bd, pool_scale):
    bsz, seq, _ = pf.shape
    cs, c64, s64, band, inv = consts
    nb = max(1, min(bsz, 1024 // seq))
    full2 = lambda b: (0, 0)
    return pl.pallas_call(
        functools.partial(_fpool_kernel, seq, nb),
        out_shape=(jax.ShapeDtypeStruct((bsz, seq, D_FNET), F32),
                   jax.ShapeDtypeStruct((bsz, seq, D_POOL), F32)),
        grid=(bsz // nb,),
        in_specs=[
            pl.BlockSpec((nb, seq, D_FNET), lambda b: (b, 0, 0)),
            pl.BlockSpec((nb, seq, D_POOL), lambda b: (b, 0, 0)),
            pl.BlockSpec((seq, 2 * seq), full2),
            pl.BlockSpec((D_FNET, D_FNET), full2),
            pl.BlockSpec((D_FNET, D_FNET), full2),
            pl.BlockSpec((D_FNET, D_FNET), full2),
            pl.BlockSpec((len(POOL_WINDOWS), seq, seq), lambda b: (0, 0, 0)),
            pl.BlockSpec((seq, D_POOL), full2),
            pl.BlockSpec((D_POOL, D_POOL), full2),
            pl.BlockSpec((1, D_POOL), full2),
        ],
        out_specs=(pl.BlockSpec((nb, seq, D_FNET), lambda b: (b, 0, 0)),
                   pl.BlockSpec((nb, seq, D_POOL), lambda b: (b, 0, 0))),
        scratch_shapes=[pltpu.VMEM((D_FNET, 2 * D_FNET), BF16),
                        pltpu.VMEM((seq, 2 * seq), BF16)],
        compiler_params=pltpu.CompilerParams(
            dimension_semantics=("arbitrary",), vmem_limit_bytes=VMEM_LIMIT),
        name="fpool",
    )(pf, pp, cs, c64, s64, wf_bd, band, inv, wp_bd, pool_scale.reshape(1, D_POOL))


def _window_matrix(n, w):
    t = np.arange(n)
    lo = np.clip(t - w // 2, 0, n)
    hi = np.clip(t + w - w // 2, 0, n)
    s = np.arange(n)
    return ((s[None, :] >= lo[:, None]) & (s[None, :] < hi[:, None])).astype(np.float64)


def _mixer_constants(seq, grid):
    n = np.arange(seq)
    ang = 2.0 * np.pi * ((n[:, None] * n[None, :]) % seq) / seq
    scale = 1.0 / math.sqrt(seq * FNET_GW)
    cs = np.concatenate([np.cos(ang), np.sin(ang)], axis=1) * scale
    m = np.arange(FNET_GW)
    ang64 = 2.0 * np.pi * ((m[:, None] * m[None, :]) % FNET_GW) / FNET_GW
    groups = D_FNET // FNET_GW
    c64 = np.kron(np.eye(groups), np.cos(ang64))
    s64 = np.kron(np.eye(groups), np.sin(ang64))
    bands = []
    invs = []
    for w in POOL_WINDOWS:
        if grid:
            rows = seq // GRID_W
            band = np.kron(_window_matrix(rows, w), _window_matrix(GRID_W, w))
        else:
            band = _window_matrix(seq, w)
        bands.append(band)
        invs.append(np.repeat((1.0 / band.sum(axis=1))[:, None], POOL_GW, axis=1))
    band = jnp.asarray(np.stack(bands), dtype=BF16)
    inv = jnp.asarray(np.concatenate(invs, axis=1), dtype=F32)
    return (jnp.asarray(cs, dtype=F32), jnp.asarray(c64, dtype=F32), jnp.asarray(s64, dtype=F32),
            band, inv)


def _block_diag(w):
    g, n, _ = w.shape
    return (w[:, :, None, :] * jnp.eye(g, dtype=w.dtype)[:, None, :, None]).reshape(g * n, g * n)


def _post_kernel(final, x_ref, ya_ref, yb_ref, yc_ref, mod_ref, wo_ref, g2_ref, w1_ref, w2_ref,
                 fg_ref, o_ref):
    mod = mod_ref[0]
    gate1 = mod[:, 2 * D_MODEL:3 * D_MODEL]
    shift2 = mod[:, 3 * D_MODEL:4 * D_MODEL]
    scale2 = mod[:, 4 * D_MODEL:5 * D_MODEL]
    gate2 = mod[:, 5 * D_MODEL:]
    mix = (jnp.dot(ya_ref[...].astype(BF16), wo_ref[0:D_RWKV, :], preferred_element_type=F32)
           + jnp.dot(yb_ref[...].astype(BF16), wo_ref[D_RWKV:D_RWKV + D_FNET, :],
                     preferred_element_type=F32)
           + jnp.dot(yc_ref[...].astype(BF16), wo_ref[D_RWKV + D_FNET:, :],
                     preferred_element_type=F32))
    x1 = x_ref[...] + gate1 * mix
    h2 = (_rms(x1, g2_ref[...]) * (1.0 + scale2) + shift2).astype(BF16)
    tf = 1024
    acc = jnp.zeros_like(x1)
    for j in range(D_FF // tf):
        ff = jnp.dot(h2, w1_ref[:, j * tf:(j + 1) * tf], preferred_element_type=F32)
        ff = jnp.square(jnp.maximum(ff, 0.0))
        acc = acc + jnp.dot(ff.astype(BF16), w2_ref[j * tf:(j + 1) * tf, :],
                            preferred_element_type=F32)
    x2 = x1 + gate2 * acc
    if final:
        x2 = _rms(x2, fg_ref[...])
    o_ref[...] = x2


def _post(x, ya, yb, yc, mod, mod_row, layer, w_out_b, g2, w1_b, w2_b, fg, final, tm):
    t = x.shape[0]
    tok = lambda i: (i, 0)
    full = lambda i: (0, 0)
    return pl.pallas_call(
        functools.partial(_post_kernel, final),
        out_shape=jax.ShapeDtypeStruct((t, D_MODEL), F32),
        grid=(t // tm,),
        in_specs=[
            pl.BlockSpec((tm, D_MODEL), tok),
            pl.BlockSpec((tm, D_RWKV), tok),
            pl.BlockSpec((tm, D_FNET), tok),
            pl.BlockSpec((tm, D_POOL), tok),
            pl.BlockSpec((None, 1, 1, 6 * D_MODEL), lambda i: (layer, mod_row(i), 0, 0)),
            _resident((D_MODEL, D_MODEL), layer),
            _resident((1, D_MODEL), layer),
            _resident((D_MODEL, D_FF), layer),
            _resident((D_FF, D_MODEL), layer),
            pl.BlockSpec((1, D_MODEL), full),
        ],
        out_specs=pl.BlockSpec((tm, D_MODEL), tok),
        compiler_params=pltpu.CompilerParams(
            dimension_semantics=("arbitrary",), vmem_limit_bytes=VMEM_LIMIT),
        name="post",
    )(x, ya, yb, yc, mod, w_out_b, g2.reshape(DEPTH, 1, D_MODEL), w1_b, w2_b, fg.reshape(1, D_MODEL))


def _layer(x, bsz, seq, layer, mod, mod_row, s0, s_prev, want_state, pairs, consts, sw, mixw, fg, tm):
    (norm1_g, w_in_b, mu_shift, w0, w_up, a0, a_up, g_up, k_k, k_a, r_k, ln_w, ln_b, w_out_b, norm2_g,
     w1_b, w2_b) = sw
    wf_bd, wp_bd, pool_scale = mixw
    (pa, pf, pp), shifted = _pre(x, mod, mod_row, layer, norm1_g, w_in_b, mu_shift, seq, tm)
    ya, s_fin = _rwkv(pa.reshape(bsz, seq, SHIFT_END), shifted, layer, s0, s_prev, want_state, pairs,
                      mu_shift, w0, w_up, a0, a_up, g_up, k_k, k_a, r_k, ln_w, ln_b)
    yb, yc = _fpool(pf.reshape(bsz, seq, D_FNET), pp.reshape(bsz, seq, D_POOL), consts, wf_bd,
                    wp_bd, pool_scale)
    x = _post(x, ya.reshape(bsz * seq, D_RWKV), yb.reshape(bsz * seq, D_FNET),
              yc.reshape(bsz * seq, D_POOL), mod, mod_row, layer, w_out_b, norm2_g, w1_b, w2_b, fg,
              layer == DEPTH - 1, tm)
    return x, s_fin


def kernel(x_prompt, x_sample, state_wkv, c, c_ctx, w_ada, b_ada, norm1_g, w_in, mu_shift, w0, w_up, a0, a_up, g_up, k_k, k_a, r_k, ln_x_w, ln_x_b, w_fnet, w_pool, pool_scale, w_out, norm2_g, w_ff1, w_ff2, final_norm_g):
    bp, lp, _ = x_prompt.shape
    bs, ls, _ = x_sample.shape
    tm = 512
    n_rows = 8
    cond = jnp.concatenate([c_ctx[None], c, jnp.zeros((n_rows - 1 - bs, D_MODEL), F32)], axis=0)
    mod_all = _ada(cond, w_ada, b_ada)
    consts_p = _mixer_constants(lp, False)
    consts_s = _mixer_constants(ls, True)
    xp = x_prompt.reshape(bp * lp, D_MODEL)
    xs = x_sample.reshape(bs * ls, D_MODEL)
    tiles_per_seq = ls // tm
    row_p = lambda i: 0
    row_s = lambda i: 1 + i // tiles_per_seq
    sw = (norm1_g, w_in.astype(BF16), mu_shift, w0, w_up, a0, a_up, g_up, k_k, k_a, r_k, ln_x_w, ln_x_b,
          w_out.astype(BF16), norm2_g, w_ff1.astype(BF16), w_ff2.astype(BF16))
    mod = mod_all.reshape(DEPTH, n_rows, 1, 6 * D_MODEL)
    new_state_wkv = None
    for l in range(DEPTH):
        mixw = (_block_diag(w_fnet[l]), _block_diag(w_pool[l]), pool_scale[l])
        xp, new_state_wkv = _layer(xp, bp, lp, l, mod, row_p, None, new_state_wkv, True, 4, consts_p, sw,
                                   mixw, final_norm_g, tm)
        xs, _ = _layer(xs, bs, ls, l, mod, row_s, state_wkv, None, False, 2, consts_s, sw, mixw,
                       final_norm_g, tm)
    return (xp.reshape(bp, lp, D_MODEL), xs.reshape(bs, ls, D_MODEL), new_state_wkv)
```

```python
import functools
import math

import numpy as np
import jax
import jax.numpy as jnp
from jax import lax
from jax.experimental import pallas as pl
from jax.experimental.pallas import tpu as pltpu

F32 = jnp.float32
BF16 = jnp.bfloat16

D_MODEL = 1024
DEPTH = 2
GRID_W = 64
D_RWKV = 512
HEAD = 64
N_HEADS = 8
LANES = 128
N_PAIRS = D_RWKV // LANES
D_FNET = 256
FNET_GW = 64
D_POOL = 256
POOL_GW = 64
POOL_WINDOWS = (2, 4, 8, 16)
LORA = 64
GATE_LORA = 128
SHIFT_END = 3 * D_RWKV + 2 * LORA + 2 * LORA + GATE_LORA
LORA_W = SHIFT_END - 3 * D_RWKV
D_IN = SHIFT_END + D_FNET + D_POOL
D_FF = 4 * D_MODEL
RMS_EPS = 1e-6
GN_EPS = 64e-5
LOG2E = 1.4426950408889634
CHUNK = 64
VMEM_LIMIT = 56 * 1024 * 1024
STREAM_UNITS = 16
STREAM_LAG = 8


def _dotf(a, b):
    return jnp.dot(a, b, preferred_element_type=F32, precision=lax.Precision.HIGHEST)


def _split(x, n):
    terms = []
    for _ in range(n - 1):
        hi = x.astype(BF16)
        terms.append(hi)
        x = x - hi.astype(F32)
    terms.append(x.astype(BF16))
    return terms


def _dots(a, b, na, nb, nt=False):
    dims = (((1,), (1,)), ((), ())) if nt else (((1,), (0,)), ((), ()))
    ta = _split(a, na)
    tb = _split(b, nb)
    out = None
    for i in range(na):
        for j in range(nb):
            if i + j < max(na, nb):
                t = lax.dot_general(ta[i], tb[j], dims, preferred_element_type=F32)
                out = t if out is None else out + t
    return out


def _mm(a, b):
    return jnp.dot(a.astype(BF16), b.astype(BF16), preferred_element_type=F32)


def _sigmoid(x):
    return 0.5 + 0.5 * jnp.tanh(0.5 * x)


def _rms(x, g):
    ms = jnp.mean(x * x, axis=-1, keepdims=True)
    return x * lax.rsqrt(ms + RMS_EPS) * g


_DONE = object()


def _interleave(streams):
    pending = sorted(streams, key=lambda s: s[0])
    active = []
    step = 0
    while pending or active:
        while pending and pending[0][0] <= step:
            active.append(pending.pop(0)[1])
        for s in list(active):
            if next(s, _DONE) is _DONE:
                active.remove(s)
        step += 1


def _ada_kernel(c_ref, w_ref, b_ref, o_ref):
    c = c_ref[...]
    s = c * _sigmoid(c)
    o_ref[0] = _dotf(s, w_ref[0]) + b_ref[0]


def _ada(cond, w_ada, b_ada):
    tn = 1536
    nrow = cond.shape[0]
    return pl.pallas_call(
        _ada_kernel,
        out_shape=jax.ShapeDtypeStruct((DEPTH, nrow, 6 * D_MODEL), F32),
        grid=(DEPTH, 6 * D_MODEL // tn),
        in_specs=[
            pl.BlockSpec((nrow, D_MODEL), lambda l, j: (0, 0)),
            pl.BlockSpec((1, D_MODEL, tn), lambda l, j: (l, 0, j)),
            pl.BlockSpec((1, 1, tn), lambda l, j: (l, 0, j)),
        ],
        out_specs=pl.BlockSpec((1, nrow, tn), lambda l, j: (l, 0, j)),
        compiler_params=pltpu.CompilerParams(
            dimension_semantics=("arbitrary", "arbitrary"), vmem_limit_bytes=VMEM_LIMIT),
        name="ada",
    )(cond, w_ada, b_ada.reshape(DEPTH, 1, 6 * D_MODEL))


def _shift_mix(p, mu, period):
    n = p.shape[0]
    pos = lax.broadcasted_iota(jnp.int32, p.shape, 0) % period
    prev = jnp.where(pos == 0, 0.0, pltpu.roll(p, 1, 0))
    nxt = jnp.where(pos == period - 1, 0.0, pltpu.roll(p, n - 1, 0))
    return p * (1.0 - mu) + (0.5 * mu) * (prev + nxt)


def _pre_kernel(shift_period, x_ref, mod_ref, g_ref, w_ref, mu_ref, oa_ref, of_ref, op_ref):
    mod = mod_ref[0]
    shift = mod[:, 0:D_MODEL]
    scale = mod[:, D_MODEL:2 * D_MODEL]
    h = _rms(x_ref[...], g_ref[...]) * (1.0 + scale) + shift
    p = jnp.dot(h.astype(BF16), w_ref[...], preferred_element_type=F32)
    pa = p[:, :SHIFT_END]
    if shift_period:
        pa = _shift_mix(pa, mu_ref[...], shift_period)
    oa_ref[...] = pa
    of_ref[...] = p[:, SHIFT_END:SHIFT_END + D_FNET]
    op_ref[...] = p[:, SHIFT_END + D_FNET:]


def _resident(shape, layer):
    zeros = (0,) * len(shape)
    return pl.BlockSpec((None,) + shape, lambda *_: (layer,) + zeros, pipeline_mode=pl.Buffered(1))


def _pre(x, mod, mod_row, layer, g, w_in_b, mu, seq, tm):
    t = x.shape[0]
    shifted = tm % seq == 0
    outs = pl.pallas_call(
        functools.partial(_pre_kernel, seq if shifted else 0),
        out_shape=(jax.ShapeDtypeStruct((t, SHIFT_END), F32),
                   jax.ShapeDtypeStruct((t, D_FNET), F32),
                   jax.ShapeDtypeStruct((t, D_POOL), F32)),
        grid=(t // tm,),
        in_specs=[
            pl.BlockSpec((tm, D_MODEL), lambda i: (i, 0)),
            pl.BlockSpec((None, 1, 1, 6 * D_MODEL), lambda i: (layer, mod_row(i), 0, 0)),
            _resident((1, D_MODEL), layer),
            _resident((D_MODEL, D_IN), layer),
            _resident((1, SHIFT_END), layer),
        ],
        out_specs=(pl.BlockSpec((tm, SHIFT_END), lambda i: (i, 0)),
                   pl.BlockSpec((tm, D_FNET), lambda i: (i, 0)),
                   pl.BlockSpec((tm, D_POOL), lambda i: (i, 0))),
        compiler_params=pltpu.CompilerParams(
            dimension_semantics=("arbitrary",), vmem_limit_bytes=VMEM_LIMIT),
        name="pre",
    )(x, mod, g.reshape(DEPTH, 1, D_MODEL), w_in_b, mu.reshape(DEPTH, 1, SHIFT_END))
    return outs, shifted


def _rwkv_kernel(seq, pp, shifted, has_state, want_state, alias_state, layer, *refs):
    (r_ref, k_ref, v_ref, lo_ref, mur_ref, muk_ref, muv_ref, mul_ref, w0_ref, wup_ref, a0_ref,
     aup_ref, gup_ref, kk_ref, ka_ref, rk_ref, lnw_ref, lnb_ref) = refs[:18]
    n_in = 18 + has_state + alias_state
    s0_ref = refs[18] if has_state else None
    y_ref = refs[n_in]
    sfin_ref = refs[n_in + 1] if want_state else None
    lw_s, kd_s, b_s, r_s, v_s, kn_s, st_s, rb_s, yb_s, gt_s, ht_s = refs[n_in + 1 + want_state:]
    nc = seq // CHUNK
    C = CHUNK
    tiles = [slice(i * LANES, (i + 1) * LANES) for i in range(pp)]

    lane1 = lax.broadcasted_iota(jnp.int32, (1, LANES), 1)
    head0 = lane1 < HEAD
    head1 = jnp.logical_not(head0)
    ri = lax.broadcasted_iota(jnp.int32, (LANES, LANES), 0)
    ci = lax.broadcasted_iota(jnp.int32, (LANES, LANES), 1)
    same_head = (ri // HEAD) == (ci // HEAD)
    seg_ones = same_head.astype(F32)
    eye = (ri == ci).astype(F32)

    def seg_sum(x):
        return jnp.concatenate([_mm(x[:, t], seg_ones) for t in tiles], axis=1)

    if shifted:
        r, k, v, lo = r_ref[0], k_ref[0], v_ref[0], lo_ref[0]
    else:
        r = _shift_mix(r_ref[0], mur_ref[...], seq)
        k = _shift_mix(k_ref[0], muk_ref[...], seq)
        v = _shift_mix(v_ref[0], muv_ref[...], seq)
        lo = _shift_mix(lo_ref[0], mul_ref[...], seq)
    wd = jnp.tanh(lo[:, 0:2 * LORA])
    ad = lo[:, 2 * LORA:4 * LORA]
    gd = _sigmoid(lo[:, 4 * LORA:])
    gate = _mm(gd, gup_ref[...])
    kx = k * kk_ref[...]
    kn = kx * lax.rsqrt(jnp.maximum(seg_sum(kx * kx), 1e-24))
    zeros_up = jnp.zeros((LORA, pp * LANES), F32)
    kd_sum = jnp.zeros_like(k)
    for d in range(2):
        if d == 0:
            wup = jnp.concatenate([wup_ref[0], zeros_up], axis=0)
            aup = jnp.concatenate([aup_ref[0], zeros_up], axis=0)
        else:
            wup = jnp.concatenate([zeros_up, wup_ref[1]], axis=0)
            aup = jnp.concatenate([zeros_up, aup_ref[1]], axis=0)
        zw = w0_ref[d:d + 1, :] + _mm(wd, wup)
        lw2 = (-LOG2E * math.exp(-0.5)) * _sigmoid(zw)
        lw_s[d] = lw2
        a_sig = _sigmoid(a0_ref[d:d + 1, :] + _mm(ad, aup))
        kd = k * (1.0 + (a_sig - 1.0) * ka_ref[...])
        kd_s[d] = kd
        b_s[d] = kn * a_sig
        kn_s[d] = -kn * jnp.exp2(-lw2)
        kd_sum = kd_sum + kd
    r_s[...] = r
    v_s[...] = v
    bonus = seg_sum(r * kd_sum * rk_ref[...]) * v

    if has_state:
        rj = lax.broadcasted_iota(jnp.int32, (LANES, HEAD), 0)
        cj = lax.broadcasted_iota(jnp.int32, (LANES, HEAD), 1)
        place = [(rj == cj + j * HEAD).astype(F32) for j in range(2)]
        keys = [(d, pi, j) for d in range(2) for pi in range(pp) for j in range(2)]
        t1 = {key: _dots(place[key[2]], s0_ref[0, key[0], 2 * key[1] + key[2]], 1, 3, nt=True)
              for key in keys}
        t2 = {key: _dots(t1[key], place[key[2]], 3, 1, nt=True) for key in keys}
        for d in range(2):
            for pi in range(pp):
                st_s[d, pi] = t2[(d, pi, 0)] + t2[(d, pi, 1)]
    else:
        st_s[...] = jnp.zeros(st_s.shape, F32)

    tr = lax.broadcasted_iota(jnp.int32, (C, C), 0)
    tc = lax.broadcasted_iota(jnp.int32, (C, C), 1)
    t_p = lax.broadcasted_iota(jnp.int32, (C, LANES), 0)
    s_p = lax.broadcasted_iota(jnp.int32, (C, LANES), 1) % C
    tris = ((tc <= tr).astype(F32), (tc >= tr).astype(F32))
    strict = (s_p < t_p, s_p > t_p)
    inclusive = (s_p <= t_p, s_p >= t_p)
    eye_p = (s_p == t_p).astype(F32)
    txs = t_p ^ s_p
    lvl_map = sum(((txs >= (1 << bit)).astype(jnp.int32) for bit in range(1, 6)),
                  jnp.where(txs == 0, -1, 0))
    zero_b = jnp.zeros((C, LANES), BF16)

    def bd(x):
        return jnp.concatenate([jnp.where(head0, x, zero_b), jnp.where(head1, x, zero_b)], axis=0)

    def bdx(x):
        return jnp.concatenate([jnp.where(head1, x, zero_b), jnp.where(head0, x, zero_b)], axis=0)

    def mm(a, b):
        return jnp.dot(a, b, preferred_element_type=F32)

    def rows(c):
        return pl.ds(c * C, C) if isinstance(c, int) else pl.ds(pl.multiple_of(c * C, C), C)

    def algebra(units):
        u = []
        for c, pi, d in units:
            sl, ln = rows(c), tiles[pi]
            lw = lw_s[d, sl, ln]
            u.append(dict(c=c, pi=pi, d=d, sl=sl, ln=ln, lw=lw, cl=_dots(tris[d], lw, 1, 2)))
        yield
        for q in u:
            d, sl, ln, cl = q["d"], q["sl"], q["ln"], q["cl"]
            tot = cl[C - 1:C, :] if d == 0 else cl[0:1, :]
            e_in = jnp.exp2(cl)
            e_inv = jnp.exp2(-cl)
            p_c = jnp.exp2(tot)
            q["at"] = (kn_s[d, sl, ln] * e_in).astype(BF16)
            q["rt"] = r_s[sl, ln] * e_in
            bt = b_s[d, sl, ln] * e_inv
            kt = kd_s[d, sl, ln] * e_inv
            q["lhs_t"] = jnp.concatenate([bt * p_c, kt * p_c], axis=0).T.astype(BF16)
            bt = bt.astype(BF16)
            kt = kt.astype(BF16)
            q["rhs_nt"] = jnp.concatenate([jnp.where(head0, bt, zero_b), jnp.where(head1, bt, zero_b),
                                           jnp.where(head1, kt, zero_b), jnp.where(head0, kt, zero_b)], axis=0)
            q["p_c"] = p_c
            q["vc"] = v_s[sl, ln].astype(BF16)
            yield
        for q in u:
            lhs = jnp.concatenate([q["at"], q["rt"].astype(BF16)], axis=0)
            q["sc"] = lax.dot_general(lhs, q["rhs_nt"], (((1,), (1,)), ((), ())), preferred_element_type=F32)
        yield
        for q in u:
            d, sc = q["d"], q["sc"]
            lab = jnp.where(strict[d], sc[:C, :LANES], 0.0)
            lakx = jnp.where(strict[d], sc[:C, LANES:], 0.0)
            q["mr"] = jnp.concatenate([jnp.where(inclusive[d], sc[C:, :LANES], 0.0),
                                       jnp.where(inclusive[d], sc[C:, LANES:], 0.0)], axis=1).astype(BF16)
            q["xv"] = mm(lakx.astype(BF16), bdx(q["vc"]))
            q["t"] = eye_p + jnp.where(lvl_map == 0, lab, 0.0)
            q["lab"] = lab
        yield
        for lvl in range(1, 6):
            for q in u:
                q["tb"] = q["t"].astype(BF16)
                q["w"] = mm(jnp.where(lvl_map == lvl, q["lab"], 0.0).astype(BF16), bd(q["tb"]))
            yield
            for q in u:
                q["t"] = q["t"] + mm(q["tb"], bd(q["w"].astype(BF16)))
            yield
        for q in u:
            x = mm(q["t"].astype(BF16),
                   jnp.concatenate([bd(q["at"]), bd(q["xv"].astype(BF16))], axis=1))
            q["ab"] = x[:, :LANES].astype(BF16)
            q["ub"] = x[:, LANES:].astype(BF16)
        yield
        for q in u:
            rhs = jnp.concatenate([jnp.concatenate([bd(q["ab"]), bd(q["ub"])], axis=1),
                                   jnp.concatenate([jnp.concatenate([zero_b, zero_b], axis=0), bdx(q["vc"])],
                                                   axis=1)], axis=0)
            q["o2"] = mm(q["mr"], rhs)
            rhs2 = jnp.concatenate([jnp.concatenate([q["ab"], q["ub"]], axis=1),
                                    jnp.concatenate([zero_b, q["vc"]], axis=1)], axis=0)
            q["gh"] = mm(q["lhs_t"], rhs2)
        yield
        for q in u:
            c, pi, d, sl, ln, o2, gh = q["c"], q["pi"], q["d"], q["sl"], q["ln"], q["o2"], q["gh"]
            rb_s[d, sl, ln] = (q["rt"] + o2[:, :LANES]).astype(BF16)
            yb_s[d, sl, ln] = o2[:, LANES:]
            gt_s[d, pi, c] = (jnp.where(same_head, gh[:, :LANES], 0.0) + eye * q["p_c"]).astype(BF16)
            ht_s[d, pi, c] = jnp.where(same_head, gh[:, LANES:], 0.0)
        yield

    def state_chain(g, first, count):
        while not (("algebra", g) in issued and (g == 0 or ("chain", g - 1) in issued)):
            yield
        for i in range(first, first + count):
            items = []
            for pi in range(pp):
                for d in range(2):
                    c = i if d == 0 else nc - 1 - i
                    sl, ln = rows(c), tiles[pi]
                    lhs = jnp.concatenate([rb_s[d, sl, ln], gt_s[d, pi, c]], axis=0)
                    items.append((pi, d, c, sl, ln, mm(lhs, st_s[d, pi].astype(BF16))))
            yield
            for pi, d, c, sl, ln, o in items:
                y_ref[0, sl, ln] += o[:C] + yb_s[d, sl, ln]
                st_s[d, pi] = o[C:] + ht_s[d, pi, c]
            yield
        issued.add(("chain", g))

    y_ref[0] = jnp.zeros(y_ref.shape[1:], F32)

    per = max(1, STREAM_UNITS // (2 * pp))
    issued = set()

    def tracked(g, units):
        yield from algebra(units)
        issued.add(("algebra", g))

    streams = []
    for grp in range(nc // per):
        units = [(i if d == 0 else nc - 1 - i, pi, d)
                 for i in range(grp * per, (grp + 1) * per) for pi in range(pp) for d in range(2)]
        streams.append((grp * STREAM_LAG, tracked(grp, units)))
        streams.append((grp * STREAM_LAG, state_chain(grp, grp * per, per)))
    _interleave(streams)

    if want_state:
        if alias_state:
            out = sfin_ref.at[0]
        else:
            out = sfin_ref.at[0, layer]
            for other in range(DEPTH):
                if other != layer:
                    sfin_ref[0, other] = jnp.zeros(sfin_ref.shape[2:], F32)
        for d in range(2):
            for pi in range(pp):
                stt = st_s[d, pi].T
                out[d, 2 * pi] = stt[:HEAD, :HEAD]
                out[d, 2 * pi + 1] = pltpu.roll(stt[HEAD:], HEAD, 1)[:, :HEAD]

    y = y_ref[0]
    yc = y - seg_sum(y) * (1.0 / HEAD)
    var = seg_sum(yc * yc) * (1.0 / HEAD)
    yn = yc * lax.rsqrt(var + GN_EPS) * lnw_ref[...] + lnb_ref[...]
    y_ref[0] = (yn + bonus) * gate


def _rwkv(pa, shifted, layer, s0, s_prev, want_state, pp, mu, w0, w_up, a0, a_up, g_up, k_k, k_a, r_k, ln_w,
          ln_b):
    bsz, seq, _ = pa.shape
    has_state = s0 is not None
    alias_state = want_state and s_prev is not None
    width = pp * LANES
    nblk = D_RWKV // width
    nc = seq // CHUNK
    mu3 = mu.reshape(DEPTH, 1, SHIFT_END)
    row = lambda a: a.reshape(DEPTH, 1, D_RWKV)
    col = lambda o: (lambda b, p: (b, 0, o * nblk + p))
    vec = lambda b, p: (layer, 0, p)
    lora_blk = 3 * D_RWKV // LORA_W
    in_specs = [
        pl.BlockSpec((1, seq, width), col(0)),
        pl.BlockSpec((1, seq, width), col(1)),
        pl.BlockSpec((1, seq, width), col(2)),
        pl.BlockSpec((1, seq, LORA_W), lambda b, p: (b, 0, lora_blk)),
        pl.BlockSpec((None, 1, width), lambda b, p: (layer, 0, p)),
        pl.BlockSpec((None, 1, width), lambda b, p: (layer, 0, nblk + p)),
        pl.BlockSpec((None, 1, width), lambda b, p: (layer, 0, 2 * nblk + p)),
        pl.BlockSpec((None, 1, LORA_W), lambda b, p: (layer, 0, lora_blk)),
        pl.BlockSpec((None, 2, width), vec),
        pl.BlockSpec((None, 2, LORA, width), lambda b, p: (layer, 0, 0, p)),
        pl.BlockSpec((None, 2, width), vec),
        pl.BlockSpec((None, 2, LORA, width), lambda b, p: (layer, 0, 0, p)),
        pl.BlockSpec((None, GATE_LORA, width), vec),
        pl.BlockSpec((None, 1, width), vec),
        pl.BlockSpec((None, 1, width), vec),
        pl.BlockSpec((None, 1, width), vec),
        pl.BlockSpec((None, 1, width), vec),
        pl.BlockSpec((None, 1, width), vec),
    ]
    args = [pa, pa, pa, pa, mu3, mu3, mu3, mu3, w0, w_up, a0, a_up, g_up, row(k_k), row(k_a),
            row(r_k), row(ln_w), row(ln_b)]
    state_spec = pl.BlockSpec((1, None, 2, 2 * pp, HEAD, HEAD), lambda b, p: (b, layer, 0, p, 0, 0))
    if has_state:
        in_specs.append(state_spec)
        args.append(s0)
    aliases = {}
    if alias_state:
        aliases[len(args)] = 1
        in_specs.append(pl.BlockSpec(memory_space=pl.ANY))
        args.append(s_prev)
    out_shape = [jax.ShapeDtypeStruct((bsz, seq, D_RWKV), F32)]
    out_specs = [pl.BlockSpec((1, seq, width), lambda b, p: (b, 0, p))]
    if want_state:
        out_shape.append(jax.ShapeDtypeStruct((bsz, DEPTH, 2, N_HEADS, HEAD, HEAD), F32))
        if alias_state:
            out_specs.append(state_spec)
        else:
            out_specs.append(pl.BlockSpec((1, DEPTH, 2, 2 * pp, HEAD, HEAD), lambda b, p: (b, 0, 0, p, 0, 0)))
    outs = pl.pallas_call(
        functools.partial(_rwkv_kernel, seq, pp, shifted, has_state, want_state, alias_state, layer),
        out_shape=tuple(out_shape),
        grid=(bsz, nblk),
        in_specs=in_specs,
        out_specs=tuple(out_specs),
        input_output_aliases=aliases,
        scratch_shapes=[
            pltpu.VMEM((2, seq, width), F32),
            pltpu.VMEM((2, seq, width), F32),
            pltpu.VMEM((2, seq, width), F32),
            pltpu.VMEM((seq, width), F32),
            pltpu.VMEM((seq, width), F32),
            pltpu.VMEM((2, seq, width), F32),
            pltpu.VMEM((2, pp, LANES, LANES), F32),
            pltpu.VMEM((2, seq, width), BF16),
            pltpu.VMEM((2, seq, width), F32),
            pltpu.VMEM((2, pp, nc, LANES, LANES), BF16),
            pltpu.VMEM((2, pp, nc, LANES, LANES), F32),
        ],
        compiler_params=pltpu.CompilerParams(
            dimension_semantics=("arbitrary", "arbitrary"), vmem_limit_bytes=VMEM_LIMIT),
        name="rwkv",
    )(*args)
    return (outs[0], outs[1]) if want_state else (outs[0], None)


def _fpool_kernel(seq, nb, f_ref, p_ref, cs_ref, c64_ref, s64_ref, wf_ref, band_ref, inv_ref, wp_ref,
                  ps_ref, yb_ref, yc_ref, cw_s, cs_s):
    @pl.when(pl.program_id(0) == 0)
    def _():
        wf = wf_ref[...]
        cw_s[:, :D_FNET] = _dotf(c64_ref[...], wf).astype(BF16)
        cw_s[:, D_FNET:] = (-_dotf(s64_ref[...], wf)).astype(BF16)
        cs_s[...] = cs_ref[...].astype(BF16)

    f = f_ref[...].reshape(nb * seq, D_FNET)
    g = jnp.dot(f.astype(BF16), cw_s[...], preferred_element_type=F32).astype(BF16)
    stacked = jnp.concatenate(
        [jnp.concatenate([g[i * seq:(i + 1) * seq, :D_FNET], g[i * seq:(i + 1) * seq, D_FNET:]], axis=0)
         for i in range(nb)], axis=1)
    yb = jnp.dot(cs_s[...], stacked, preferred_element_type=F32)
    for i in range(nb):
        yb_ref[i] = yb[:, i * D_FNET:(i + 1) * D_FNET]

    p = jnp.concatenate([p_ref[i] for i in range(nb)], axis=1)
    p_b = p.astype(BF16)
    group = (lax.broadcasted_iota(jnp.int32, p.shape, 1) % D_POOL) // POOL_GW
    d = jnp.zeros(p.shape, F32)
    for i in range(len(POOL_WINDOWS)):
        s = jnp.dot(band_ref[i], p_b, preferred_element_type=F32)
        d = jnp.where(group == i, s, d)
    d = d * jnp.concatenate([inv_ref[...]] * nb, axis=1) - p
    d = jnp.concatenate([d[:, i * D_POOL:(i + 1) * D_POOL] for i in range(nb)], axis=0)
    yc_ref[...] = (_mm(d, wp_ref[...]) * ps_ref[...]).reshape(nb, seq, D_POOL)


def _fpool(pf, pp, consts, wf_bd, wp_bd, pool_scale):
    bsz, seq, _ = pf.shape
    cs, c64, s64, band, inv = consts
    nb = max(1, min(bsz, 1024 // seq))
    full2 = lambda b: (0, 0)
    return pl.pallas_call(
        functools.partial(_fpool_kernel, seq, nb),
        out_shape=(jax.ShapeDtypeStruct((bsz, seq, D_FNET), F32),
                   jax.ShapeDtypeStruct((bsz, seq, D_POOL), F32)),
        grid=(bsz // nb,),
        in_specs=[
            pl.BlockSpec((nb, seq, D_FNET), lambda b: (b, 0, 0)),
            pl.BlockSpec((nb, seq, D_POOL), lambda b: (b, 0, 0)),
            pl.BlockSpec((seq, 2 * seq), full2),
            pl.BlockSpec((D_FNET, D_FNET), full2),
            pl.BlockSpec((D_FNET, D_FNET), full2),
            pl.BlockSpec((D_FNET, D_FNET), full2),
            pl.BlockSpec((len(POOL_WINDOWS), seq, seq), lambda b: (0, 0, 0)),
            pl.BlockSpec((seq, D_POOL), full2),
            pl.BlockSpec((D_POOL, D_POOL), full2),
            pl.BlockSpec((1, D_POOL), full2),
        ],
        out_specs=(pl.BlockSpec((nb, seq, D_FNET), lambda b: (b, 0, 0)),
                   pl.BlockSpec((nb, seq, D_POOL), lambda b: (b, 0, 0))),
        scratch_shapes=[pltpu.VMEM((D_FNET, 2 * D_FNET), BF16),
                        pltpu.VMEM((seq, 2 * seq), BF16)],
        compiler_params=pltpu.CompilerParams(
            dimension_semantics=("arbitrary",), vmem_limit_bytes=VMEM_LIMIT),
        name="fpool",
    )(pf, pp, cs, c64, s64, wf_bd, band, inv, wp_bd, pool_scale.reshape(1, D_POOL))


def _window_matrix(n, w):
    t = np.arange(n)
    lo = np.clip(t - w // 2, 0, n)
    hi = np.clip(t + w - w // 2, 0, n)
    s = np.arange(n)
    return ((s[None, :] >= lo[:, None]) & (s[None, :] < hi[:, None])).astype(np.float64)


def _mixer_constants(seq, grid):
    n = np.arange(seq)
    ang = 2.0 * np.pi * ((n[:, None] * n[None, :]) % seq) / seq
    scale = 1.0 / math.sqrt(seq * FNET_GW)
    cs = np.concatenate([np.cos(ang), np.sin(ang)], axis=1) * scale
    m = np.arange(FNET_GW)
    ang64 = 2.0 * np.pi * ((m[:, None] * m[None, :]) % FNET_GW) / FNET_GW
    groups = D_FNET // FNET_GW
    c64 = np.kron(np.eye(groups), np.cos(ang64))
    s64 = np.kron(np.eye(groups), np.sin(ang64))
    bands = []
    invs = []
    for w in POOL_WINDOWS:
        if grid:
            rows = seq // GRID_W
            band = np.kron(_window_matrix(rows, w), _window_matrix(GRID_W, w))
        else:
            band = _window_matrix(seq, w)
        bands.append(band)
        invs.append(np.repeat((1.0 / band.sum(axis=1))[:, None], POOL_GW, axis=1))
    band = jnp.asarray(np.stack(bands), dtype=BF16)
    inv = jnp.asarray(np.concatenate(invs, axis=1), dtype=F32)
    return (jnp.asarray(cs, dtype=F32), jnp.asarray(c64, dtype=F32), jnp.asarray(s64, dtype=F32),
            band, inv)


def _block_diag(w):
    g, n, _ = w.shape
    return (w[:, :, None, :] * jnp.eye(g, dtype=w.dtype)[:, None, :, None]).reshape(g * n, g * n)


def _post_kernel(final, x_ref, ya_ref, yb_ref, yc_ref, mod_ref, wo_ref, g2_ref, w1_ref, w2_ref,
                 fg_ref, o_ref):
    mod = mod_ref[0]
    gate1 = mod[:, 2 * D_MODEL:3 * D_MODEL]
    shift2 = mod[:, 3 * D_MODEL:4 * D_MODEL]
    scale2 = mod[:, 4 * D_MODEL:5 * D_MODEL]
    gate2 = mod[:, 5 * D_MODEL:]
    mix = (jnp.dot(ya_ref[...].astype(BF16), wo_ref[0:D_RWKV, :], preferred_element_type=F32)
           + jnp.dot(yb_ref[...].astype(BF16), wo_ref[D_RWKV:D_RWKV + D_FNET, :],
                     preferred_element_type=F32)
           + jnp.dot(yc_ref[...].astype(BF16), wo_ref[D_RWKV + D_FNET:, :],
                     preferred_element_type=F32))
    x1 = x_ref[...] + gate1 * mix
    h2 = (_rms(x1, g2_ref[...]) * (1.0 + scale2) + shift2).astype(BF16)
    tf = 1024
    acc = jnp.zeros_like(x1)
    for j in range(D_FF // tf):
        ff = jnp.dot(h2, w1_ref[:, j * tf:(j + 1) * tf], preferred_element_type=F32)
        ff = jnp.square(jnp.maximum(ff, 0.0))
        acc = acc + jnp.dot(ff.astype(BF16), w2_ref[j * tf:(j + 1) * tf, :],
                            preferred_element_type=F32)
    x2 = x1 + gate2 * acc
    if final:
        x2 = _rms(x2, fg_ref[...])
    o_ref[...] = x2


def _post(x, ya, yb, yc, mod, mod_row, layer, w_out_b, g2, w1_b, w2_b, fg, final, tm):
    t = x.shape[0]
    tok = lambda i: (i, 0)
    full = lambda i: (0, 0)
    return pl.pallas_call(
        functools.partial(_post_kernel, final),
        out_shape=jax.ShapeDtypeStruct((t, D_MODEL), F32),
        grid=(t // tm,),
        in_specs=[
            pl.BlockSpec((tm, D_MODEL), tok),
            pl.BlockSpec((tm, D_RWKV), tok),
            pl.BlockSpec((tm, D_FNET), tok),
            pl.BlockSpec((tm, D_POOL), tok),
            pl.BlockSpec((None, 1, 1, 6 * D_MODEL), lambda i: (layer, mod_row(i), 0, 0)),
            _resident((D_MODEL, D_MODEL), layer),
            _resident((1, D_MODEL), layer),
            _resident((D_MODEL, D_FF), layer),
            _resident((D_FF, D_MODEL), layer),
            pl.BlockSpec((1, D_MODEL), full),
        ],
        out_specs=pl.BlockSpec((tm, D_MODEL), tok),
        compiler_params=pltpu.CompilerParams(
            dimension_semantics=("arbitrary",), vmem_limit_bytes=VMEM_LIMIT),
        name="post",
    )(x, ya, yb, yc, mod, w_out_b, g2.reshape(DEPTH, 1, D_MODEL), w1_b, w2_b, fg.reshape(1, D_MODEL))


def _layer(x, bsz, seq, layer, mod, mod_row, s0, s_prev, want_state, pairs, consts, sw, mixw, fg, tm):
    (norm1_g, w_in_b, mu_shift, w0, w_up, a0, a_up, g_up, k_k, k_a, r_k, ln_w, ln_b, w_out_b, norm2_g,
     w1_b, w2_b) = sw
    wf_bd, wp_bd, pool_scale = mixw
    (pa, pf, pp), shifted = _pre(x, mod, mod_row, layer, norm1_g, w_in_b, mu_shift, seq, tm)
    ya, s_fin = _rwkv(pa.reshape(bsz, seq, SHIFT_END), shifted, layer, s0, s_prev, want_state, pairs,
                      mu_shift, w0, w_up, a0, a_up, g_up, k_k, k_a, r_k, ln_w, ln_b)
    yb, yc = _fpool(pf.reshape(bsz, seq, D_FNET), pp.reshape(bsz, seq, D_POOL), consts, wf_bd,
                    wp_bd, pool_scale)
    x = _post(x, ya.reshape(bsz * seq, D_RWKV), yb.reshape(bsz * seq, D_FNET),
              yc.reshape(bsz * seq, D_POOL), mod, mod_row, layer, w_out_b, norm2_g, w1_b, w2_b, fg,
              layer == DEPTH - 1, tm)
    return x, s_fin


def kernel(x_prompt, x_sample, state_wkv, c, c_ctx, w_ada, b_ada, norm1_g, w_in, mu_shift, w0, w_up, a0, a_up, g_up, k_k, k_a, r_k, ln_x_w, ln_x_b, w_fnet, w_pool, pool_scale, w_out, norm2_g, w_ff1, w_ff2, final_norm_g):
    bp, lp, _ = x_prompt.shape
    bs, ls, _ = x_sample.shape
    tm = 512
    n_rows = 8
    cond = jnp.concatenate([c_ctx[None], c, jnp.zeros((n_rows - 1 - bs, D_MODEL), F32)], axis=0)
    mod_all = _ada(cond, w_ada, b_ada)
    consts_p = _mixer_constants(lp, False)
    consts_s = _mixer_constants(ls, True)
    xp = x_prompt.reshape(bp * lp, D_MODEL)
    xs = x_sample.reshape(bs * ls, D_MODEL)
    tiles_per_seq = ls // tm
    row_p = lambda i: 0
    row_s = lambda i: 1 + i // tiles_per_seq
    sw = (norm1_g, w_in.astype(BF16), mu_shift, w0, w_up, a0, a_up, g_up, k_k, k_a, r_k, ln_x_w, ln_x_b,
          w_out.astype(BF16), norm2_g, w_ff1.astype(BF16), w_ff2.astype(BF16))
    mod = mod_all.reshape(DEPTH, n_rows, 1, 6 * D_MODEL)
    new_state_wkv = None
    for l in range(DEPTH):
        mixw = (_block_diag(w_fnet[l]), _block_diag(w_pool[l]), pool_scale[l])
        xp, new_state_wkv = _layer(xp, bp, lp, l, mod, row_p, None, new_state_wkv, True, 4, consts_p, sw,
                                   mixw, final_norm_g, tm)
        xs, _ = _layer(xs, bs, ls, l, mod, row_s, state_wkv, None, False, 2, consts_s, sw, mixw,
                       final_norm_g, tm)
    return (xp.reshape(bp, lp, D_MODEL), xs.reshape(bs, ls, D_MODEL), new_state_wkv)
```

```python
import functools
import math

import numpy as np
import jax
import jax.numpy as jnp
from jax import lax
from jax.experimental import pallas as pl
from jax.experimental.pallas import tpu as pltpu

F32 = jnp.float32
BF16 = jnp.bfloat16

D_MODEL = 1024
DEPTH = 2
GRID_W = 64
D_RWKV = 512
HEAD = 64
N_HEADS = 8
LANES = 128
N_PAIRS = D_RWKV // LANES
D_FNET = 256
FNET_GW = 64
D_POOL = 256
POOL_GW = 64
POOL_WINDOWS = (2, 4, 8, 16)
LORA = 64
GATE_LORA = 128
SHIFT_END = 3 * D_RWKV + 2 * LORA + 2 * LORA + GATE_LORA
LORA_W = SHIFT_END - 3 * D_RWKV
D_IN = SHIFT_END + D_FNET + D_POOL
D_FF = 4 * D_MODEL
RMS_EPS = 1e-6
GN_EPS = 64e-5
LOG2E = 1.4426950408889634
CHUNK = 64
VMEM_LIMIT = 56 * 1024 * 1024
STREAM_UNITS = 16
STREAM_LAG = 8
PRE_CHUNK = 512


def _dotf(a, b):
    return jnp.dot(a, b, preferred_element_type=F32, precision=lax.Precision.HIGHEST)


def _split(x, n):
    terms = []
    for _ in range(n - 1):
        hi = x.astype(BF16)
        terms.append(hi)
        x = x - hi.astype(F32)
    terms.append(x.astype(BF16))
    return terms


def _dots(a, b, na, nb, nt=False):
    dims = (((1,), (1,)), ((), ())) if nt else (((1,), (0,)), ((), ()))
    ta = _split(a, na)
    tb = _split(b, nb)
    out = None
    for i in range(na):
        for j in range(nb):
            if i + j < max(na, nb):
                t = lax.dot_general(ta[i], tb[j], dims, preferred_element_type=F32)
                out = t if out is None else out + t
    return out


def _mm(a, b):
    return jnp.dot(a.astype(BF16), b.astype(BF16), preferred_element_type=F32)


def _sigmoid(x):
    return 0.5 + 0.5 * jnp.tanh(0.5 * x)


def _rms(x, g):
    ms = jnp.mean(x * x, axis=-1, keepdims=True)
    return x * lax.rsqrt(ms + RMS_EPS) * g


_DONE = object()


def _interleave(streams):
    pending = sorted(streams, key=lambda s: s[0])
    active = []
    step = 0
    while pending or active:
        while pending and pending[0][0] <= step:
            active.append(pending.pop(0)[1])
        for s in list(active):
            if next(s, _DONE) is _DONE:
                active.remove(s)
        step += 1


def _ada_kernel(c_ref, w_ref, b_ref, o_ref):
    c = c_ref[...]
    s = c * _sigmoid(c)
    o_ref[0] = _dotf(s, w_ref[0]) + b_ref[0]


def _ada(cond, w_ada, b_ada):
    tn = 1536
    nrow = cond.shape[0]
    return pl.pallas_call(
        _ada_kernel,
        out_shape=jax.ShapeDtypeStruct((DEPTH, nrow, 6 * D_MODEL), F32),
        grid=(DEPTH, 6 * D_MODEL // tn),
        in_specs=[
            pl.BlockSpec((nrow, D_MODEL), lambda l, j: (0, 0)),
            pl.BlockSpec((1, D_MODEL, tn), lambda l, j: (l, 0, j)),
            pl.BlockSpec((1, 1, tn), lambda l, j: (l, 0, j)),
        ],
        out_specs=pl.BlockSpec((1, nrow, tn), lambda l, j: (l, 0, j)),
        compiler_params=pltpu.CompilerParams(
            dimension_semantics=("arbitrary", "arbitrary"), vmem_limit_bytes=VMEM_LIMIT),
        name="ada",
    )(cond, w_ada, b_ada.reshape(DEPTH, 1, 6 * D_MODEL))


def _shift_mix(p, mu, period):
    n = p.shape[0]
    pos = lax.broadcasted_iota(jnp.int32, p.shape, 0) % period
    prev = jnp.where(pos == 0, 0.0, pltpu.roll(p, 1, 0))
    nxt = jnp.where(pos == period - 1, 0.0, pltpu.roll(p, n - 1, 0))
    return p * (1.0 - mu) + (0.5 * mu) * (prev + nxt)


def _pre_kernel(shift_period, x_ref, mod_ref, g_ref, w_ref, mu_ref, oa_ref, of_ref, op_ref):
    mod = mod_ref[0]
    shift = mod[:, 0:D_MODEL]
    scale = mod[:, D_MODEL:2 * D_MODEL]
    h = (_rms(x_ref[...], g_ref[...]) * (1.0 + scale) + shift).astype(BF16)
    outs = ((oa_ref, 0, SHIFT_END), (of_ref, SHIFT_END, SHIFT_END + D_FNET), (op_ref, SHIFT_END + D_FNET, D_IN))
    for c0 in range(0, D_IN, PRE_CHUNK):
        c1 = min(c0 + PRE_CHUNK, D_IN)
        p = jnp.dot(h, w_ref[:, c0:c1], preferred_element_type=F32)
        if shift_period and c0 < SHIFT_END:
            p = _shift_mix(p, mu_ref[:, c0:c1], shift_period)
        for ref, o0, o1 in outs:
            lo, hi = max(c0, o0), min(c1, o1)
            if lo < hi:
                ref[:, lo - o0:hi - o0] = p[:, lo - c0:hi - c0]


def _resident(shape, layer):
    zeros = (0,) * len(shape)
    return pl.BlockSpec((None,) + shape, lambda *_: (layer,) + zeros, pipeline_mode=pl.Buffered(1))


def _pre(x, mod, mod_row, layer, g, w_in_b, mu, seq, tm):
    t = x.shape[0]
    shifted = tm % seq == 0
    outs = pl.pallas_call(
        functools.partial(_pre_kernel, seq if shifted else 0),
        out_shape=(jax.ShapeDtypeStruct((t, SHIFT_END), F32),
                   jax.ShapeDtypeStruct((t, D_FNET), F32),
                   jax.ShapeDtypeStruct((t, D_POOL), F32)),
        grid=(t // tm,),
        in_specs=[
            pl.BlockSpec((tm, D_MODEL), lambda i: (i, 0)),
            pl.BlockSpec((None, 1, 1, 6 * D_MODEL), lambda i: (layer, mod_row(i), 0, 0)),
            _resident((1, D_MODEL), layer),
            _resident((D_MODEL, D_IN), layer),
            _resident((1, D_IN), layer),
        ],
        out_specs=(pl.BlockSpec((tm, SHIFT_END), lambda i: (i, 0)),
                   pl.BlockSpec((tm, D_FNET), lambda i: (i, 0)),
                   pl.BlockSpec((tm, D_POOL), lambda i: (i, 0))),
        compiler_params=pltpu.CompilerParams(
            dimension_semantics=("arbitrary",), vmem_limit_bytes=VMEM_LIMIT),
        name="pre",
    )(x, mod, g.reshape(DEPTH, 1, D_MODEL), w_in_b,
      jnp.pad(mu, ((0, 0), (0, D_IN - SHIFT_END))).reshape(DEPTH, 1, D_IN))
    return outs, shifted


def _rwkv_kernel(seq, pp, shifted, has_state, want_state, alias_state, layer, *refs):
    (r_ref, k_ref, v_ref, lo_ref, mur_ref, muk_ref, muv_ref, mul_ref, w0_ref, wup_ref, a0_ref,
     aup_ref, gup_ref, kk_ref, ka_ref, rk_ref, lnw_ref, lnb_ref) = refs[:18]
    n_in = 18 + has_state + alias_state
    s0_ref = refs[18] if has_state else None
    y_ref = refs[n_in]
    sfin_ref = refs[n_in + 1] if want_state else None
    lw_s, kd_s, b_s, r_s, v_s, kn_s, st_s, rb_s, yb_s, gt_s, ht_s = refs[n_in + 1 + want_state:]
    nc = seq // CHUNK
    C = CHUNK
    tiles = [slice(i * LANES, (i + 1) * LANES) for i in range(pp)]

    lane1 = lax.broadcasted_iota(jnp.int32, (1, LANES), 1)
    head0 = lane1 < HEAD
    head1 = jnp.logical_not(head0)
    ri = lax.broadcasted_iota(jnp.int32, (LANES, LANES), 0)
    ci = lax.broadcasted_iota(jnp.int32, (LANES, LANES), 1)
    same_head = (ri // HEAD) == (ci // HEAD)
    seg_ones = same_head.astype(F32)
    eye = (ri == ci).astype(F32)

    def seg_sum(x):
        return jnp.concatenate([_mm(x[:, t], seg_ones) for t in tiles], axis=1)

    if shifted:
        r, k, v, lo = r_ref[0], k_ref[0], v_ref[0], lo_ref[0]
    else:
        r = _shift_mix(r_ref[0], mur_ref[...], seq)
        k = _shift_mix(k_ref[0], muk_ref[...], seq)
        v = _shift_mix(v_ref[0], muv_ref[...], seq)
        lo = _shift_mix(lo_ref[0], mul_ref[...], seq)
    wd = jnp.tanh(lo[:, 0:2 * LORA])
    ad = lo[:, 2 * LORA:4 * LORA]
    gd = _sigmoid(lo[:, 4 * LORA:])
    gate = _mm(gd, gup_ref[...])
    kx = k * kk_ref[...]
    kn = kx * lax.rsqrt(jnp.maximum(seg_sum(kx * kx), 1e-24))
    zeros_up = jnp.zeros((LORA, pp * LANES), F32)
    kd_sum = jnp.zeros_like(k)
    for d in range(2):
        if d == 0:
            wup = jnp.concatenate([wup_ref[0], zeros_up], axis=0)
            aup = jnp.concatenate([aup_ref[0], zeros_up], axis=0)
        else:
            wup = jnp.concatenate([zeros_up, wup_ref[1]], axis=0)
            aup = jnp.concatenate([zeros_up, aup_ref[1]], axis=0)
        zw = w0_ref[d:d + 1, :] + _mm(wd, wup)
        lw2 = (-LOG2E * math.exp(-0.5)) * _sigmoid(zw)
        lw_s[d] = lw2
        a_sig = _sigmoid(a0_ref[d:d + 1, :] + _mm(ad, aup))
        kd = k * (1.0 + (a_sig - 1.0) * ka_ref[...])
        kd_s[d] = kd
        b_s[d] = kn * a_sig
        kn_s[d] = -kn * jnp.exp2(-lw2)
        kd_sum = kd_sum + kd
    r_s[...] = r
    v_s[...] = v
    bonus = seg_sum(r * kd_sum * rk_ref[...]) * v

    if has_state:
        rj = lax.broadcasted_iota(jnp.int32, (LANES, HEAD), 0)
        cj = lax.broadcasted_iota(jnp.int32, (LANES, HEAD), 1)
        place = [(rj == cj + j * HEAD).astype(F32) for j in range(2)]
        keys = [(d, pi, j) for d in range(2) for pi in range(pp) for j in range(2)]
        t1 = {key: _dots(place[key[2]], s0_ref[0, key[0], 2 * key[1] + key[2]], 1, 3, nt=True)
              for key in keys}
        t2 = {key: _dots(t1[key], place[key[2]], 3, 1, nt=True) for key in keys}
        for d in range(2):
            for pi in range(pp):
                st_s[d, pi] = t2[(d, pi, 0)] + t2[(d, pi, 1)]
    else:
        st_s[...] = jnp.zeros(st_s.shape, F32)

    tr = lax.broadcasted_iota(jnp.int32, (C, C), 0)
    tc = lax.broadcasted_iota(jnp.int32, (C, C), 1)
    t_p = lax.broadcasted_iota(jnp.int32, (C, LANES), 0)
    s_p = lax.broadcasted_iota(jnp.int32, (C, LANES), 1) % C
    tris = ((tc <= tr).astype(F32), (tc >= tr).astype(F32))
    strict = (s_p < t_p, s_p > t_p)
    inclusive = (s_p <= t_p, s_p >= t_p)
    eye_p = (s_p == t_p).astype(F32)
    txs = t_p ^ s_p
    lvl_map = sum(((txs >= (1 << bit)).astype(jnp.int32) for bit in range(1, 6)),
                  jnp.where(txs == 0, -1, 0))
    zero_b = jnp.zeros((C, LANES), BF16)

    def bd(x):
        return jnp.concatenate([jnp.where(head0, x, zero_b), jnp.where(head1, x, zero_b)], axis=0)

    def bdx(x):
        return jnp.concatenate([jnp.where(head1, x, zero_b), jnp.where(head0, x, zero_b)], axis=0)

    def mm(a, b):
        return jnp.dot(a, b, preferred_element_type=F32)

    def rows(c):
        return pl.ds(c * C, C) if isinstance(c, int) else pl.ds(pl.multiple_of(c * C, C), C)

    def algebra(units):
        u = []
        for c, pi, d in units:
            sl, ln = rows(c), tiles[pi]
            lw = lw_s[d, sl, ln]
            u.append(dict(c=c, pi=pi, d=d, sl=sl, ln=ln, lw=lw, cl=_dots(tris[d], lw, 1, 2)))
        yield
        for q in u:
            d, sl, ln, cl = q["d"], q["sl"], q["ln"], q["cl"]
            tot = cl[C - 1:C, :] if d == 0 else cl[0:1, :]
            e_in = jnp.exp2(cl)
            e_inv = jnp.exp2(-cl)
            p_c = jnp.exp2(tot)
            q["at"] = (kn_s[d, sl, ln] * e_in).astype(BF16)
            q["rt"] = r_s[sl, ln] * e_in
            bt = b_s[d, sl, ln] * e_inv
            kt = kd_s[d, sl, ln] * e_inv
            q["lhs_t"] = jnp.concatenate([bt * p_c, kt * p_c], axis=0).T.astype(BF16)
            bt = bt.astype(BF16)
            kt = kt.astype(BF16)
            q["rhs_nt"] = jnp.concatenate([jnp.where(head0, bt, zero_b), jnp.where(head1, bt, zero_b),
                                           jnp.where(head1, kt, zero_b), jnp.where(head0, kt, zero_b)], axis=0)
            q["p_c"] = p_c
            q["vc"] = v_s[sl, ln].astype(BF16)
            yield
        for q in u:
            lhs = jnp.concatenate([q["at"], q["rt"].astype(BF16)], axis=0)
            q["sc"] = lax.dot_general(lhs, q["rhs_nt"], (((1,), (1,)), ((), ())), preferred_element_type=F32)
        yield
        for q in u:
            d, sc = q["d"], q["sc"]
            lab = jnp.where(strict[d], sc[:C, :LANES], 0.0)
            lakx = jnp.where(strict[d], sc[:C, LANES:], 0.0)
            q["mr"] = jnp.concatenate([jnp.where(inclusive[d], sc[C:, :LANES], 0.0),
                                       jnp.where(inclusive[d], sc[C:, LANES:], 0.0)], axis=1).astype(BF16)
            q["xv"] = mm(lakx.astype(BF16), bdx(q["vc"]))
            q["t"] = eye_p + jnp.where(lvl_map == 0, lab, 0.0)
            q["lab"] = lab
        yield
        for lvl in range(1, 6):
            for q in u:
                q["tb"] = q["t"].astype(BF16)
                q["w"] = mm(jnp.where(lvl_map == lvl, q["lab"], 0.0).astype(BF16), bd(q["tb"]))
            yield
            for q in u:
                q["t"] = q["t"] + mm(q["tb"], bd(q["w"].astype(BF16)))
            yield
        for q in u:
            x = mm(q["t"].astype(BF16),
                   jnp.concatenate([bd(q["at"]), bd(q["xv"].astype(BF16))], axis=1))
            q["ab"] = x[:, :LANES].astype(BF16)
            q["ub"] = x[:, LANES:].astype(BF16)
        yield
        for q in u:
            rhs = jnp.concatenate([jnp.concatenate([bd(q["ab"]), bd(q["ub"])], axis=1),
                                   jnp.concatenate([jnp.concatenate([zero_b, zero_b], axis=0), bdx(q["vc"])],
                                                   axis=1)], axis=0)
            q["o2"] = mm(q["mr"], rhs)
            rhs2 = jnp.concatenate([jnp.concatenate([q["ab"], q["ub"]], axis=1),
                                    jnp.concatenate([zero_b, q["vc"]], axis=1)], axis=0)
            q["gh"] = mm(q["lhs_t"], rhs2)
        yield
        for q in u:
            c, pi, d, sl, ln, o2, gh = q["c"], q["pi"], q["d"], q["sl"], q["ln"], q["o2"], q["gh"]
            rb_s[d, sl, ln] = (q["rt"] + o2[:, :LANES]).astype(BF16)
            yb_s[d, sl, ln] = o2[:, LANES:]
            gt_s[d, pi, c] = (jnp.where(same_head, gh[:, :LANES], 0.0) + eye * q["p_c"]).astype(BF16)
            ht_s[d, pi, c] = jnp.where(same_head, gh[:, LANES:], 0.0)
        yield

    def state_chain(g, first, count):
        while not (("algebra", g) in issued and (g == 0 or ("chain", g - 1) in issued)):
            yield
        for i in range(first, first + count):
            items = []
            for pi in range(pp):
                for d in range(2):
                    c = i if d == 0 else nc - 1 - i
                    sl, ln = rows(c), tiles[pi]
                    lhs = jnp.concatenate([rb_s[d, sl, ln], gt_s[d, pi, c]], axis=0)
                    items.append((pi, d, c, sl, ln, mm(lhs, st_s[d, pi].astype(BF16))))
            yield
            for pi, d, c, sl, ln, o in items:
                y_ref[0, sl, ln] += o[:C] + yb_s[d, sl, ln]
                st_s[d, pi] = o[C:] + ht_s[d, pi, c]
            yield
        issued.add(("chain", g))

    y_ref[0] = jnp.zeros(y_ref.shape[1:], F32)

    per = max(1, STREAM_UNITS // (2 * pp))
    issued = set()

    def tracked(g, units):
        yield from algebra(units)
        issued.add(("algebra", g))

    streams = []
    for grp in range(nc // per):
        units = [(i if d == 0 else nc - 1 - i, pi, d)
                 for i in range(grp * per, (grp + 1) * per) for pi in range(pp) for d in range(2)]
        streams.append((grp * STREAM_LAG, tracked(grp, units)))
        streams.append((grp * STREAM_LAG, state_chain(grp, grp * per, per)))
    _interleave(streams)

    if want_state:
        if alias_state:
            out = sfin_ref.at[0]
        else:
            out = sfin_ref.at[0, layer]
            for other in range(DEPTH):
                if other != layer:
                    sfin_ref[0, other] = jnp.zeros(sfin_ref.shape[2:], F32)
        for d in range(2):
            for pi in range(pp):
                stt = st_s[d, pi].T
                out[d, 2 * pi] = stt[:HEAD, :HEAD]
                out[d, 2 * pi + 1] = pltpu.roll(stt[HEAD:], HEAD, 1)[:, :HEAD]

    y = y_ref[0]
    yc = y - seg_sum(y) * (1.0 / HEAD)
    var = seg_sum(yc * yc) * (1.0 / HEAD)
    yn = yc * lax.rsqrt(var + GN_EPS) * lnw_ref[...] + lnb_ref[...]
    y_ref[0] = (yn + bonus) * gate


def _rwkv(pa, shifted, layer, s0, s_prev, want_state, pp, mu, w0, w_up, a0, a_up, g_up, k_k, k_a, r_k, ln_w,
          ln_b):
    bsz, seq, _ = pa.shape
    has_state = s0 is not None
    alias_state = want_state and s_prev is not None
    width = pp * LANES
    nblk = D_RWKV // width
    nc = seq // CHUNK
    mu3 = mu.reshape(DEPTH, 1, SHIFT_END)
    row = lambda a: a.reshape(DEPTH, 1, D_RWKV)
    col = lambda o: (lambda b, p: (b, 0, o * nblk + p))
    vec = lambda b, p: (layer, 0, p)
    lora_blk = 3 * D_RWKV // LORA_W
    in_specs = [
        pl.BlockSpec((1, seq, width), col(0)),
        pl.BlockSpec((1, seq, width), col(1)),
        pl.BlockSpec((1, seq, width), col(2)),
        pl.BlockSpec((1, seq, LORA_W), lambda b, p: (b, 0, lora_blk)),
        pl.BlockSpec((None, 1, width), lambda b, p: (layer, 0, p)),
        pl.BlockSpec((None, 1, width), lambda b, p: (layer, 0, nblk + p)),
        pl.BlockSpec((None, 1, width), lambda b, p: (layer, 0, 2 * nblk + p)),
        pl.BlockSpec((None, 1, LORA_W), lambda b, p: (layer, 0, lora_blk)),
        pl.BlockSpec((None, 2, width), vec),
        pl.BlockSpec((None, 2, LORA, width), lambda b, p: (layer, 0, 0, p)),
        pl.BlockSpec((None, 2, width), vec),
        pl.BlockSpec((None, 2, LORA, width), lambda b, p: (layer, 0, 0, p)),
        pl.BlockSpec((None, GATE_LORA, width), vec),
        pl.BlockSpec((None, 1, width), vec),
        pl.BlockSpec((None, 1, width), vec),
        pl.BlockSpec((None, 1, width), vec),
        pl.BlockSpec((None, 1, width), vec),
        pl.BlockSpec((None, 1, width), vec),
    ]
    args = [pa, pa, pa, pa, mu3, mu3, mu3, mu3, w0, w_up, a0, a_up, g_up, row(k_k), row(k_a),
            row(r_k), row(ln_w), row(ln_b)]
    state_spec = pl.BlockSpec((1, None, 2, 2 * pp, HEAD, HEAD), lambda b, p: (b, layer, 0, p, 0, 0))
    if has_state:
        in_specs.append(state_spec)
        args.append(s0)
    aliases = {}
    if alias_state:
        aliases[len(args)] = 1
        in_specs.append(pl.BlockSpec(memory_space=pl.ANY))
        args.append(s_prev)
    out_shape = [jax.ShapeDtypeStruct((bsz, seq, D_RWKV), F32)]
    out_specs = [pl.BlockSpec((1, seq, width), lambda b, p: (b, 0, p))]
    if want_state:
        out_shape.append(jax.ShapeDtypeStruct((bsz, DEPTH, 2, N_HEADS, HEAD, HEAD), F32))
        if alias_state:
            out_specs.append(state_spec)
        else:
            out_specs.append(pl.BlockSpec((1, DEPTH, 2, 2 * pp, HEAD, HEAD), lambda b, p: (b, 0, 0, p, 0, 0)))
    outs = pl.pallas_call(
        functools.partial(_rwkv_kernel, seq, pp, shifted, has_state, want_state, alias_state, layer),
        out_shape=tuple(out_shape),
        grid=(bsz, nblk),
        in_specs=in_specs,
        out_specs=tuple(out_specs),
        input_output_aliases=aliases,
        scratch_shapes=[
            pltpu.VMEM((2, seq, width), F32),
            pltpu.VMEM((2, seq, width), F32),
            pltpu.VMEM((2, seq, width), F32),
            pltpu.VMEM((seq, width), F32),
            pltpu.VMEM((seq, width), F32),
            pltpu.VMEM((2, seq, width), F32),
            pltpu.VMEM((2, pp, LANES, LANES), F32),
            pltpu.VMEM((2, seq, width), BF16),
            pltpu.VMEM((2, seq, width), F32),
            pltpu.VMEM((2, pp, nc, LANES, LANES), BF16),
            pltpu.VMEM((2, pp, nc, LANES, LANES), F32),
        ],
        compiler_params=pltpu.CompilerParams(
            dimension_semantics=("arbitrary", "arbitrary"), vmem_limit_bytes=VMEM_LIMIT),
        name="rwkv",
    )(*args)
    return (outs[0], outs[1]) if want_state else (outs[0], None)


def _fpool_kernel(seq, nb, f_ref, p_ref, cs_ref, c64_ref, s64_ref, wf_ref, band_ref, inv_ref, wp_ref,
                  ps_ref, yb_ref, yc_ref, cw_s, cs_s):
    @pl.when(pl.program_id(0) == 0)
    def _():
        wf = wf_ref[...]
        cw_s[:, :D_FNET] = _dotf(c64_ref[...], wf).astype(BF16)
        cw_s[:, D_FNET:] = (-_dotf(s64_ref[...], wf)).astype(BF16)
        cs_s[...] = cs_ref[...].astype(BF16)

    f = f_ref[...].reshape(nb * seq, D_FNET)
    g = jnp.dot(f.astype(BF16), cw_s[...], preferred_element_type=F32).astype(BF16)
    stacked = jnp.concatenate(
        [jnp.concatenate([g[i * seq:(i + 1) * seq, :D_FNET], g[i * seq:(i + 1) * seq, D_FNET:]], axis=0)
         for i in range(nb)], axis=1)
    yb = jnp.dot(cs_s[...], stacked, preferred_element_type=F32)
    for i in range(nb):
        yb_ref[i] = yb[:, i * D_FNET:(i + 1) * D_FNET]

    p = jnp.concatenate([p_ref[i] for i in range(nb)], axis=1)
    p_b = p.astype(BF16)
    group = (lax.broadcasted_iota(jnp.int32, p.shape, 1) % D_POOL) // POOL_GW
    d = jnp.zeros(p.shape, F32)
    for i in range(len(POOL_WINDOWS)):
        s = jnp.dot(band_ref[i], p_b, preferred_element_type=F32)
        d = jnp.where(group == i, s, d)
    d = d * jnp.concatenate([inv_ref[...]] * nb, axis=1) - p
    d = jnp.concatenate([d[:, i * D_POOL:(i + 1) * D_POOL] for i in range(nb)], axis=0)
    yc_ref[...] = (_mm(d, wp_ref[...]) * ps_ref[...]).reshape(nb, seq, D_POOL)


def _fpool(pf, pp, consts, wf_bd, wp_bd, pool_scale):
    bsz, seq, _ = pf.shape
    cs, c64, s64, band, inv = consts
    nb = max(1, min(bsz, 1024 // seq))
    full2 = lambda b: (0, 0)
    return pl.pallas_call(
        functools.partial(_fpool_kernel, seq, nb),
        out_shape=(jax.ShapeDtypeStruct((bsz, seq, D_FNET), F32),
                   jax.ShapeDtypeStruct((bsz, seq, D_POOL), F32)),
        grid=(bsz // nb,),
        in_specs=[
            pl.BlockSpec((nb, seq, D_FNET), lambda b: (b, 0, 0)),
            pl.BlockSpec((nb, seq, D_POOL), lambda b: (b, 0, 0)),
            pl.BlockSpec((seq, 2 * seq), full2),
            pl.BlockSpec((D_FNET, D_FNET), full2),
            pl.BlockSpec((D_FNET, D_FNET), full2),
            pl.BlockSpec((D_FNET, D_FNET), full2),
            pl.BlockSpec((len(POOL_WINDOWS), seq, seq), lambda b: (0, 0, 0)),
            pl.BlockSpec((seq, D_POOL), full2),
            pl.BlockSpec((D_POOL, D_POOL), full2),
            pl.BlockSpec((1, D_POOL), full2),
        ],
        out_specs=(pl.BlockSpec((nb, seq, D_FNET), lambda b: (b, 0, 0)),
                   pl.BlockSpec((nb, seq, D_POOL), lambda b: (b, 0, 0))),
        scratch_shapes=[pltpu.VMEM((D_FNET, 2 * D_FNET), BF16),
                        pltpu.VMEM((seq, 2 * seq), BF16)],
        compiler_params=pltpu.CompilerParams(
            dimension_semantics=("arbitrary",), vmem_limit_bytes=VMEM_LIMIT),
        name="fpool",
    )(pf, pp, cs, c64, s64, wf_bd, band, inv, wp_bd, pool_scale.reshape(1, D_POOL))


def _window_matrix(n, w):
    t = np.arange(n)
    lo = np.clip(t - w // 2, 0, n)
    hi = np.clip(t + w - w // 2, 0, n)
    s = np.arange(n)
    return ((s[None, :] >= lo[:, None]) & (s[None, :] < hi[:, None])).astype(np.float64)


def _mixer_constants(seq, grid):
    n = np.arange(seq)
    ang = 2.0 * np.pi * ((n[:, None] * n[None, :]) % seq) / seq
    scale = 1.0 / math.sqrt(seq * FNET_GW)
    cs = np.concatenate([np.cos(ang), np.sin(ang)], axis=1) * scale
    m = np.arange(FNET_GW)
    ang64 = 2.0 * np.pi * ((m[:, None] * m[None, :]) % FNET_GW) / FNET_GW
    groups = D_FNET // FNET_GW
    c64 = np.kron(np.eye(groups), np.cos(ang64))
    s64 = np.kron(np.eye(groups), np.sin(ang64))
    bands = []
    invs = []
    for w in POOL_WINDOWS:
        if grid:
            rows = seq // GRID_W
            band = np.kron(_window_matrix(rows, w), _window_matrix(GRID_W, w))
        else:
            band = _window_matrix(seq, w)
        bands.append(band)
        invs.append(np.repeat((1.0 / band.sum(axis=1))[:, None], POOL_GW, axis=1))
    band = jnp.asarray(np.stack(bands), dtype=BF16)
    inv = jnp.asarray(np.concatenate(invs, axis=1), dtype=F32)
    return (jnp.asarray(cs, dtype=F32), jnp.asarray(c64, dtype=F32), jnp.asarray(s64, dtype=F32),
            band, inv)


def _block_diag(w):
    g, n, _ = w.shape
    return (w[:, :, None, :] * jnp.eye(g, dtype=w.dtype)[:, None, :, None]).reshape(g * n, g * n)


def _post_kernel(final, x_ref, ya_ref, yb_ref, yc_ref, mod_ref, wo_ref, g2_ref, w1_ref, w2_ref,
                 fg_ref, o_ref):
    mod = mod_ref[0]
    gate1 = mod[:, 2 * D_MODEL:3 * D_MODEL]
    shift2 = mod[:, 3 * D_MODEL:4 * D_MODEL]
    scale2 = mod[:, 4 * D_MODEL:5 * D_MODEL]
    gate2 = mod[:, 5 * D_MODEL:]
    mix = (jnp.dot(ya_ref[...].astype(BF16), wo_ref[0:D_RWKV, :], preferred_element_type=F32)
           + jnp.dot(yb_ref[...].astype(BF16), wo_ref[D_RWKV:D_RWKV + D_FNET, :],
                     preferred_element_type=F32)
           + jnp.dot(yc_ref[...].astype(BF16), wo_ref[D_RWKV + D_FNET:, :],
                     preferred_element_type=F32))
    x1 = x_ref[...] + gate1 * mix
    h2 = (_rms(x1, g2_ref[...]) * (1.0 + scale2) + shift2).astype(BF16)
    tf = 1024
    acc = jnp.zeros_like(x1)
    for j in range(D_FF // tf):
        ff = jnp.dot(h2, w1_ref[:, j * tf:(j + 1) * tf], preferred_element_type=F32)
        ff = jnp.square(jnp.maximum(ff, 0.0))
        acc = acc + jnp.dot(ff.astype(BF16), w2_ref[j * tf:(j + 1) * tf, :],
                            preferred_element_type=F32)
    x2 = x1 + gate2 * acc
    if final:
        x2 = _rms(x2, fg_ref[...])
    o_ref[...] = x2


def _post(x, ya, yb, yc, mod, mod_row, layer, w_out_b, g2, w1_b, w2_b, fg, final, tm):
    t = x.shape[0]
    tok = lambda i: (i, 0)
    full = lambda i: (0, 0)
    return pl.pallas_call(
        functools.partial(_post_kernel, final),
        out_shape=jax.ShapeDtypeStruct((t, D_MODEL), F32),
        grid=(t // tm,),
        in_specs=[
            pl.BlockSpec((tm, D_MODEL), tok),
            pl.BlockSpec((tm, D_RWKV), tok),
            pl.BlockSpec((tm, D_FNET), tok),
            pl.BlockSpec((tm, D_POOL), tok),
            pl.BlockSpec((None, 1, 1, 6 * D_MODEL), lambda i: (layer, mod_row(i), 0, 0)),
            _resident((D_MODEL, D_MODEL), layer),
            _resident((1, D_MODEL), layer),
            _resident((D_MODEL, D_FF), layer),
            _resident((D_FF, D_MODEL), layer),
            pl.BlockSpec((1, D_MODEL), full),
        ],
        out_specs=pl.BlockSpec((tm, D_MODEL), tok),
        compiler_params=pltpu.CompilerParams(
            dimension_semantics=("arbitrary",), vmem_limit_bytes=VMEM_LIMIT),
        name="post",
    )(x, ya, yb, yc, mod, w_out_b, g2.reshape(DEPTH, 1, D_MODEL), w1_b, w2_b, fg.reshape(1, D_MODEL))


def _layer(x, bsz, seq, layer, mod, mod_rows, s0, s_prev, want_state, pairs, consts, sw, mixw, fg, tm):
    (norm1_g, w_in_b, mu_shift, w0, w_up, a0, a_up, g_up, k_k, k_a, r_k, ln_w, ln_b, w_out_b, norm2_g,
     w1_b, w2_b) = sw
    wf_bd, wp_bd, pool_scale = mixw
    tm_pre = max(tm, seq)
    mod_row = mod_rows(tm)
    (pa, pf, pp), shifted = _pre(x, mod, mod_rows(tm_pre), layer, norm1_g, w_in_b, mu_shift, seq, tm_pre)
    ya, s_fin = _rwkv(pa.reshape(bsz, seq, SHIFT_END), shifted, layer, s0, s_prev, want_state, pairs,
                      mu_shift, w0, w_up, a0, a_up, g_up, k_k, k_a, r_k, ln_w, ln_b)
    yb, yc = _fpool(pf.reshape(bsz, seq, D_FNET), pp.reshape(bsz, seq, D_POOL), consts, wf_bd,
                    wp_bd, pool_scale)
    x = _post(x, ya.reshape(bsz * seq, D_RWKV), yb.reshape(bsz * seq, D_FNET),
              yc.reshape(bsz * seq, D_POOL), mod, mod_row, layer, w_out_b, norm2_g, w1_b, w2_b, fg,
              layer == DEPTH - 1, tm)
    return x, s_fin


def kernel(x_prompt, x_sample, state_wkv, c, c_ctx, w_ada, b_ada, norm1_g, w_in, mu_shift, w0, w_up, a0, a_up, g_up, k_k, k_a, r_k, ln_x_w, ln_x_b, w_fnet, w_pool, pool_scale, w_out, norm2_g, w_ff1, w_ff2, final_norm_g):
    bp, lp, _ = x_prompt.shape
    bs, ls, _ = x_sample.shape
    tm = 512
    n_rows = 8
    cond = jnp.concatenate([c_ctx[None], c, jnp.zeros((n_rows - 1 - bs, D_MODEL), F32)], axis=0)
    mod_all = _ada(cond, w_ada, b_ada)
    consts_p = _mixer_constants(lp, False)
    consts_s = _mixer_constants(ls, True)
    xp = x_prompt.reshape(bp * lp, D_MODEL)
    xs = x_sample.reshape(bs * ls, D_MODEL)
    row_p = lambda tile: (lambda i: 0)
    row_s = lambda tile: (lambda i: 1 + (i * tile) // ls)
    sw = (norm1_g, w_in.astype(BF16), mu_shift, w0, w_up, a0, a_up, g_up, k_k, k_a, r_k, ln_x_w, ln_x_b,
          w_out.astype(BF16), norm2_g, w_ff1.astype(BF16), w_ff2.astype(BF16))
    mod = mod_all.reshape(DEPTH, n_rows, 1, 6 * D_MODEL)
    new_state_wkv = None
    for l in range(DEPTH):
        mixw = (_block_diag(w_fnet[l]), _block_diag(w_pool[l]), pool_scale[l])
        xp, new_state_wkv = _layer(xp, bp, lp, l, mod, row_p, None, new_state_wkv, True, 4, consts_p, sw,
                                   mixw, final_norm_g, tm)
        xs, _ = _layer(xs, bs, ls, l, mod, row_s, state_wkv, None, False, 2, consts_s, sw, mixw,
                       final_norm_g, tm)
    return (xp.reshape(bp, lp, D_MODEL), xs.reshape(bs, ls, D_MODEL), new_state_wkv)
```

```python
import functools
import math

import numpy as np
import jax
import jax.numpy as jnp
from jax import lax
from jax.experimental import pallas as pl
from jax.experimental.pallas import tpu as pltpu

F32 = jnp.float32
BF16 = jnp.bfloat16

D_MODEL = 1024
DEPTH = 2
GRID_W = 64
D_RWKV = 512
HEAD = 64
N_HEADS = 8
LANES = 128
D_FNET = 256
FNET_GW = 64
D_POOL = 256
POOL_GW = 64
POOL_WINDOWS = (2, 4, 8, 16)
LORA = 64
GATE_LORA = 128
SHIFT_END = 3 * D_RWKV + 2 * LORA + 2 * LORA + GATE_LORA
LORA_W = SHIFT_END - 3 * D_RWKV
D_IN = SHIFT_END + D_FNET + D_POOL
D_FF = 4 * D_MODEL
RMS_EPS = 1e-6
GN_EPS = 64e-5
LOG2E = 1.4426950408889634
CHUNK = 64
VMEM_LIMIT = 56 * 1024 * 1024
STREAM_UNITS = 16
STREAM_LAG = 8
PRE_CHUNK = 512


def _dotf(a, b):
    return jnp.dot(a, b, preferred_element_type=F32, precision=lax.Precision.HIGHEST)


def _split(x, n):
    terms = []
    for _ in range(n - 1):
        hi = x.astype(BF16)
        terms.append(hi)
        x = x - hi.astype(F32)
    terms.append(x.astype(BF16))
    return terms


def _dots(a, b, na, nb, nt=False):
    dims = (((1,), (1,)), ((), ())) if nt else (((1,), (0,)), ((), ()))
    ta = _split(a, na)
    tb = _split(b, nb)
    out = None
    for i in range(na):
        for j in range(nb):
            if i + j < max(na, nb):
                t = lax.dot_general(ta[i], tb[j], dims, preferred_element_type=F32)
                out = t if out is None else out + t
    return out


def _mm(a, b):
    return jnp.dot(a.astype(BF16), b.astype(BF16), preferred_element_type=F32)


def _sigmoid(x):
    return 0.5 + 0.5 * jnp.tanh(0.5 * x)


def _rms(x, g):
    ms = jnp.mean(x * x, axis=-1, keepdims=True)
    return x * lax.rsqrt(ms + RMS_EPS) * g


_DONE = object()


def _interleave(streams):
    pending = sorted(streams, key=lambda s: s[0])
    active = []
    step = 0
    while pending or active:
        while pending and pending[0][0] <= step:
            active.append(pending.pop(0)[1])
        for s in list(active):
            if next(s, _DONE) is _DONE:
                active.remove(s)
        step += 1


def _ada_kernel(c_ref, w_ref, b_ref, o_ref):
    c = c_ref[...]
    s = c * _sigmoid(c)
    o_ref[0] = _dots(s, w_ref[0], 2, 2) + b_ref[0]


def _ada(cond, w_ada, b_ada):
    tn = 1536
    nrow = cond.shape[0]
    return pl.pallas_call(
        _ada_kernel,
        out_shape=jax.ShapeDtypeStruct((DEPTH, nrow, 6 * D_MODEL), F32),
        grid=(DEPTH, 6 * D_MODEL // tn),
        in_specs=[
            pl.BlockSpec((nrow, D_MODEL), lambda l, j: (0, 0)),
            pl.BlockSpec((1, D_MODEL, tn), lambda l, j: (l, 0, j)),
            pl.BlockSpec((1, 1, tn), lambda l, j: (l, 0, j)),
        ],
        out_specs=pl.BlockSpec((1, nrow, tn), lambda l, j: (l, 0, j)),
        compiler_params=pltpu.CompilerParams(
            dimension_semantics=("arbitrary", "arbitrary"), vmem_limit_bytes=VMEM_LIMIT),
        name="ada",
    )(cond, w_ada, b_ada.reshape(DEPTH, 1, 6 * D_MODEL))


def _shift_mix(p, mu, period):
    n = p.shape[0]
    pos = lax.broadcasted_iota(jnp.int32, p.shape, 0) % period
    prev = jnp.where(pos == 0, 0.0, pltpu.roll(p, 1, 0))
    nxt = jnp.where(pos == period - 1, 0.0, pltpu.roll(p, n - 1, 0))
    return p * (1.0 - mu) + (0.5 * mu) * (prev + nxt)


def _pre_kernel(shift_period, x_ref, mod_ref, g_ref, w_ref, mu_ref, oa_ref, of_ref, op_ref):
    mod = mod_ref[0]
    shift = mod[:, 0:D_MODEL]
    scale = mod[:, D_MODEL:2 * D_MODEL]
    h = (_rms(x_ref[...], g_ref[...]) * (1.0 + scale) + shift).astype(BF16)
    outs = ((oa_ref, 0, SHIFT_END), (of_ref, SHIFT_END, SHIFT_END + D_FNET), (op_ref, SHIFT_END + D_FNET, D_IN))
    for c0 in range(0, D_IN, PRE_CHUNK):
        c1 = min(c0 + PRE_CHUNK, D_IN)
        p = jnp.dot(h, w_ref[:, c0:c1], preferred_element_type=F32)
        if shift_period and c0 < SHIFT_END:
            p = _shift_mix(p, mu_ref[:, c0:c1], shift_period)
        for ref, o0, o1 in outs:
            lo, hi = max(c0, o0), min(c1, o1)
            if lo < hi:
                ref[:, lo - o0:hi - o0] = p[:, lo - c0:hi - c0]


def _resident(shape, layer):
    zeros = (0,) * len(shape)
    return pl.BlockSpec((None,) + shape, lambda *_: (layer,) + zeros, pipeline_mode=pl.Buffered(1))


def _pre(x, mod, mod_row, layer, g, w_in_b, mu, seq, tm):
    t = x.shape[0]
    shifted = tm % seq == 0
    outs = pl.pallas_call(
        functools.partial(_pre_kernel, seq if shifted else 0),
        out_shape=(jax.ShapeDtypeStruct((t, SHIFT_END), F32),
                   jax.ShapeDtypeStruct((t, D_FNET), F32),
                   jax.ShapeDtypeStruct((t, D_POOL), F32)),
        grid=(t // tm,),
        in_specs=[
            pl.BlockSpec((tm, D_MODEL), lambda i: (i, 0)),
            pl.BlockSpec((None, 1, 1, 6 * D_MODEL), lambda i: (layer, mod_row(i), 0, 0)),
            _resident((1, D_MODEL), layer),
            _resident((D_MODEL, D_IN), layer),
            _resident((1, D_IN), layer),
        ],
        out_specs=(pl.BlockSpec((tm, SHIFT_END), lambda i: (i, 0)),
                   pl.BlockSpec((tm, D_FNET), lambda i: (i, 0)),
                   pl.BlockSpec((tm, D_POOL), lambda i: (i, 0))),
        compiler_params=pltpu.CompilerParams(
            dimension_semantics=("arbitrary",), vmem_limit_bytes=VMEM_LIMIT),
        name="pre",
    )(x, mod, g.reshape(DEPTH, 1, D_MODEL), w_in_b,
      jnp.pad(mu, ((0, 0), (0, D_IN - SHIFT_END))).reshape(DEPTH, 1, D_IN))
    return outs, shifted


def _rwkv_kernel(seq, pp, shifted, has_state, want_state, alias_state, layer, *refs):
    (r_ref, k_ref, v_ref, lo_ref, mur_ref, muk_ref, muv_ref, mul_ref, w0_ref, wup_ref, a0_ref,
     aup_ref, gup_ref, kk_ref, ka_ref, rk_ref, lnw_ref, lnb_ref) = refs[:18]
    n_in = 18 + has_state + alias_state
    s0_ref = refs[18] if has_state else None
    y_ref = refs[n_in]
    sfin_ref = refs[n_in + 1] if want_state else None
    lw_s, kd_s, b_s, r_s, v_s, kn_s, st_s, rb_s, yb_s, gt_s, ht_s = refs[n_in + 1 + want_state:]
    nc = seq // CHUNK
    C = CHUNK
    tiles = [slice(i * LANES, (i + 1) * LANES) for i in range(pp)]

    lane1 = lax.broadcasted_iota(jnp.int32, (1, LANES), 1)
    head0 = lane1 < HEAD
    head1 = jnp.logical_not(head0)
    ri = lax.broadcasted_iota(jnp.int32, (LANES, LANES), 0)
    ci = lax.broadcasted_iota(jnp.int32, (LANES, LANES), 1)
    same_head = (ri // HEAD) == (ci // HEAD)
    seg_ones = same_head.astype(F32)
    eye = (ri == ci).astype(F32)

    def seg_sum(x):
        return jnp.concatenate([_mm(x[:, t], seg_ones) for t in tiles], axis=1)

    if shifted:
        r, k, v, lo = r_ref[0], k_ref[0], v_ref[0], lo_ref[0]
    else:
        r = _shift_mix(r_ref[0], mur_ref[...], seq)
        k = _shift_mix(k_ref[0], muk_ref[...], seq)
        v = _shift_mix(v_ref[0], muv_ref[...], seq)
        lo = _shift_mix(lo_ref[0], mul_ref[...], seq)
    wd = jnp.tanh(lo[:, 0:2 * LORA])
    ad = lo[:, 2 * LORA:4 * LORA]
    gd = _sigmoid(lo[:, 4 * LORA:])
    gate = _mm(gd, gup_ref[...])
    kx = k * kk_ref[...]
    kn = kx * lax.rsqrt(jnp.maximum(seg_sum(kx * kx), 1e-24))
    zeros_up = jnp.zeros((LORA, pp * LANES), F32)
    kd_sum = jnp.zeros_like(k)
    for d in range(2):
        if d == 0:
            wup = jnp.concatenate([wup_ref[0], zeros_up], axis=0)
            aup = jnp.concatenate([aup_ref[0], zeros_up], axis=0)
        else:
            wup = jnp.concatenate([zeros_up, wup_ref[1]], axis=0)
            aup = jnp.concatenate([zeros_up, aup_ref[1]], axis=0)
        zw = w0_ref[d:d + 1, :] + _mm(wd, wup)
        lw2 = (-LOG2E * math.exp(-0.5)) * _sigmoid(zw)
        lw_s[d] = lw2
        a_sig = _sigmoid(a0_ref[d:d + 1, :] + _mm(ad, aup))
        kd = k * (1.0 + (a_sig - 1.0) * ka_ref[...])
        kd_s[d] = kd
        b_s[d] = kn * a_sig
        kn_s[d] = -kn * jnp.exp2(-lw2)
        kd_sum = kd_sum + kd
    r_s[...] = r
    v_s[...] = v
    bonus = seg_sum(r * kd_sum * rk_ref[...]) * v

    if has_state:
        rj = lax.broadcasted_iota(jnp.int32, (LANES, HEAD), 0)
        cj = lax.broadcasted_iota(jnp.int32, (LANES, HEAD), 1)
        place = [(rj == cj + j * HEAD).astype(F32) for j in range(2)]
        keys = [(d, pi, j) for d in range(2) for pi in range(pp) for j in range(2)]
        t1 = {key: _dots(place[key[2]], s0_ref[0, key[0], 2 * key[1] + key[2]], 1, 3, nt=True)
              for key in keys}
        t2 = {key: _dots(t1[key], place[key[2]], 3, 1, nt=True) for key in keys}
        for d in range(2):
            for pi in range(pp):
                st_s[d, pi] = t2[(d, pi, 0)] + t2[(d, pi, 1)]
    else:
        st_s[...] = jnp.zeros(st_s.shape, F32)

    tr = lax.broadcasted_iota(jnp.int32, (C, C), 0)
    tc = lax.broadcasted_iota(jnp.int32, (C, C), 1)
    t_p = lax.broadcasted_iota(jnp.int32, (C, LANES), 0)
    s_p = lax.broadcasted_iota(jnp.int32, (C, LANES), 1) % C
    tris = ((tc <= tr).astype(F32), (tc >= tr).astype(F32))
    strict = (s_p < t_p, s_p > t_p)
    inclusive = (s_p <= t_p, s_p >= t_p)
    eye_p = (s_p == t_p).astype(F32)
    txs = t_p ^ s_p
    lvl_map = sum(((txs >= (1 << bit)).astype(jnp.int32) for bit in range(1, 6)),
                  jnp.where(txs == 0, -1, 0))
    zero_b = jnp.zeros((C, LANES), BF16)

    def bd(x):
        return jnp.concatenate([jnp.where(head0, x, zero_b), jnp.where(head1, x, zero_b)], axis=0)

    def bdx(x):
        return jnp.concatenate([jnp.where(head1, x, zero_b), jnp.where(head0, x, zero_b)], axis=0)

    def mm(a, b):
        return jnp.dot(a, b, preferred_element_type=F32)

    def rows(c):
        return pl.ds(c * C, C)

    def algebra(units):
        u = []
        for c, pi, d in units:
            sl, ln = rows(c), tiles[pi]
            lw = lw_s[d, sl, ln]
            u.append(dict(c=c, pi=pi, d=d, sl=sl, ln=ln, lw=lw, cl=_dots(tris[d], lw, 1, 2)))
        yield
        for q in u:
            d, sl, ln, cl = q["d"], q["sl"], q["ln"], q["cl"]
            tot = cl[C - 1:C, :] if d == 0 else cl[0:1, :]
            e_in = jnp.exp2(cl)
            e_inv = jnp.exp2(-cl)
            p_c = jnp.exp2(tot)
            q["at"] = (kn_s[d, sl, ln] * e_in).astype(BF16)
            q["rt"] = r_s[sl, ln] * e_in
            bt = b_s[d, sl, ln] * e_inv
            kt = kd_s[d, sl, ln] * e_inv
            q["lhs_t"] = jnp.concatenate([bt * p_c, kt * p_c], axis=0).T.astype(BF16)
            bt = bt.astype(BF16)
            kt = kt.astype(BF16)
            q["rhs_nt"] = jnp.concatenate([jnp.where(head0, bt, zero_b), jnp.where(head1, bt, zero_b),
                                           jnp.where(head1, kt, zero_b), jnp.where(head0, kt, zero_b)], axis=0)
            q["p_c"] = p_c
            q["vc"] = v_s[sl, ln].astype(BF16)
            yield
        for q in u:
            lhs = jnp.concatenate([q["at"], q["rt"].astype(BF16)], axis=0)
            q["sc"] = lax.dot_general(lhs, q["rhs_nt"], (((1,), (1,)), ((), ())), preferred_element_type=F32)
        yield
        for q in u:
            d, sc = q["d"], q["sc"]
            lab = jnp.where(strict[d], sc[:C, :LANES], 0.0)
            lakx = jnp.where(strict[d], sc[:C, LANES:], 0.0)
            q["mr"] = jnp.concatenate([jnp.where(inclusive[d], sc[C:, :LANES], 0.0),
                                       jnp.where(inclusive[d], sc[C:, LANES:], 0.0)], axis=1).astype(BF16)
            q["xv"] = mm(lakx.astype(BF16), bdx(q["vc"]))
            q["t"] = eye_p + jnp.where(lvl_map == 0, lab, 0.0)
            q["lab"] = lab
        yield
        for lvl in range(1, 6):
            for q in u:
                q["tb"] = q["t"].astype(BF16)
                q["w"] = mm(jnp.where(lvl_map == lvl, q["lab"], 0.0).astype(BF16), bd(q["tb"]))
            yield
            for q in u:
                q["t"] = q["t"] + mm(q["tb"], bd(q["w"].astype(BF16)))
            yield
        for q in u:
            x = mm(q["t"].astype(BF16),
                   jnp.concatenate([bd(q["at"]), bd(q["xv"].astype(BF16))], axis=1))
            q["ab"] = x[:, :LANES].astype(BF16)
            q["ub"] = x[:, LANES:].astype(BF16)
        yield
        for q in u:
            rhs = jnp.concatenate([jnp.concatenate([bd(q["ab"]), bd(q["ub"])], axis=1),
                                   jnp.concatenate([jnp.concatenate([zero_b, zero_b], axis=0), bdx(q["vc"])],
                                                   axis=1)], axis=0)
            q["o2"] = mm(q["mr"], rhs)
            rhs2 = jnp.concatenate([jnp.concatenate([q["ab"], q["ub"]], axis=1),
                                    jnp.concatenate([zero_b, q["vc"]], axis=1)], axis=0)
            q["gh"] = mm(q["lhs_t"], rhs2)
        yield
        for q in u:
            c, pi, d, sl, ln, o2, gh = q["c"], q["pi"], q["d"], q["sl"], q["ln"], q["o2"], q["gh"]
            rb_s[d, sl, ln] = (q["rt"] + o2[:, :LANES]).astype(BF16)
            yb_s[d, sl, ln] = o2[:, LANES:]
            gt_s[d, pi, c] = (jnp.where(same_head, gh[:, :LANES], 0.0) + eye * q["p_c"]).astype(BF16)
            ht_s[d, pi, c] = jnp.where(same_head, gh[:, LANES:], 0.0)
        yield

    def state_chain(g, first, count):
        while not (("algebra", g) in issued and (g == 0 or ("chain", g - 1) in issued)):
            yield
        for i in range(first, first + count):
            items = []
            for pi in range(pp):
                for d in range(2):
                    c = i if d == 0 else nc - 1 - i
                    sl, ln = rows(c), tiles[pi]
                    lhs = jnp.concatenate([rb_s[d, sl, ln], gt_s[d, pi, c]], axis=0)
                    items.append((pi, d, c, sl, ln, mm(lhs, st_s[d, pi].astype(BF16))))
            yield
            for pi, d, c, sl, ln, o in items:
                y_ref[0, sl, ln] += o[:C] + yb_s[d, sl, ln]
                st_s[d, pi] = o[C:] + ht_s[d, pi, c]
            yield
        issued.add(("chain", g))

    y_ref[0] = jnp.zeros(y_ref.shape[1:], F32)

    per = max(1, STREAM_UNITS // (2 * pp))
    issued = set()

    def tracked(g, units):
        yield from algebra(units)
        issued.add(("algebra", g))

    streams = []
    for grp in range(nc // per):
        units = [(i if d == 0 else nc - 1 - i, pi, d)
                 for i in range(grp * per, (grp + 1) * per) for pi in range(pp) for d in range(2)]
        streams.append((grp * STREAM_LAG, tracked(grp, units)))
        streams.append((grp * STREAM_LAG, state_chain(grp, grp * per, per)))
    _interleave(streams)

    if want_state:
        if alias_state:
            out = sfin_ref.at[0]
        else:
            out = sfin_ref.at[0, layer]
            for other in range(DEPTH):
                if other != layer:
                    sfin_ref[0, other] = jnp.zeros(sfin_ref.shape[2:], F32)
        for d in range(2):
            for pi in range(pp):
                stt = st_s[d, pi].T
                out[d, 2 * pi] = stt[:HEAD, :HEAD]
                out[d, 2 * pi + 1] = pltpu.roll(stt[HEAD:], HEAD, 1)[:, :HEAD]

    y = y_ref[0]
    yc = y - seg_sum(y) * (1.0 / HEAD)
    var = seg_sum(yc * yc) * (1.0 / HEAD)
    yn = yc * lax.rsqrt(var + GN_EPS) * lnw_ref[...] + lnb_ref[...]
    y_ref[0] = (yn + bonus) * gate


def _rwkv(pa, shifted, layer, s0, s_prev, want_state, pp, mu, w0, w_up, a0, a_up, g_up, k_k, k_a, r_k, ln_w,
          ln_b):
    bsz, seq, _ = pa.shape
    has_state = s0 is not None
    alias_state = want_state and s_prev is not None
    width = pp * LANES
    nblk = D_RWKV // width
    nc = seq // CHUNK
    mu3 = mu.reshape(DEPTH, 1, SHIFT_END)
    row = lambda a: a.reshape(DEPTH, 1, D_RWKV)
    col = lambda o: (lambda b, p: (b, 0, o * nblk + p))
    vec = lambda b, p: (layer, 0, p)
    lora_blk = 3 * D_RWKV // LORA_W
    in_specs = [
        pl.BlockSpec((1, seq, width), col(0)),
        pl.BlockSpec((1, seq, width), col(1)),
        pl.BlockSpec((1, seq, width), col(2)),
        pl.BlockSpec((1, seq, LORA_W), lambda b, p: (b, 0, lora_blk)),
        pl.BlockSpec((None, 1, width), lambda b, p: (layer, 0, p)),
        pl.BlockSpec((None, 1, width), lambda b, p: (layer, 0, nblk + p)),
        pl.BlockSpec((None, 1, width), lambda b, p: (layer, 0, 2 * nblk + p)),
        pl.BlockSpec((None, 1, LORA_W), lambda b, p: (layer, 0, lora_blk)),
        pl.BlockSpec((None, 2, width), vec),
        pl.BlockSpec((None, 2, LORA, width), lambda b, p: (layer, 0, 0, p)),
        pl.BlockSpec((None, 2, width), vec),
        pl.BlockSpec((None, 2, LORA, width), lambda b, p: (layer, 0, 0, p)),
        pl.BlockSpec((None, GATE_LORA, width), vec),
        pl.BlockSpec((None, 1, width), vec),
        pl.BlockSpec((None, 1, width), vec),
        pl.BlockSpec((None, 1, width), vec),
        pl.BlockSpec((None, 1, width), vec),
        pl.BlockSpec((None, 1, width), vec),
    ]
    args = [pa, pa, pa, pa, mu3, mu3, mu3, mu3, w0, w_up, a0, a_up, g_up, row(k_k), row(k_a),
            row(r_k), row(ln_w), row(ln_b)]
    state_spec = pl.BlockSpec((1, None, 2, 2 * pp, HEAD, HEAD), lambda b, p: (b, layer, 0, p, 0, 0))
    if has_state:
        in_specs.append(state_spec)
        args.append(s0)
    aliases = {}
    if alias_state:
        aliases[len(args)] = 1
        in_specs.append(pl.BlockSpec(memory_space=pl.ANY))
        args.append(s_prev)
    out_shape = [jax.ShapeDtypeStruct((bsz, seq, D_RWKV), F32)]
    out_specs = [pl.BlockSpec((1, seq, width), lambda b, p: (b, 0, p))]
    if want_state:
        out_shape.append(jax.ShapeDtypeStruct((bsz, DEPTH, 2, N_HEADS, HEAD, HEAD), F32))
        if alias_state:
            out_specs.append(state_spec)
        else:
            out_specs.append(pl.BlockSpec((1, DEPTH, 2, 2 * pp, HEAD, HEAD), lambda b, p: (b, 0, 0, p, 0, 0)))
    outs = pl.pallas_call(
        functools.partial(_rwkv_kernel, seq, pp, shifted, has_state, want_state, alias_state, layer),
        out_shape=tuple(out_shape),
        grid=(bsz, nblk),
        in_specs=in_specs,
        out_specs=tuple(out_specs),
        input_output_aliases=aliases,
        scratch_shapes=[
            pltpu.VMEM((2, seq, width), F32),
            pltpu.VMEM((2, seq, width), F32),
            pltpu.VMEM((2, seq, width), F32),
            pltpu.VMEM((seq, width), F32),
            pltpu.VMEM((seq, width), F32),
            pltpu.VMEM((2, seq, width), F32),
            pltpu.VMEM((2, pp, LANES, LANES), F32),
            pltpu.VMEM((2, seq, width), BF16),
            pltpu.VMEM((2, seq, width), F32),
            pltpu.VMEM((2, pp, nc, LANES, LANES), BF16),
            pltpu.VMEM((2, pp, nc, LANES, LANES), F32),
        ],
        compiler_params=pltpu.CompilerParams(
            dimension_semantics=("arbitrary", "arbitrary"), vmem_limit_bytes=VMEM_LIMIT),
        name="rwkv",
    )(*args)
    return (outs[0], outs[1]) if want_state else (outs[0], None)


def _fpool_kernel(seq, nb, f_ref, p_ref, cs_ref, c64_ref, s64_ref, wf_ref, band_ref, inv_ref, wp_ref,
                  ps_ref, yb_ref, yc_ref, cw_s, cs_s):
    @pl.when(pl.program_id(0) == 0)
    def _():
        wf = wf_ref[...]
        cw_s[:, :D_FNET] = _dotf(c64_ref[...], wf).astype(BF16)
        cw_s[:, D_FNET:] = (-_dotf(s64_ref[...], wf)).astype(BF16)
        cs_s[...] = cs_ref[...].astype(BF16)

    f = f_ref[...].reshape(nb * seq, D_FNET)
    g = jnp.dot(f.astype(BF16), cw_s[...], preferred_element_type=F32).astype(BF16)
    stacked = jnp.concatenate(
        [jnp.concatenate([g[i * seq:(i + 1) * seq, :D_FNET], g[i * seq:(i + 1) * seq, D_FNET:]], axis=0)
         for i in range(nb)], axis=1)
    yb = jnp.dot(cs_s[...], stacked, preferred_element_type=F32)
    for i in range(nb):
        yb_ref[i] = yb[:, i * D_FNET:(i + 1) * D_FNET]

    p = jnp.concatenate([p_ref[i] for i in range(nb)], axis=1)
    p_b = p.astype(BF16)
    group = (lax.broadcasted_iota(jnp.int32, p.shape, 1) % D_POOL) // POOL_GW
    d = jnp.zeros(p.shape, F32)
    for i in range(len(POOL_WINDOWS)):
        s = jnp.dot(band_ref[i], p_b, preferred_element_type=F32)
        d = jnp.where(group == i, s, d)
    d = d * jnp.concatenate([inv_ref[...]] * nb, axis=1) - p
    d = jnp.concatenate([d[:, i * D_POOL:(i + 1) * D_POOL] for i in range(nb)], axis=0)
    yc_ref[...] = (_mm(d, wp_ref[...]) * ps_ref[...]).reshape(nb, seq, D_POOL)


def _fpool(pf, pp, consts, wf_bd, wp_bd, pool_scale):
    bsz, seq, _ = pf.shape
    cs, c64, s64, band, inv = consts
    nb = max(1, min(bsz, 1024 // seq))
    full2 = lambda b: (0, 0)
    return pl.pallas_call(
        functools.partial(_fpool_kernel, seq, nb),
        out_shape=(jax.ShapeDtypeStruct((bsz, seq, D_FNET), F32),
                   jax.ShapeDtypeStruct((bsz, seq, D_POOL), F32)),
        grid=(bsz // nb,),
        in_specs=[
            pl.BlockSpec((nb, seq, D_FNET), lambda b: (b, 0, 0)),
            pl.BlockSpec((nb, seq, D_POOL), lambda b: (b, 0, 0)),
            pl.BlockSpec((seq, 2 * seq), full2),
            pl.BlockSpec((D_FNET, D_FNET), full2),
            pl.BlockSpec((D_FNET, D_FNET), full2),
            pl.BlockSpec((D_FNET, D_FNET), full2),
            pl.BlockSpec((len(POOL_WINDOWS), seq, seq), lambda b: (0, 0, 0)),
            pl.BlockSpec((seq, D_POOL), full2),
            pl.BlockSpec((D_POOL, D_POOL), full2),
            pl.BlockSpec((1, D_POOL), full2),
        ],
        out_specs=(pl.BlockSpec((nb, seq, D_FNET), lambda b: (b, 0, 0)),
                   pl.BlockSpec((nb, seq, D_POOL), lambda b: (b, 0, 0))),
        scratch_shapes=[pltpu.VMEM((D_FNET, 2 * D_FNET), BF16),
                        pltpu.VMEM((seq, 2 * seq), BF16)],
        compiler_params=pltpu.CompilerParams(
            dimension_semantics=("arbitrary",), vmem_limit_bytes=VMEM_LIMIT),
        name="fpool",
    )(pf, pp, cs, c64, s64, wf_bd, band, inv, wp_bd, pool_scale.reshape(1, D_POOL))


def _window_matrix(n, w):
    t = np.arange(n)
    lo = np.clip(t - w // 2, 0, n)
    hi = np.clip(t + w - w // 2, 0, n)
    s = np.arange(n)
    return ((s[None, :] >= lo[:, None]) & (s[None, :] < hi[:, None])).astype(np.float64)


def _mixer_constants(seq, grid):
    n = np.arange(seq)
    ang = 2.0 * np.pi * ((n[:, None] * n[None, :]) % seq) / seq
    scale = 1.0 / math.sqrt(seq * FNET_GW)
    cs = np.concatenate([np.cos(ang), np.sin(ang)], axis=1) * scale
    m = np.arange(FNET_GW)
    ang64 = 2.0 * np.pi * ((m[:, None] * m[None, :]) % FNET_GW) / FNET_GW
    groups = D_FNET // FNET_GW
    c64 = np.kron(np.eye(groups), np.cos(ang64))
    s64 = np.kron(np.eye(groups), np.sin(ang64))
    bands = []
    invs = []
    for w in POOL_WINDOWS:
        if grid:
            rows = seq // GRID_W
            band = np.kron(_window_matrix(rows, w), _window_matrix(GRID_W, w))
        else:
            band = _window_matrix(seq, w)
        bands.append(band)
        invs.append(np.repeat((1.0 / band.sum(axis=1))[:, None], POOL_GW, axis=1))
    band = jnp.asarray(np.stack(bands), dtype=BF16)
    inv = jnp.asarray(np.concatenate(invs, axis=1), dtype=F32)
    return (jnp.asarray(cs, dtype=F32), jnp.asarray(c64, dtype=F32), jnp.asarray(s64, dtype=F32),
            band, inv)


def _block_diag(w):
    g, n, _ = w.shape
    return (w[:, :, None, :] * jnp.eye(g, dtype=w.dtype)[:, None, :, None]).reshape(g * n, g * n)


def _post_kernel(final, x_ref, ya_ref, yb_ref, yc_ref, mod_ref, wo_ref, g2_ref, w1_ref, w2_ref,
                 fg_ref, o_ref):
    mod = mod_ref[0]
    gate1 = mod[:, 2 * D_MODEL:3 * D_MODEL]
    shift2 = mod[:, 3 * D_MODEL:4 * D_MODEL]
    scale2 = mod[:, 4 * D_MODEL:5 * D_MODEL]
    gate2 = mod[:, 5 * D_MODEL:]
    mix = (jnp.dot(ya_ref[...].astype(BF16), wo_ref[0:D_RWKV, :], preferred_element_type=F32)
           + jnp.dot(yb_ref[...].astype(BF16), wo_ref[D_RWKV:D_RWKV + D_FNET, :],
                     preferred_element_type=F32)
           + jnp.dot(yc_ref[...].astype(BF16), wo_ref[D_RWKV + D_FNET:, :],
                     preferred_element_type=F32))
    x1 = x_ref[...] + gate1 * mix
    h2 = (_rms(x1, g2_ref[...]) * (1.0 + scale2) + shift2).astype(BF16)
    tf = 1024
    acc = jnp.zeros_like(x1)
    for j in range(D_FF // tf):
        ff = jnp.dot(h2, w1_ref[:, j * tf:(j + 1) * tf], preferred_element_type=F32)
        ff = jnp.square(jnp.maximum(ff, 0.0))
        acc = acc + jnp.dot(ff.astype(BF16), w2_ref[j * tf:(j + 1) * tf, :],
                            preferred_element_type=F32)
    x2 = x1 + gate2 * acc
    if final:
        x2 = _rms(x2, fg_ref[...])
    o_ref[...] = x2


def _post(x, ya, yb, yc, mod, mod_row, layer, w_out_b, g2, w1_b, w2_b, fg, final, tm):
    t = x.shape[0]
    tok = lambda i: (i, 0)
    full = lambda i: (0, 0)
    return pl.pallas_call(
        functools.partial(_post_kernel, final),
        out_shape=jax.ShapeDtypeStruct((t, D_MODEL), F32),
        grid=(t // tm,),
        in_specs=[
            pl.BlockSpec((tm, D_MODEL), tok),
            pl.BlockSpec((tm, D_RWKV), tok),
            pl.BlockSpec((tm, D_FNET), tok),
            pl.BlockSpec((tm, D_POOL), tok),
            pl.BlockSpec((None, 1, 1, 6 * D_MODEL), lambda i: (layer, mod_row(i), 0, 0)),
            _resident((D_MODEL, D_MODEL), layer),
            _resident((1, D_MODEL), layer),
            _resident((D_MODEL, D_FF), layer),
            _resident((D_FF, D_MODEL), layer),
            pl.BlockSpec((1, D_MODEL), full),
        ],
        out_specs=pl.BlockSpec((tm, D_MODEL), tok),
        compiler_params=pltpu.CompilerParams(
            dimension_semantics=("arbitrary",), vmem_limit_bytes=VMEM_LIMIT),
        name="post",
    )(x, ya, yb, yc, mod, w_out_b, g2.reshape(DEPTH, 1, D_MODEL), w1_b, w2_b, fg.reshape(1, D_MODEL))


def _layer(x, bsz, seq, layer, mod, mod_rows, s0, s_prev, want_state, pairs, consts, sw, mixw, fg, tm):
    (norm1_g, w_in_b, mu_shift, w0, w_up, a0, a_up, g_up, k_k, k_a, r_k, ln_w, ln_b, w_out_b, norm2_g,
     w1_b, w2_b) = sw
    wf_bd, wp_bd, pool_scale = mixw
    tm_pre = max(tm, seq)
    mod_row = mod_rows(tm)
    (pa, pf, pp), shifted = _pre(x, mod, mod_rows(tm_pre), layer, norm1_g, w_in_b, mu_shift, seq, tm_pre)
    ya, s_fin = _rwkv(pa.reshape(bsz, seq, SHIFT_END), shifted, layer, s0, s_prev, want_state, pairs,
                      mu_shift, w0, w_up, a0, a_up, g_up, k_k, k_a, r_k, ln_w, ln_b)
    yb, yc = _fpool(pf.reshape(bsz, seq, D_FNET), pp.reshape(bsz, seq, D_POOL), consts, wf_bd,
                    wp_bd, pool_scale)
    x = _post(x, ya.reshape(bsz * seq, D_RWKV), yb.reshape(bsz * seq, D_FNET),
              yc.reshape(bsz * seq, D_POOL), mod, mod_row, layer, w_out_b, norm2_g, w1_b, w2_b, fg,
              layer == DEPTH - 1, tm)
    return x, s_fin


def kernel(x_prompt, x_sample, state_wkv, c, c_ctx, w_ada, b_ada, norm1_g, w_in, mu_shift, w0, w_up, a0, a_up, g_up, k_k, k_a, r_k, ln_x_w, ln_x_b, w_fnet, w_pool, pool_scale, w_out, norm2_g, w_ff1, w_ff2, final_norm_g):
    bp, lp, _ = x_prompt.shape
    bs, ls, _ = x_sample.shape
    tm = 512
    n_rows = 8
    cond = jnp.concatenate([c_ctx[None], c, jnp.zeros((n_rows - 1 - bs, D_MODEL), F32)], axis=0)
    mod_all = _ada(cond, w_ada, b_ada)
    consts_p = _mixer_constants(lp, False)
    consts_s = _mixer_constants(ls, True)
    xp = x_prompt.reshape(bp * lp, D_MODEL)
    xs = x_sample.reshape(bs * ls, D_MODEL)
    row_p = lambda tile: (lambda i: 0)
    row_s = lambda tile: (lambda i: 1 + (i * tile) // ls)
    sw = (norm1_g, w_in.astype(BF16), mu_shift, w0, w_up, a0, a_up, g_up, k_k, k_a, r_k, ln_x_w, ln_x_b,
          w_out.astype(BF16), norm2_g, w_ff1.astype(BF16), w_ff2.astype(BF16))
    mod = mod_all.reshape(DEPTH, n_rows, 1, 6 * D_MODEL)
    new_state_wkv = None
    for l in range(DEPTH):
        mixw = (_block_diag(w_fnet[l]), _block_diag(w_pool[l]), pool_scale[l])
        xp, new_state_wkv = _layer(xp, bp, lp, l, mod, row_p, None, new_state_wkv, True, 4, consts_p, sw,
                                   mixw, final_norm_g, tm)
        xs, _ = _layer(xs, bs, ls, l, mod, row_s, state_wkv, None, False, 2, consts_s, sw, mixw,
                       final_norm_g, tm)
    return (xp.reshape(bp, lp, D_MODEL), xs.reshape(bs, ls, D_MODEL), new_state_wkv)
```

```python
import functools
import math

import numpy as np
import jax
import jax.numpy as jnp
from jax import lax
from jax.experimental import pallas as pl
from jax.experimental.pallas import tpu as pltpu

F32 = jnp.float32
BF16 = jnp.bfloat16

D_MODEL = 1024
DEPTH = 2
GRID_W = 64
D_RWKV = 512
HEAD = 64
N_HEADS = 8
LANES = 128
D_FNET = 256
FNET_GW = 64
D_POOL = 256
POOL_GW = 64
POOL_WINDOWS = (2, 4, 8, 16)
LORA = 64
GATE_LORA = 128
SHIFT_END = 3 * D_RWKV + 2 * LORA + 2 * LORA + GATE_LORA
LORA_W = SHIFT_END - 3 * D_RWKV
D_IN = SHIFT_END + D_FNET + D_POOL
D_FF = 4 * D_MODEL
RMS_EPS = 1e-6
GN_EPS = 64e-5
LOG2E = 1.4426950408889634
CHUNK = 64
VMEM_LIMIT = 56 * 1024 * 1024
STREAM_UNITS = 16
STREAM_LAG = 8
PRE_CHUNK = 512


def _dotf(a, b):
    return jnp.dot(a, b, preferred_element_type=F32, precision=lax.Precision.HIGHEST)


def _split(x, n):
    terms = []
    for _ in range(n - 1):
        hi = x.astype(BF16)
        terms.append(hi)
        x = x - hi.astype(F32)
    terms.append(x.astype(BF16))
    return terms


def _dots(a, b, na, nb, nt=False):
    dims = (((1,), (1,)), ((), ())) if nt else (((1,), (0,)), ((), ()))
    ta = _split(a, na)
    tb = _split(b, nb)
    out = None
    for i in range(na):
        for j in range(nb):
            if i + j < max(na, nb):
                t = lax.dot_general(ta[i], tb[j], dims, preferred_element_type=F32)
                out = t if out is None else out + t
    return out


def _mm(a, b):
    return jnp.dot(a.astype(BF16), b.astype(BF16), preferred_element_type=F32)


def _sigmoid(x):
    return 0.5 + 0.5 * jnp.tanh(0.5 * x)


def _rms(x, g):
    ms = jnp.mean(x * x, axis=-1, keepdims=True)
    return x * lax.rsqrt(ms + RMS_EPS) * g


_DONE = object()


def _interleave(streams):
    pending = sorted(streams, key=lambda s: s[0])
    active = []
    step = 0
    while pending or active:
        while pending and pending[0][0] <= step:
            active.append(pending.pop(0)[1])
        for s in list(active):
            if next(s, _DONE) is _DONE:
                active.remove(s)
        step += 1


def _ada_kernel(c_ref, w_ref, b_ref, o_ref):
    c = c_ref[...]
    s = c * _sigmoid(c)
    o_ref[0] = _dots(s, w_ref[0], 2, 2) + b_ref[0]


def _ada(cond, w_ada, b_ada):
    tn = 1536
    nrow = cond.shape[0]
    return pl.pallas_call(
        _ada_kernel,
        out_shape=jax.ShapeDtypeStruct((DEPTH, nrow, 6 * D_MODEL), F32),
        grid=(DEPTH, 6 * D_MODEL // tn),
        in_specs=[
            pl.BlockSpec((nrow, D_MODEL), lambda l, j: (0, 0)),
            pl.BlockSpec((1, D_MODEL, tn), lambda l, j: (l, 0, j)),
            pl.BlockSpec((1, 1, tn), lambda l, j: (l, 0, j)),
        ],
        out_specs=pl.BlockSpec((1, nrow, tn), lambda l, j: (l, 0, j)),
        compiler_params=pltpu.CompilerParams(
            dimension_semantics=("arbitrary", "arbitrary"), vmem_limit_bytes=VMEM_LIMIT),
        name="ada",
    )(cond, w_ada, b_ada.reshape(DEPTH, 1, 6 * D_MODEL))


def _shift_mix(p, mu, period):
    n = p.shape[0]
    pos = lax.broadcasted_iota(jnp.int32, p.shape, 0) % period
    prev = jnp.where(pos == 0, 0.0, pltpu.roll(p, 1, 0))
    nxt = jnp.where(pos == period - 1, 0.0, pltpu.roll(p, n - 1, 0))
    return p * (1.0 - mu) + (0.5 * mu) * (prev + nxt)


def _pre_kernel(seq, x_ref, mod_ref, g_ref, w_ref, mu_ref, oa_ref, of_ref, op_ref):
    mod = mod_ref[0]
    shift = mod[:, 0:D_MODEL]
    scale = mod[:, D_MODEL:2 * D_MODEL]
    h = (_rms(x_ref[...], g_ref[...]) * (1.0 + scale) + shift).astype(BF16)
    outs = ((oa_ref, 0, SHIFT_END), (of_ref, SHIFT_END, SHIFT_END + D_FNET), (op_ref, SHIFT_END + D_FNET, D_IN))
    for c0 in range(0, D_IN, PRE_CHUNK):
        c1 = min(c0 + PRE_CHUNK, D_IN)
        p = jnp.dot(h, w_ref[:, c0:c1], preferred_element_type=F32)
        if c0 < SHIFT_END:
            p = _shift_mix(p, mu_ref[:, c0:c1], seq)
        for ref, o0, o1 in outs:
            lo, hi = max(c0, o0), min(c1, o1)
            if lo < hi:
                ref[:, lo - o0:hi - o0] = p[:, lo - c0:hi - c0]


def _resident(shape, layer):
    zeros = (0,) * len(shape)
    return pl.BlockSpec((None,) + shape, lambda *_: (layer,) + zeros, pipeline_mode=pl.Buffered(1))


def _pre(x, mod, mod_row, layer, g, w_in_b, mu, seq, tm):
    t = x.shape[0]
    assert tm % seq == 0, "a pre tile must hold whole sequences (token shift)"
    return pl.pallas_call(
        functools.partial(_pre_kernel, seq),
        out_shape=(jax.ShapeDtypeStruct((t, SHIFT_END), F32),
                   jax.ShapeDtypeStruct((t, D_FNET), F32),
                   jax.ShapeDtypeStruct((t, D_POOL), F32)),
        grid=(t // tm,),
        in_specs=[
            pl.BlockSpec((tm, D_MODEL), lambda i: (i, 0)),
            pl.BlockSpec((None, 1, 1, 6 * D_MODEL), lambda i: (layer, mod_row(i), 0, 0)),
            _resident((1, D_MODEL), layer),
            _resident((D_MODEL, D_IN), layer),
            _resident((1, D_IN), layer),
        ],
        out_specs=(pl.BlockSpec((tm, SHIFT_END), lambda i: (i, 0)),
                   pl.BlockSpec((tm, D_FNET), lambda i: (i, 0)),
                   pl.BlockSpec((tm, D_POOL), lambda i: (i, 0))),
        compiler_params=pltpu.CompilerParams(
            dimension_semantics=("arbitrary",), vmem_limit_bytes=VMEM_LIMIT),
        name="pre",
    )(x, mod, g.reshape(DEPTH, 1, D_MODEL), w_in_b,
      jnp.pad(mu, ((0, 0), (0, D_IN - SHIFT_END))).reshape(DEPTH, 1, D_IN))


def _rwkv_kernel(seq, pp, nb, has_state, want_state, alias_state, layer, *refs):
    (r_ref, k_ref, v_ref, lo_ref, w0_ref, wup_ref, a0_ref, aup_ref, gup_ref, kk_ref, ka_ref, rk_ref,
     lnw_ref, lnb_ref) = refs[:14]
    n_in = 14 + has_state + alias_state
    s0_ref = refs[14] if has_state else None
    y_ref = refs[n_in]
    sfin_ref = refs[n_in + 1] if want_state else None
    lw_s, kd_s, b_s, r_s, v_s, kn_s, st_s, rb_s, yb_s, gt_s, ht_s = refs[n_in + 1 + want_state:]
    nc = seq // CHUNK
    C = CHUNK
    tiles = [slice(i * LANES, (i + 1) * LANES) for i in range(pp)]

    lane1 = lax.broadcasted_iota(jnp.int32, (1, LANES), 1)
    head0 = lane1 < HEAD
    head1 = jnp.logical_not(head0)
    ri = lax.broadcasted_iota(jnp.int32, (LANES, LANES), 0)
    ci = lax.broadcasted_iota(jnp.int32, (LANES, LANES), 1)
    same_head = (ri // HEAD) == (ci // HEAD)
    seg_ones = same_head.astype(F32)
    eye = (ri == ci).astype(F32)

    def seg_sum(x):
        return jnp.concatenate([_mm(x[:, t], seg_ones) for t in tiles], axis=1)

    n_rows = nb * seq
    r = r_ref[...].reshape(n_rows, pp * LANES)
    k = k_ref[...].reshape(n_rows, pp * LANES)
    v = v_ref[...].reshape(n_rows, pp * LANES)
    lo = lo_ref[...].reshape(n_rows, LORA_W)
    wd = jnp.tanh(lo[:, 0:2 * LORA])
    ad = lo[:, 2 * LORA:4 * LORA]
    gd = _sigmoid(lo[:, 4 * LORA:])
    gate = _mm(gd, gup_ref[...])
    kx = k * kk_ref[...]
    kn = kx * lax.rsqrt(jnp.maximum(seg_sum(kx * kx), 1e-24))
    zeros_up = jnp.zeros((LORA, pp * LANES), F32)
    kd_sum = jnp.zeros_like(k)
    for d in range(2):
        if d == 0:
            wup = jnp.concatenate([wup_ref[0], zeros_up], axis=0)
            aup = jnp.concatenate([aup_ref[0], zeros_up], axis=0)
        else:
            wup = jnp.concatenate([zeros_up, wup_ref[1]], axis=0)
            aup = jnp.concatenate([zeros_up, aup_ref[1]], axis=0)
        zw = w0_ref[d:d + 1, :] + _mm(wd, wup)
        lw2 = (-LOG2E * math.exp(-0.5)) * _sigmoid(zw)
        lw_s[d] = lw2
        a_sig = _sigmoid(a0_ref[d:d + 1, :] + _mm(ad, aup))
        kd = k * (1.0 + (a_sig - 1.0) * ka_ref[...])
        kd_s[d] = kd
        b_s[d] = kn * a_sig
        kn_s[d] = -kn * jnp.exp2(-lw2)
        kd_sum = kd_sum + kd
    r_s[...] = r
    v_s[...] = v
    bonus = seg_sum(r * kd_sum * rk_ref[...]) * v

    if has_state:
        rj = lax.broadcasted_iota(jnp.int32, (LANES, HEAD), 0)
        cj = lax.broadcasted_iota(jnp.int32, (LANES, HEAD), 1)
        place = [(rj == cj + j * HEAD).astype(F32) for j in range(2)]
        keys = [(e, d, pi, j) for e in range(nb) for d in range(2) for pi in range(pp) for j in range(2)]
        t1 = {key: _dots(place[key[3]], s0_ref[key[0], key[1], 2 * key[2] + key[3]], 1, 3, nt=True)
              for key in keys}
        t2 = {key: _dots(t1[key], place[key[3]], 3, 1, nt=True) for key in keys}
        for e in range(nb):
            for d in range(2):
                for pi in range(pp):
                    st_s[d, e * pp + pi] = t2[(e, d, pi, 0)] + t2[(e, d, pi, 1)]
    else:
        st_s[...] = jnp.zeros(st_s.shape, F32)

    tr = lax.broadcasted_iota(jnp.int32, (C, C), 0)
    tc = lax.broadcasted_iota(jnp.int32, (C, C), 1)
    t_p = lax.broadcasted_iota(jnp.int32, (C, LANES), 0)
    s_p = lax.broadcasted_iota(jnp.int32, (C, LANES), 1) % C
    tris = ((tc <= tr).astype(F32), (tc >= tr).astype(F32))
    strict = (s_p < t_p, s_p > t_p)
    inclusive = (s_p <= t_p, s_p >= t_p)
    eye_p = (s_p == t_p).astype(F32)
    txs = t_p ^ s_p
    lvl_map = sum(((txs >= (1 << bit)).astype(jnp.int32) for bit in range(1, 6)),
                  jnp.where(txs == 0, -1, 0))
    zero_b = jnp.zeros((C, LANES), BF16)

    def bd(x):
        return jnp.concatenate([jnp.where(head0, x, zero_b), jnp.where(head1, x, zero_b)], axis=0)

    def bdx(x):
        return jnp.concatenate([jnp.where(head1, x, zero_b), jnp.where(head0, x, zero_b)], axis=0)

    def mm(a, b):
        return jnp.dot(a, b, preferred_element_type=F32)

    def rows(c):
        return pl.ds(c * C, C)

    def algebra(units):
        u = []
        for c, pi, d in units:
            sl, ln = rows(c), tiles[pi]
            lw = lw_s[d, sl, ln]
            u.append(dict(c=c, pi=pi, d=d, sl=sl, ln=ln, lw=lw, cl=_dots(tris[d], lw, 1, 2)))
        yield
        for q in u:
            d, sl, ln, cl = q["d"], q["sl"], q["ln"], q["cl"]
            tot = cl[C - 1:C, :] if d == 0 else cl[0:1, :]
            e_in = jnp.exp2(cl)
            e_inv = jnp.exp2(-cl)
            p_c = jnp.exp2(tot)
            q["at"] = (kn_s[d, sl, ln] * e_in).astype(BF16)
            q["rt"] = r_s[sl, ln] * e_in
            bt = b_s[d, sl, ln] * e_inv
            kt = kd_s[d, sl, ln] * e_inv
            q["lhs_t"] = jnp.concatenate([bt * p_c, kt * p_c], axis=0).T.astype(BF16)
            bt = bt.astype(BF16)
            kt = kt.astype(BF16)
            q["rhs_nt"] = jnp.concatenate([jnp.where(head0, bt, zero_b), jnp.where(head1, bt, zero_b),
                                           jnp.where(head1, kt, zero_b), jnp.where(head0, kt, zero_b)], axis=0)
            q["p_c"] = p_c
            q["vc"] = v_s[sl, ln].astype(BF16)
            yield
        for q in u:
            lhs = jnp.concatenate([q["at"], q["rt"].astype(BF16)], axis=0)
            q["sc"] = lax.dot_general(lhs, q["rhs_nt"], (((1,), (1,)), ((), ())), preferred_element_type=F32)
        yield
        for q in u:
            d, sc = q["d"], q["sc"]
            lab = jnp.where(strict[d], sc[:C, :LANES], 0.0)
            lakx = jnp.where(strict[d], sc[:C, LANES:], 0.0)
            q["mr"] = jnp.concatenate([jnp.where(inclusive[d], sc[C:, :LANES], 0.0),
                                       jnp.where(inclusive[d], sc[C:, LANES:], 0.0)], axis=1).astype(BF16)
            q["xv"] = mm(lakx.astype(BF16), bdx(q["vc"]))
            q["t"] = eye_p + jnp.where(lvl_map == 0, lab, 0.0)
            q["lab"] = lab
        yield
        for lvl in range(1, 6):
            for q in u:
                q["tb"] = q["t"].astype(BF16)
                q["w"] = mm(jnp.where(lvl_map == lvl, q["lab"], 0.0).astype(BF16), bd(q["tb"]))
            yield
            for q in u:
                q["t"] = q["t"] + mm(q["tb"], bd(q["w"].astype(BF16)))
            yield
        for q in u:
            x = mm(q["t"].astype(BF16),
                   jnp.concatenate([bd(q["at"]), bd(q["xv"].astype(BF16))], axis=1))
            q["ab"] = x[:, :LANES].astype(BF16)
            q["ub"] = x[:, LANES:].astype(BF16)
        yield
        for q in u:
            rhs = jnp.concatenate([jnp.concatenate([bd(q["ab"]), bd(q["ub"])], axis=1),
                                   jnp.concatenate([jnp.concatenate([zero_b, zero_b], axis=0), bdx(q["vc"])],
                                                   axis=1)], axis=0)
            q["o2"] = mm(q["mr"], rhs)
            rhs2 = jnp.concatenate([jnp.concatenate([q["ab"], q["ub"]], axis=1),
                                    jnp.concatenate([zero_b, q["vc"]], axis=1)], axis=0)
            q["gh"] = mm(q["lhs_t"], rhs2)
        yield
        for q in u:
            c, pi, d, sl, ln, o2, gh = q["c"], q["pi"], q["d"], q["sl"], q["ln"], q["o2"], q["gh"]
            rb_s[d, sl, ln] = (q["rt"] + o2[:, :LANES]).astype(BF16)
            yb_s[d, sl, ln] = o2[:, LANES:]
            gt_s[d, pi, c] = (jnp.where(same_head, gh[:, :LANES], 0.0) + eye * q["p_c"]).astype(BF16)
            ht_s[d, pi, c] = jnp.where(same_head, gh[:, LANES:], 0.0)
        yield

    def state_chain(e, g, first, count):
        while not (("algebra", e, g) in issued and (g == 0 or ("chain", e, g - 1) in issued)):
            yield
        for i in range(first, first + count):
            items = []
            for pi in range(pp):
                for d in range(2):
                    c = i if d == 0 else nc - 1 - i
                    cg, ln = e * nc + c, tiles[pi]
                    lhs = jnp.concatenate([rb_s[d, rows(cg), ln], gt_s[d, pi, cg]], axis=0)
                    items.append((pi, d, c, cg, ln, mm(lhs, st_s[d, e * pp + pi].astype(BF16))))
            yield
            for pi, d, c, cg, ln, o in items:
                y_ref[e, rows(c), ln] += o[:C] + yb_s[d, rows(cg), ln]
                st_s[d, e * pp + pi] = o[C:] + ht_s[d, pi, cg]
            yield
        issued.add(("chain", e, g))

    y_ref[...] = jnp.zeros(y_ref.shape, F32)

    per = max(1, STREAM_UNITS // (2 * pp))
    issued = set()

    def tracked(e, g, units):
        yield from algebra(units)
        issued.add(("algebra", e, g))

    streams = []
    for e in range(nb):
        for grp in range(nc // per):
            units = [(e * nc + (i if d == 0 else nc - 1 - i), pi, d)
                     for i in range(grp * per, (grp + 1) * per) for pi in range(pp) for d in range(2)]
            start = len(streams) // 2 * STREAM_LAG
            streams.append((start, tracked(e, grp, units)))
            streams.append((start, state_chain(e, grp, grp * per, per)))
    _interleave(streams)

    if want_state:
        for e in range(nb):
            if alias_state:
                out = sfin_ref.at[e]
            else:
                out = sfin_ref.at[e, layer]
                for other in range(DEPTH):
                    if other != layer:
                        sfin_ref[e, other] = jnp.zeros(sfin_ref.shape[2:], F32)
            for d in range(2):
                for pi in range(pp):
                    stt = st_s[d, e * pp + pi].T
                    out[d, 2 * pi] = stt[:HEAD, :HEAD]
                    out[d, 2 * pi + 1] = pltpu.roll(stt[HEAD:], HEAD, 1)[:, :HEAD]

    y = y_ref[...].reshape(n_rows, pp * LANES)
    yc = y - seg_sum(y) * (1.0 / HEAD)
    var = seg_sum(yc * yc) * (1.0 / HEAD)
    yn = yc * lax.rsqrt(var + GN_EPS) * lnw_ref[...] + lnb_ref[...]
    y_ref[...] = ((yn + bonus) * gate).reshape(y_ref.shape)


def _rwkv(pa, layer, s0, s_prev, want_state, pp, nb, w0, w_up, a0, a_up, g_up, k_k, k_a, r_k, ln_w, ln_b):
    bsz, seq, _ = pa.shape
    has_state = s0 is not None
    alias_state = want_state and s_prev is not None
    width = pp * LANES
    nblk = D_RWKV // width
    nc = seq // CHUNK
    rows = nb * seq
    row = lambda a: a.reshape(DEPTH, 1, D_RWKV)
    col = lambda o: (lambda b, p: (b, 0, o * nblk + p))
    vec = lambda b, p: (layer, 0, p)
    lora_blk = 3 * D_RWKV // LORA_W
    in_specs = [
        pl.BlockSpec((nb, seq, width), col(0)),
        pl.BlockSpec((nb, seq, width), col(1)),
        pl.BlockSpec((nb, seq, width), col(2)),
        pl.BlockSpec((nb, seq, LORA_W), lambda b, p: (b, 0, lora_blk)),
        pl.BlockSpec((None, 2, width), vec),
        pl.BlockSpec((None, 2, LORA, width), lambda b, p: (layer, 0, 0, p)),
        pl.BlockSpec((None, 2, width), vec),
        pl.BlockSpec((None, 2, LORA, width), lambda b, p: (layer, 0, 0, p)),
        pl.BlockSpec((None, GATE_LORA, width), vec),
        pl.BlockSpec((None, 1, width), vec),
        pl.BlockSpec((None, 1, width), vec),
        pl.BlockSpec((None, 1, width), vec),
        pl.BlockSpec((None, 1, width), vec),
        pl.BlockSpec((None, 1, width), vec),
    ]
    args = [pa, pa, pa, pa, w0, w_up, a0, a_up, g_up, row(k_k), row(k_a), row(r_k), row(ln_w), row(ln_b)]
    state_spec = pl.BlockSpec((nb, None, 2, 2 * pp, HEAD, HEAD), lambda b, p: (b, layer, 0, p, 0, 0))
    if has_state:
        in_specs.append(state_spec)
        args.append(s0)
    aliases = {}
    if alias_state:
        aliases[len(args)] = 1
        in_specs.append(pl.BlockSpec(memory_space=pl.ANY))
        args.append(s_prev)
    out_shape = [jax.ShapeDtypeStruct((bsz, seq, D_RWKV), F32)]
    out_specs = [pl.BlockSpec((nb, seq, width), lambda b, p: (b, 0, p))]
    if want_state:
        out_shape.append(jax.ShapeDtypeStruct((bsz, DEPTH, 2, N_HEADS, HEAD, HEAD), F32))
        if alias_state:
            out_specs.append(state_spec)
        else:
            out_specs.append(pl.BlockSpec((nb, DEPTH, 2, 2 * pp, HEAD, HEAD), lambda b, p: (b, 0, 0, p, 0, 0)))
    outs = pl.pallas_call(
        functools.partial(_rwkv_kernel, seq, pp, nb, has_state, want_state, alias_state, layer),
        out_shape=tuple(out_shape),
        grid=(bsz // nb, nblk),
        in_specs=in_specs,
        out_specs=tuple(out_specs),
        input_output_aliases=aliases,
        scratch_shapes=[
            pltpu.VMEM((2, rows, width), F32),
            pltpu.VMEM((2, rows, width), F32),
            pltpu.VMEM((2, rows, width), F32),
            pltpu.VMEM((rows, width), F32),
            pltpu.VMEM((rows, width), F32),
            pltpu.VMEM((2, rows, width), F32),
            pltpu.VMEM((2, nb * pp, LANES, LANES), F32),
            pltpu.VMEM((2, rows, width), BF16),
            pltpu.VMEM((2, rows, width), F32),
            pltpu.VMEM((2, pp, nb * nc, LANES, LANES), BF16),
            pltpu.VMEM((2, pp, nb * nc, LANES, LANES), F32),
        ],
        compiler_params=pltpu.CompilerParams(
            dimension_semantics=("arbitrary", "arbitrary"), vmem_limit_bytes=VMEM_LIMIT),
        name="rwkv",
    )(*args)
    return (outs[0], outs[1]) if want_state else (outs[0], None)


def _fpool_kernel(seq, nb, f_ref, p_ref, cs_ref, c64_ref, s64_ref, wf_ref, band_ref, inv_ref, wp_ref,
                  ps_ref, yb_ref, yc_ref, cw_s, cs_s):
    @pl.when(pl.program_id(0) == 0)
    def _():
        wf = wf_ref[...]
        cw_s[:, :D_FNET] = _dotf(c64_ref[...], wf).astype(BF16)
        cw_s[:, D_FNET:] = (-_dotf(s64_ref[...], wf)).astype(BF16)
        cs_s[...] = cs_ref[...].astype(BF16)

    f = f_ref[...].reshape(nb * seq, D_FNET)
    g = jnp.dot(f.astype(BF16), cw_s[...], preferred_element_type=F32).astype(BF16)
    stacked = jnp.concatenate(
        [jnp.concatenate([g[i * seq:(i + 1) * seq, :D_FNET], g[i * seq:(i + 1) * seq, D_FNET:]], axis=0)
         for i in range(nb)], axis=1)
    yb = jnp.dot(cs_s[...], stacked, preferred_element_type=F32)
    for i in range(nb):
        yb_ref[i] = yb[:, i * D_FNET:(i + 1) * D_FNET]

    p = jnp.concatenate([p_ref[i] for i in range(nb)], axis=1)
    p_b = p.astype(BF16)
    group = (lax.broadcasted_iota(jnp.int32, p.shape, 1) % D_POOL) // POOL_GW
    d = jnp.zeros(p.shape, F32)
    for i in range(len(POOL_WINDOWS)):
        s = jnp.dot(band_ref[i], p_b, preferred_element_type=F32)
        d = jnp.where(group == i, s, d)
    d = d * jnp.concatenate([inv_ref[...]] * nb, axis=1) - p
    d = jnp.concatenate([d[:, i * D_POOL:(i + 1) * D_POOL] for i in range(nb)], axis=0)
    yc_ref[...] = (_mm(d, wp_ref[...]) * ps_ref[...]).reshape(nb, seq, D_POOL)


def _fpool(pf, pp, consts, wf_bd, wp_bd, pool_scale):
    bsz, seq, _ = pf.shape
    cs, c64, s64, band, inv = consts
    nb = max(1, min(bsz, 1024 // seq))
    full2 = lambda b: (0, 0)
    return pl.pallas_call(
        functools.partial(_fpool_kernel, seq, nb),
        out_shape=(jax.ShapeDtypeStruct((bsz, seq, D_FNET), F32),
                   jax.ShapeDtypeStruct((bsz, seq, D_POOL), F32)),
        grid=(bsz // nb,),
        in_specs=[
            pl.BlockSpec((nb, seq, D_FNET), lambda b: (b, 0, 0)),
            pl.BlockSpec((nb, seq, D_POOL), lambda b: (b, 0, 0)),
            pl.BlockSpec((seq, 2 * seq), full2),
            pl.BlockSpec((D_FNET, D_FNET), full2),
            pl.BlockSpec((D_FNET, D_FNET), full2),
            pl.BlockSpec((D_FNET, D_FNET), full2),
            pl.BlockSpec((len(POOL_WINDOWS), seq, seq), lambda b: (0, 0, 0)),
            pl.BlockSpec((seq, D_POOL), full2),
            pl.BlockSpec((D_POOL, D_POOL), full2),
            pl.BlockSpec((1, D_POOL), full2),
        ],
        out_specs=(pl.BlockSpec((nb, seq, D_FNET), lambda b: (b, 0, 0)),
                   pl.BlockSpec((nb, seq, D_POOL), lambda b: (b, 0, 0))),
        scratch_shapes=[pltpu.VMEM((D_FNET, 2 * D_FNET), BF16),
                        pltpu.VMEM((seq, 2 * seq), BF16)],
        compiler_params=pltpu.CompilerParams(
            dimension_semantics=("arbitrary",), vmem_limit_bytes=VMEM_LIMIT),
        name="fpool",
    )(pf, pp, cs, c64, s64, wf_bd, band, inv, wp_bd, pool_scale.reshape(1, D_POOL))


def _window_matrix(n, w):
    t = np.arange(n)
    lo = np.clip(t - w // 2, 0, n)
    hi = np.clip(t + w - w // 2, 0, n)
    s = np.arange(n)
    return ((s[None, :] >= lo[:, None]) & (s[None, :] < hi[:, None])).astype(np.float64)


def _mixer_constants(seq, grid):
    n = np.arange(seq)
    ang = 2.0 * np.pi * ((n[:, None] * n[None, :]) % seq) / seq
    scale = 1.0 / math.sqrt(seq * FNET_GW)
    cs = np.concatenate([np.cos(ang), np.sin(ang)], axis=1) * scale
    m = np.arange(FNET_GW)
    ang64 = 2.0 * np.pi * ((m[:, None] * m[None, :]) % FNET_GW) / FNET_GW
    groups = D_FNET // FNET_GW
    c64 = np.kron(np.eye(groups), np.cos(ang64))
    s64 = np.kron(np.eye(groups), np.sin(ang64))
    bands = []
    invs = []
    for w in POOL_WINDOWS:
        if grid:
            rows = seq // GRID_W
            band = np.kron(_window_matrix(rows, w), _window_matrix(GRID_W, w))
        else:
            band = _window_matrix(seq, w)
        bands.append(band)
        invs.append(np.repeat((1.0 / band.sum(axis=1))[:, None], POOL_GW, axis=1))
    band = jnp.asarray(np.stack(bands), dtype=BF16)
    inv = jnp.asarray(np.concatenate(invs, axis=1), dtype=F32)
    return (jnp.asarray(cs, dtype=F32), jnp.asarray(c64, dtype=F32), jnp.asarray(s64, dtype=F32),
            band, inv)


def _block_diag(w):
    g, n, _ = w.shape
    return (w[:, :, None, :] * jnp.eye(g, dtype=w.dtype)[:, None, :, None]).reshape(g * n, g * n)


def _post_kernel(final, x_ref, ya_ref, yb_ref, yc_ref, mod_ref, wo_ref, g2_ref, w1_ref, w2_ref,
                 fg_ref, o_ref):
    mod = mod_ref[0]
    gate1 = mod[:, 2 * D_MODEL:3 * D_MODEL]
    shift2 = mod[:, 3 * D_MODEL:4 * D_MODEL]
    scale2 = mod[:, 4 * D_MODEL:5 * D_MODEL]
    gate2 = mod[:, 5 * D_MODEL:]
    mix = (jnp.dot(ya_ref[...].astype(BF16), wo_ref[0:D_RWKV, :], preferred_element_type=F32)
           + jnp.dot(yb_ref[...].astype(BF16), wo_ref[D_RWKV:D_RWKV + D_FNET, :],
                     preferred_element_type=F32)
           + jnp.dot(yc_ref[...].astype(BF16), wo_ref[D_RWKV + D_FNET:, :],
                     preferred_element_type=F32))
    x1 = x_ref[...] + gate1 * mix
    h2 = (_rms(x1, g2_ref[...]) * (1.0 + scale2) + shift2).astype(BF16)
    tf = 1024
    acc = jnp.zeros_like(x1)
    for j in range(D_FF // tf):
        ff = jnp.dot(h2, w1_ref[:, j * tf:(j + 1) * tf], preferred_element_type=F32)
        ff = jnp.square(jnp.maximum(ff, 0.0))
        acc = acc + jnp.dot(ff.astype(BF16), w2_ref[j * tf:(j + 1) * tf, :],
                            preferred_element_type=F32)
    x2 = x1 + gate2 * acc
    if final:
        x2 = _rms(x2, fg_ref[...])
    o_ref[...] = x2


def _post(x, ya, yb, yc, mod, mod_row, layer, w_out_b, g2, w1_b, w2_b, fg, final, tm):
    t = x.shape[0]
    tok = lambda i: (i, 0)
    full = lambda i: (0, 0)
    return pl.pallas_call(
        functools.partial(_post_kernel, final),
        out_shape=jax.ShapeDtypeStruct((t, D_MODEL), F32),
        grid=(t // tm,),
        in_specs=[
            pl.BlockSpec((tm, D_MODEL), tok),
            pl.BlockSpec((tm, D_RWKV), tok),
            pl.BlockSpec((tm, D_FNET), tok),
            pl.BlockSpec((tm, D_POOL), tok),
            pl.BlockSpec((None, 1, 1, 6 * D_MODEL), lambda i: (layer, mod_row(i), 0, 0)),
            _resident((D_MODEL, D_MODEL), layer),
            _resident((1, D_MODEL), layer),
            _resident((D_MODEL, D_FF), layer),
            _resident((D_FF, D_MODEL), layer),
            pl.BlockSpec((1, D_MODEL), full),
        ],
        out_specs=pl.BlockSpec((tm, D_MODEL), tok),
        compiler_params=pltpu.CompilerParams(
            dimension_semantics=("arbitrary",), vmem_limit_bytes=VMEM_LIMIT),
        name="post",
    )(x, ya, yb, yc, mod, w_out_b, g2.reshape(DEPTH, 1, D_MODEL), w1_b, w2_b, fg.reshape(1, D_MODEL))


def _layer(x, bsz, seq, layer, mod, mod_rows, s0, s_prev, want_state, pairs, per_step, consts, sw, mixw, fg,
           tm):
    (norm1_g, w_in_b, mu_shift, w0, w_up, a0, a_up, g_up, k_k, k_a, r_k, ln_w, ln_b, w_out_b, norm2_g,
     w1_b, w2_b) = sw
    wf_bd, wp_bd, pool_scale = mixw
    tm_pre = max(tm, seq)
    mod_row = mod_rows(tm)
    pa, pf, pp = _pre(x, mod, mod_rows(tm_pre), layer, norm1_g, w_in_b, mu_shift, seq, tm_pre)
    ya, s_fin = _rwkv(pa.reshape(bsz, seq, SHIFT_END), layer, s0, s_prev, want_state, pairs, per_step,
                      w0, w_up, a0, a_up, g_up, k_k, k_a, r_k, ln_w, ln_b)
    yb, yc = _fpool(pf.reshape(bsz, seq, D_FNET), pp.reshape(bsz, seq, D_POOL), consts, wf_bd,
                    wp_bd, pool_scale)
    x = _post(x, ya.reshape(bsz * seq, D_RWKV), yb.reshape(bsz * seq, D_FNET),
              yc.reshape(bsz * seq, D_POOL), mod, mod_row, layer, w_out_b, norm2_g, w1_b, w2_b, fg,
              layer == DEPTH - 1, tm)
    return x, s_fin


def kernel(x_prompt, x_sample, state_wkv, c, c_ctx, w_ada, b_ada, norm1_g, w_in, mu_shift, w0, w_up, a0, a_up, g_up, k_k, k_a, r_k, ln_x_w, ln_x_b, w_fnet, w_pool, pool_scale, w_out, norm2_g, w_ff1, w_ff2, final_norm_g):
    bp, lp, _ = x_prompt.shape
    bs, ls, _ = x_sample.shape
    tm = 512
    n_rows = 8
    cond = jnp.concatenate([c_ctx[None], c, jnp.zeros((n_rows - 1 - bs, D_MODEL), F32)], axis=0)
    mod_all = _ada(cond, w_ada, b_ada)
    consts_p = _mixer_constants(lp, False)
    consts_s = _mixer_constants(ls, True)
    xp = x_prompt.reshape(bp * lp, D_MODEL)
    xs = x_sample.reshape(bs * ls, D_MODEL)
    row_p = lambda tile: (lambda i: 0)
    row_s = lambda tile: (lambda i: 1 + (i * tile) // ls)
    sw = (norm1_g, w_in.astype(BF16), mu_shift, w0, w_up, a0, a_up, g_up, k_k, k_a, r_k, ln_x_w, ln_x_b,
          w_out.astype(BF16), norm2_g, w_ff1.astype(BF16), w_ff2.astype(BF16))
    mod = mod_all.reshape(DEPTH, n_rows, 1, 6 * D_MODEL)
    new_state_wkv = None
    for l in range(DEPTH):
        mixw = (_block_diag(w_fnet[l]), _block_diag(w_pool[l]), pool_scale[l])
        xp, new_state_wkv = _layer(xp, bp, lp, l, mod, row_p, None, new_state_wkv, True, 4, 2, consts_p, sw,
                                   mixw, final_norm_g, tm)
        xs, _ = _layer(xs, bs, ls, l, mod, row_s, state_wkv, None, False, 2, 1, consts_s, sw, mixw,
                       final_norm_g, tm)
    return (xp.reshape(bp, lp, D_MODEL), xs.reshape(bs, ls, D_MODEL), new_state_wkv)
```

```python
import functools
import math

import numpy as np
import jax
import jax.numpy as jnp
from jax import lax
from jax.experimental import pallas as pl
from jax.experimental.pallas import tpu as pltpu

F32 = jnp.float32
BF16 = jnp.bfloat16

D_MODEL = 1024
DEPTH = 2
GRID_W = 64
D_RWKV = 512
HEAD = 64
N_HEADS = 8
LANES = 128
D_FNET = 256
FNET_GW = 64
D_POOL = 256
POOL_GW = 64
POOL_WINDOWS = (2, 4, 8, 16)
LORA = 64
GATE_LORA = 128
SHIFT_END = 3 * D_RWKV + 2 * LORA + 2 * LORA + GATE_LORA
LORA_W = SHIFT_END - 3 * D_RWKV
D_IN = SHIFT_END + D_FNET + D_POOL
D_FF = 4 * D_MODEL
RMS_EPS = 1e-6
GN_EPS = 64e-5
LOG2E = 1.4426950408889634
CHUNK = 64
VMEM_LIMIT = 56 * 1024 * 1024
STREAM_UNITS = 16
STREAM_LAG = 8
PRE_CHUNK = 512
FPOOL_ROWS = 2048


def _dotf(a, b):
    return jnp.dot(a, b, preferred_element_type=F32, precision=lax.Precision.HIGHEST)


def _split(x, n):
    terms = []
    for _ in range(n - 1):
        hi = x.astype(BF16)
        terms.append(hi)
        x = x - hi.astype(F32)
    terms.append(x.astype(BF16))
    return terms


def _dots(a, b, na, nb, nt=False):
    dims = (((1,), (1,)), ((), ())) if nt else (((1,), (0,)), ((), ()))
    ta = _split(a, na)
    tb = _split(b, nb)
    out = None
    for i in range(na):
        for j in range(nb):
            if i + j < max(na, nb):
                t = lax.dot_general(ta[i], tb[j], dims, preferred_element_type=F32)
                out = t if out is None else out + t
    return out


def _mm(a, b):
    return jnp.dot(a.astype(BF16), b.astype(BF16), preferred_element_type=F32)


def _sigmoid(x):
    return 0.5 + 0.5 * jnp.tanh(0.5 * x)


def _rms(x, g):
    ms = jnp.mean(x * x, axis=-1, keepdims=True)
    return x * lax.rsqrt(ms + RMS_EPS) * g


_DONE = object()


def _interleave(streams):
    pending = sorted(streams, key=lambda s: s[0])
    active = []
    step = 0
    while pending or active:
        while pending and pending[0][0] <= step:
            active.append(pending.pop(0)[1])
        for s in list(active):
            if next(s, _DONE) is _DONE:
                active.remove(s)
        step += 1


def _ada_kernel(c_ref, w_ref, b_ref, o_ref):
    c = c_ref[...]
    s = c * _sigmoid(c)
    o_ref[0] = _dots(s, w_ref[0], 2, 2) + b_ref[0]


def _ada(cond, w_ada, b_ada):
    tn = 1536
    nrow = cond.shape[0]
    return pl.pallas_call(
        _ada_kernel,
        out_shape=jax.ShapeDtypeStruct((DEPTH, nrow, 6 * D_MODEL), F32),
        grid=(DEPTH, 6 * D_MODEL // tn),
        in_specs=[
            pl.BlockSpec((nrow, D_MODEL), lambda l, j: (0, 0)),
            pl.BlockSpec((1, D_MODEL, tn), lambda l, j: (l, 0, j)),
            pl.BlockSpec((1, 1, tn), lambda l, j: (l, 0, j)),
        ],
        out_specs=pl.BlockSpec((1, nrow, tn), lambda l, j: (l, 0, j)),
        compiler_params=pltpu.CompilerParams(
            dimension_semantics=("arbitrary", "arbitrary"), vmem_limit_bytes=VMEM_LIMIT),
        name="ada",
    )(cond, w_ada, b_ada.reshape(DEPTH, 1, 6 * D_MODEL))


def _shift_mix(p, mu, period):
    n = p.shape[0]
    pos = lax.broadcasted_iota(jnp.int32, p.shape, 0) % period
    prev = jnp.where(pos == 0, 0.0, pltpu.roll(p, 1, 0))
    nxt = jnp.where(pos == period - 1, 0.0, pltpu.roll(p, n - 1, 0))
    return p * (1.0 - mu) + (0.5 * mu) * (prev + nxt)


def _pre_kernel(seq, x_ref, mod_ref, g_ref, w_ref, mu_ref, oa_ref, of_ref, op_ref):
    mod = mod_ref[0]
    shift = mod[:, 0:D_MODEL]
    scale = mod[:, D_MODEL:2 * D_MODEL]
    h = (_rms(x_ref[...], g_ref[...]) * (1.0 + scale) + shift).astype(BF16)
    outs = ((oa_ref, 0, SHIFT_END), (of_ref, SHIFT_END, SHIFT_END + D_FNET), (op_ref, SHIFT_END + D_FNET, D_IN))
    for c0 in range(0, D_IN, PRE_CHUNK):
        c1 = min(c0 + PRE_CHUNK, D_IN)
        p = jnp.dot(h, w_ref[:, c0:c1], preferred_element_type=F32)
        if c0 < SHIFT_END:
            p = _shift_mix(p, mu_ref[:, c0:c1], seq)
        for ref, o0, o1 in outs:
            lo, hi = max(c0, o0), min(c1, o1)
            if lo < hi:
                ref[:, lo - o0:hi - o0] = p[:, lo - c0:hi - c0]


def _resident(shape, layer):
    zeros = (0,) * len(shape)
    return pl.BlockSpec((None,) + shape, lambda *_: (layer,) + zeros, pipeline_mode=pl.Buffered(1))


def _pre(x, mod, mod_row, layer, g, w_in_b, mu, seq, tm):
    t = x.shape[0]
    assert tm % seq == 0, "a pre tile must hold whole sequences (token shift)"
    return pl.pallas_call(
        functools.partial(_pre_kernel, seq),
        out_shape=(jax.ShapeDtypeStruct((t, SHIFT_END), F32),
                   jax.ShapeDtypeStruct((t, D_FNET), F32),
                   jax.ShapeDtypeStruct((t, D_POOL), F32)),
        grid=(t // tm,),
        in_specs=[
            pl.BlockSpec((tm, D_MODEL), lambda i: (i, 0)),
            pl.BlockSpec((None, 1, 1, 6 * D_MODEL), lambda i: (layer, mod_row(i), 0, 0)),
            _resident((1, D_MODEL), layer),
            _resident((D_MODEL, D_IN), layer),
            _resident((1, D_IN), layer),
        ],
        out_specs=(pl.BlockSpec((tm, SHIFT_END), lambda i: (i, 0)),
                   pl.BlockSpec((tm, D_FNET), lambda i: (i, 0)),
                   pl.BlockSpec((tm, D_POOL), lambda i: (i, 0))),
        compiler_params=pltpu.CompilerParams(
            dimension_semantics=("arbitrary",), vmem_limit_bytes=VMEM_LIMIT),
        name="pre",
    )(x, mod, g.reshape(DEPTH, 1, D_MODEL), w_in_b,
      jnp.pad(mu, ((0, 0), (0, D_IN - SHIFT_END))).reshape(DEPTH, 1, D_IN))


def _rwkv_kernel(seq, pp, nb, has_state, want_state, alias_state, layer, *refs):
    (r_ref, k_ref, v_ref, lo_ref, w0_ref, wup_ref, a0_ref, aup_ref, gup_ref, kk_ref, ka_ref, rk_ref,
     lnw_ref, lnb_ref) = refs[:14]
    n_in = 14 + has_state + alias_state
    s0_ref = refs[14] if has_state else None
    y_ref = refs[n_in]
    sfin_ref = refs[n_in + 1] if want_state else None
    lw_s, kd_s, b_s, r_s, v_s, kn_s, st_s, rb_s, yb_s, gt_s, ht_s = refs[n_in + 1 + want_state:]
    nc = seq // CHUNK
    C = CHUNK
    tiles = [slice(i * LANES, (i + 1) * LANES) for i in range(pp)]

    lane1 = lax.broadcasted_iota(jnp.int32, (1, LANES), 1)
    head0 = lane1 < HEAD
    head1 = jnp.logical_not(head0)
    ri = lax.broadcasted_iota(jnp.int32, (LANES, LANES), 0)
    ci = lax.broadcasted_iota(jnp.int32, (LANES, LANES), 1)
    same_head = (ri // HEAD) == (ci // HEAD)
    seg_ones = same_head.astype(F32)
    eye = (ri == ci).astype(F32)

    def seg_sum(x):
        return jnp.concatenate([_mm(x[:, t], seg_ones) for t in tiles], axis=1)

    n_rows = nb * seq
    r = r_ref[...].reshape(n_rows, pp * LANES)
    k = k_ref[...].reshape(n_rows, pp * LANES)
    v = v_ref[...].reshape(n_rows, pp * LANES)
    lo = lo_ref[...].reshape(n_rows, LORA_W)
    wd = jnp.tanh(lo[:, 0:2 * LORA])
    ad = lo[:, 2 * LORA:4 * LORA]
    gd = _sigmoid(lo[:, 4 * LORA:])
    gate = _mm(gd, gup_ref[...])
    kx = k * kk_ref[...]
    kn = kx * lax.rsqrt(jnp.maximum(seg_sum(kx * kx), 1e-24))
    zeros_up = jnp.zeros((LORA, pp * LANES), F32)
    kd_sum = jnp.zeros_like(k)
    for d in range(2):
        if d == 0:
            wup = jnp.concatenate([wup_ref[0], zeros_up], axis=0)
            aup = jnp.concatenate([aup_ref[0], zeros_up], axis=0)
        else:
            wup = jnp.concatenate([zeros_up, wup_ref[1]], axis=0)
            aup = jnp.concatenate([zeros_up, aup_ref[1]], axis=0)
        zw = w0_ref[d:d + 1, :] + _mm(wd, wup)
        lw2 = (-LOG2E * math.exp(-0.5)) * _sigmoid(zw)
        lw_s[d] = lw2
        a_sig = _sigmoid(a0_ref[d:d + 1, :] + _mm(ad, aup))
        kd = k * (1.0 + (a_sig - 1.0) * ka_ref[...])
        kd_s[d] = kd
        b_s[d] = kn * a_sig
        kn_s[d] = -kn * jnp.exp2(-lw2)
        kd_sum = kd_sum + kd
    r_s[...] = r
    v_s[...] = v
    bonus = seg_sum(r * kd_sum * rk_ref[...]) * v

    if has_state:
        rj = lax.broadcasted_iota(jnp.int32, (LANES, HEAD), 0)
        cj = lax.broadcasted_iota(jnp.int32, (LANES, HEAD), 1)
        place = [(rj == cj + j * HEAD).astype(F32) for j in range(2)]
        keys = [(e, d, pi, j) for e in range(nb) for d in range(2) for pi in range(pp) for j in range(2)]
        t1 = {key: _dots(place[key[3]], s0_ref[key[0], key[1], 2 * key[2] + key[3]], 1, 3, nt=True)
              for key in keys}
        t2 = {key: _dots(t1[key], place[key[3]], 3, 1, nt=True) for key in keys}
        for e in range(nb):
            for d in range(2):
                for pi in range(pp):
                    st_s[d, e * pp + pi] = t2[(e, d, pi, 0)] + t2[(e, d, pi, 1)]
    else:
        st_s[...] = jnp.zeros(st_s.shape, F32)

    tr = lax.broadcasted_iota(jnp.int32, (C, C), 0)
    tc = lax.broadcasted_iota(jnp.int32, (C, C), 1)
    t_p = lax.broadcasted_iota(jnp.int32, (C, LANES), 0)
    s_p = lax.broadcasted_iota(jnp.int32, (C, LANES), 1) % C
    tris = ((tc <= tr).astype(F32), (tc >= tr).astype(F32))
    strict = (s_p < t_p, s_p > t_p)
    inclusive = (s_p <= t_p, s_p >= t_p)
    eye_p = (s_p == t_p).astype(F32)
    txs = t_p ^ s_p
    lvl_map = sum(((txs >= (1 << bit)).astype(jnp.int32) for bit in range(1, 6)),
                  jnp.where(txs == 0, -1, 0))
    zero_b = jnp.zeros((C, LANES), BF16)

    def bd(x):
        return jnp.concatenate([jnp.where(head0, x, zero_b), jnp.where(head1, x, zero_b)], axis=0)

    def bdx(x):
        return jnp.concatenate([jnp.where(head1, x, zero_b), jnp.where(head0, x, zero_b)], axis=0)

    def mm(a, b):
        return jnp.dot(a, b, preferred_element_type=F32)

    def rows(c):
        return pl.ds(c * C, C)

    def algebra(units):
        u = []
        for c, pi, d in units:
            sl, ln = rows(c), tiles[pi]
            lw = lw_s[d, sl, ln]
            u.append(dict(c=c, pi=pi, d=d, sl=sl, ln=ln, lw=lw, cl=_dots(tris[d], lw, 1, 2)))
        yield
        for q in u:
            d, sl, ln, cl = q["d"], q["sl"], q["ln"], q["cl"]
            tot = cl[C - 1:C, :] if d == 0 else cl[0:1, :]
            e_in = jnp.exp2(cl)
            e_inv = jnp.exp2(-cl)
            p_c = jnp.exp2(tot)
            q["at"] = (kn_s[d, sl, ln] * e_in).astype(BF16)
            q["rt"] = r_s[sl, ln] * e_in
            bt = b_s[d, sl, ln] * e_inv
            kt = kd_s[d, sl, ln] * e_inv
            q["lhs_t"] = jnp.concatenate([bt * p_c, kt * p_c], axis=0).T.astype(BF16)
            bt = bt.astype(BF16)
            kt = kt.astype(BF16)
            q["rhs_nt"] = jnp.concatenate([jnp.where(head0, bt, zero_b), jnp.where(head1, bt, zero_b),
                                           jnp.where(head1, kt, zero_b), jnp.where(head0, kt, zero_b)], axis=0)
            q["p_c"] = p_c
            q["vc"] = v_s[sl, ln].astype(BF16)
            yield
        for q in u:
            lhs = jnp.concatenate([q["at"], q["rt"].astype(BF16)], axis=0)
            q["sc"] = lax.dot_general(lhs, q["rhs_nt"], (((1,), (1,)), ((), ())), preferred_element_type=F32)
        yield
        for q in u:
            d, sc = q["d"], q["sc"]
            lab = jnp.where(strict[d], sc[:C, :LANES], 0.0)
            lakx = jnp.where(strict[d], sc[:C, LANES:], 0.0)
            q["mr"] = jnp.concatenate([jnp.where(inclusive[d], sc[C:, :LANES], 0.0),
                                       jnp.where(inclusive[d], sc[C:, LANES:], 0.0)], axis=1).astype(BF16)
            q["xv"] = mm(lakx.astype(BF16), bdx(q["vc"]))
            q["t"] = eye_p + jnp.where(lvl_map == 0, lab, 0.0)
            q["lab"] = lab
        yield
        for lvl in range(1, 6):
            for q in u:
                q["tb"] = q["t"].astype(BF16)
                q["w"] = mm(jnp.where(lvl_map == lvl, q["lab"], 0.0).astype(BF16), bd(q["tb"]))
            yield
            for q in u:
                q["t"] = q["t"] + mm(q["tb"], bd(q["w"].astype(BF16)))
            yield
        for q in u:
            x = mm(q["t"].astype(BF16),
                   jnp.concatenate([bd(q["at"]), bd(q["xv"].astype(BF16))], axis=1))
            q["ab"] = x[:, :LANES].astype(BF16)
            q["ub"] = x[:, LANES:].astype(BF16)
        yield
        for q in u:
            rhs = jnp.concatenate([jnp.concatenate([bd(q["ab"]), bd(q["ub"])], axis=1),
                                   jnp.concatenate([jnp.concatenate([zero_b, zero_b], axis=0), bdx(q["vc"])],
                                                   axis=1)], axis=0)
            q["o2"] = mm(q["mr"], rhs)
            rhs2 = jnp.concatenate([jnp.concatenate([q["ab"], q["ub"]], axis=1),
                                    jnp.concatenate([zero_b, q["vc"]], axis=1)], axis=0)
            q["gh"] = mm(q["lhs_t"], rhs2)
        yield
        for q in u:
            c, pi, d, sl, ln, o2, gh = q["c"], q["pi"], q["d"], q["sl"], q["ln"], q["o2"], q["gh"]
            rb_s[d, sl, ln] = (q["rt"] + o2[:, :LANES]).astype(BF16)
            yb_s[d, sl, ln] = o2[:, LANES:]
            gt_s[d, pi, c] = (jnp.where(same_head, gh[:, :LANES], 0.0) + eye * q["p_c"]).astype(BF16)
            ht_s[d, pi, c] = jnp.where(same_head, gh[:, LANES:], 0.0)
        yield

    def state_chain(e, g, first, count):
        while not (("algebra", e, g) in issued and (g == 0 or ("chain", e, g - 1) in issued)):
            yield
        for i in range(first, first + count):
            items = []
            for pi in range(pp):
                for d in range(2):
                    c = i if d == 0 else nc - 1 - i
                    cg, ln = e * nc + c, tiles[pi]
                    lhs = jnp.concatenate([rb_s[d, rows(cg), ln], gt_s[d, pi, cg]], axis=0)
                    items.append((pi, d, c, cg, ln, mm(lhs, st_s[d, e * pp + pi].astype(BF16))))
            yield
            for pi, d, c, cg, ln, o in items:
                y_ref[e, rows(c), ln] += o[:C] + yb_s[d, rows(cg), ln]
                st_s[d, e * pp + pi] = o[C:] + ht_s[d, pi, cg]
            yield
        issued.add(("chain", e, g))

    y_ref[...] = jnp.zeros(y_ref.shape, F32)

    per = max(1, STREAM_UNITS // (2 * pp))
    issued = set()

    def tracked(e, g, units):
        yield from algebra(units)
        issued.add(("algebra", e, g))

    streams = []
    for e in range(nb):
        for grp in range(nc // per):
            units = [(e * nc + (i if d == 0 else nc - 1 - i), pi, d)
                     for i in range(grp * per, (grp + 1) * per) for pi in range(pp) for d in range(2)]
            start = len(streams) // 2 * STREAM_LAG
            streams.append((start, tracked(e, grp, units)))
            streams.append((start, state_chain(e, grp, grp * per, per)))
    _interleave(streams)

    if want_state:
        for e in range(nb):
            if alias_state:
                out = sfin_ref.at[e]
            else:
                out = sfin_ref.at[e, layer]
                for other in range(DEPTH):
                    if other != layer:
                        sfin_ref[e, other] = jnp.zeros(sfin_ref.shape[2:], F32)
            for d in range(2):
                for pi in range(pp):
                    stt = st_s[d, e * pp + pi].T
                    out[d, 2 * pi] = stt[:HEAD, :HEAD]
                    out[d, 2 * pi + 1] = pltpu.roll(stt[HEAD:], HEAD, 1)[:, :HEAD]

    y = y_ref[...].reshape(n_rows, pp * LANES)
    yc = y - seg_sum(y) * (1.0 / HEAD)
    var = seg_sum(yc * yc) * (1.0 / HEAD)
    yn = yc * lax.rsqrt(var + GN_EPS) * lnw_ref[...] + lnb_ref[...]
    y_ref[...] = ((yn + bonus) * gate).reshape(y_ref.shape)


def _rwkv(pa, layer, s0, s_prev, want_state, pp, nb, w0, w_up, a0, a_up, g_up, k_k, k_a, r_k, ln_w, ln_b):
    bsz, seq, _ = pa.shape
    has_state = s0 is not None
    alias_state = want_state and s_prev is not None
    width = pp * LANES
    nblk = D_RWKV // width
    nc = seq // CHUNK
    rows = nb * seq
    row = lambda a: a.reshape(DEPTH, 1, D_RWKV)
    col = lambda o: (lambda b, p: (b, 0, o * nblk + p))
    vec = lambda b, p: (layer, 0, p)
    lora_blk = 3 * D_RWKV // LORA_W
    in_specs = [
        pl.BlockSpec((nb, seq, width), col(0)),
        pl.BlockSpec((nb, seq, width), col(1)),
        pl.BlockSpec((nb, seq, width), col(2)),
        pl.BlockSpec((nb, seq, LORA_W), lambda b, p: (b, 0, lora_blk)),
        pl.BlockSpec((None, 2, width), vec),
        pl.BlockSpec((None, 2, LORA, width), lambda b, p: (layer, 0, 0, p)),
        pl.BlockSpec((None, 2, width), vec),
        pl.BlockSpec((None, 2, LORA, width), lambda b, p: (layer, 0, 0, p)),
        pl.BlockSpec((None, GATE_LORA, width), vec),
        pl.BlockSpec((None, 1, width), vec),
        pl.BlockSpec((None, 1, width), vec),
        pl.BlockSpec((None, 1, width), vec),
        pl.BlockSpec((None, 1, width), vec),
        pl.BlockSpec((None, 1, width), vec),
    ]
    args = [pa, pa, pa, pa, w0, w_up, a0, a_up, g_up, row(k_k), row(k_a), row(r_k), row(ln_w), row(ln_b)]
    state_spec = pl.BlockSpec((nb, None, 2, 2 * pp, HEAD, HEAD), lambda b, p: (b, layer, 0, p, 0, 0))
    if has_state:
        in_specs.append(state_spec)
        args.append(s0)
    aliases = {}
    if alias_state:
        aliases[len(args)] = 1
        in_specs.append(pl.BlockSpec(memory_space=pl.ANY))
        args.append(s_prev)
    out_shape = [jax.ShapeDtypeStruct((bsz, seq, D_RWKV), F32)]
    out_specs = [pl.BlockSpec((nb, seq, width), lambda b, p: (b, 0, p))]
    if want_state:
        out_shape.append(jax.ShapeDtypeStruct((bsz, DEPTH, 2, N_HEADS, HEAD, HEAD), F32))
        if alias_state:
            out_specs.append(state_spec)
        else:
            out_specs.append(pl.BlockSpec((nb, DEPTH, 2, 2 * pp, HEAD, HEAD), lambda b, p: (b, 0, 0, p, 0, 0)))
    outs = pl.pallas_call(
        functools.partial(_rwkv_kernel, seq, pp, nb, has_state, want_state, alias_state, layer),
        out_shape=tuple(out_shape),
        grid=(bsz // nb, nblk),
        in_specs=in_specs,
        out_specs=tuple(out_specs),
        input_output_aliases=aliases,
        scratch_shapes=[
            pltpu.VMEM((2, rows, width), F32),
            pltpu.VMEM((2, rows, width), F32),
            pltpu.VMEM((2, rows, width), F32),
            pltpu.VMEM((rows, width), F32),
            pltpu.VMEM((rows, width), F32),
            pltpu.VMEM((2, rows, width), F32),
            pltpu.VMEM((2, nb * pp, LANES, LANES), F32),
            pltpu.VMEM((2, rows, width), BF16),
            pltpu.VMEM((2, rows, width), F32),
            pltpu.VMEM((2, pp, nb * nc, LANES, LANES), BF16),
            pltpu.VMEM((2, pp, nb * nc, LANES, LANES), F32),
        ],
        compiler_params=pltpu.CompilerParams(
            dimension_semantics=("arbitrary", "arbitrary"), vmem_limit_bytes=VMEM_LIMIT),
        name="rwkv",
    )(*args)
    return (outs[0], outs[1]) if want_state else (outs[0], None)


def _fpool_kernel(seq, nb, f_ref, p_ref, cs_ref, c64_ref, s64_ref, wf_ref, band_ref, inv_ref, wp_ref,
                  ps_ref, yb_ref, yc_ref, cw_s, cs_s):
    @pl.when(pl.program_id(0) == 0)
    def _():
        wf = wf_ref[...]
        cw_s[:, :D_FNET] = _dotf(c64_ref[...], wf).astype(BF16)
        cw_s[:, D_FNET:] = (-_dotf(s64_ref[...], wf)).astype(BF16)
        cs_s[...] = cs_ref[...].astype(BF16)

    f = f_ref[...].reshape(nb * seq, D_FNET)
    g = jnp.dot(f.astype(BF16), cw_s[...], preferred_element_type=F32).astype(BF16)
    stacked = jnp.concatenate(
        [jnp.concatenate([g[i * seq:(i + 1) * seq, :D_FNET], g[i * seq:(i + 1) * seq, D_FNET:]], axis=0)
         for i in range(nb)], axis=1)
    yb = jnp.dot(cs_s[...], stacked, preferred_element_type=F32)
    for i in range(nb):
        yb_ref[i] = yb[:, i * D_FNET:(i + 1) * D_FNET]

    p = jnp.concatenate([p_ref[i] for i in range(nb)], axis=1)
    p_b = p.astype(BF16)
    group = (lax.broadcasted_iota(jnp.int32, p.shape, 1) % D_POOL) // POOL_GW
    d = jnp.zeros(p.shape, F32)
    for i in range(len(POOL_WINDOWS)):
        s = jnp.dot(band_ref[i], p_b, preferred_element_type=F32)
        d = jnp.where(group == i, s, d)
    d = d * jnp.concatenate([inv_ref[...]] * nb, axis=1) - p
    d = jnp.concatenate([d[:, i * D_POOL:(i + 1) * D_POOL] for i in range(nb)], axis=0)
    yc_ref[...] = (_mm(d, wp_ref[...]) * ps_ref[...]).reshape(nb, seq, D_POOL)


def _fpool(pf, pp, consts, wf_bd, wp_bd, pool_scale):
    bsz, seq, _ = pf.shape
    cs, c64, s64, band, inv = consts
    nb = max(1, min(bsz, FPOOL_ROWS // seq))
    full2 = lambda b: (0, 0)
    return pl.pallas_call(
        functools.partial(_fpool_kernel, seq, nb),
        out_shape=(jax.ShapeDtypeStruct((bsz, seq, D_FNET), F32),
                   jax.ShapeDtypeStruct((bsz, seq, D_POOL), F32)),
        grid=(bsz // nb,),
        in_specs=[
            pl.BlockSpec((nb, seq, D_FNET), lambda b: (b, 0, 0)),
            pl.BlockSpec((nb, seq, D_POOL), lambda b: (b, 0, 0)),
            pl.BlockSpec((seq, 2 * seq), full2),
            pl.BlockSpec((D_FNET, D_FNET), full2),
            pl.BlockSpec((D_FNET, D_FNET), full2),
            pl.BlockSpec((D_FNET, D_FNET), full2),
            pl.BlockSpec((len(POOL_WINDOWS), seq, seq), lambda b: (0, 0, 0)),
            pl.BlockSpec((seq, D_POOL), full2),
            pl.BlockSpec((D_POOL, D_POOL), full2),
            pl.BlockSpec((1, D_POOL), full2),
        ],
        out_specs=(pl.BlockSpec((nb, seq, D_FNET), lambda b: (b, 0, 0)),
                   pl.BlockSpec((nb, seq, D_POOL), lambda b: (b, 0, 0))),
        scratch_shapes=[pltpu.VMEM((D_FNET, 2 * D_FNET), BF16),
                        pltpu.VMEM((seq, 2 * seq), BF16)],
        compiler_params=pltpu.CompilerParams(
            dimension_semantics=("arbitrary",), vmem_limit_bytes=VMEM_LIMIT),
        name="fpool",
    )(pf, pp, cs, c64, s64, wf_bd, band, inv, wp_bd, pool_scale.reshape(1, D_POOL))


def _window_matrix(n, w):
    t = np.arange(n)
    lo = np.clip(t - w // 2, 0, n)
    hi = np.clip(t + w - w // 2, 0, n)
    s = np.arange(n)
    return ((s[None, :] >= lo[:, None]) & (s[None, :] < hi[:, None])).astype(np.float64)


def _mixer_constants(seq, grid):
    n = np.arange(seq)
    ang = 2.0 * np.pi * ((n[:, None] * n[None, :]) % seq) / seq
    scale = 1.0 / math.sqrt(seq * FNET_GW)
    cs = np.concatenate([np.cos(ang), np.sin(ang)], axis=1) * scale
    m = np.arange(FNET_GW)
    ang64 = 2.0 * np.pi * ((m[:, None] * m[None, :]) % FNET_GW) / FNET_GW
    groups = D_FNET // FNET_GW
    c64 = np.kron(np.eye(groups), np.cos(ang64))
    s64 = np.kron(np.eye(groups), np.sin(ang64))
    bands = []
    invs = []
    for w in POOL_WINDOWS:
        if grid:
            rows = seq // GRID_W
            band = np.kron(_window_matrix(rows, w), _window_matrix(GRID_W, w))
        else:
            band = _window_matrix(seq, w)
        bands.append(band)
        invs.append(np.repeat((1.0 / band.sum(axis=1))[:, None], POOL_GW, axis=1))
    band = jnp.asarray(np.stack(bands), dtype=BF16)
    inv = jnp.asarray(np.concatenate(invs, axis=1), dtype=F32)
    return (jnp.asarray(cs, dtype=F32), jnp.asarray(c64, dtype=F32), jnp.asarray(s64, dtype=F32),
            band, inv)


def _block_diag(w):
    g, n, _ = w.shape
    return (w[:, :, None, :] * jnp.eye(g, dtype=w.dtype)[:, None, :, None]).reshape(g * n, g * n)


def _post_kernel(final, x_ref, ya_ref, yb_ref, yc_ref, mod_ref, wo_ref, g2_ref, w1_ref, w2_ref,
                 fg_ref, o_ref):
    mod = mod_ref[0]
    gate1 = mod[:, 2 * D_MODEL:3 * D_MODEL]
    shift2 = mod[:, 3 * D_MODEL:4 * D_MODEL]
    scale2 = mod[:, 4 * D_MODEL:5 * D_MODEL]
    gate2 = mod[:, 5 * D_MODEL:]
    mix = (jnp.dot(ya_ref[...].astype(BF16), wo_ref[0:D_RWKV, :], preferred_element_type=F32)
           + jnp.dot(yb_ref[...].astype(BF16), wo_ref[D_RWKV:D_RWKV + D_FNET, :],
                     preferred_element_type=F32)
           + jnp.dot(yc_ref[...].astype(BF16), wo_ref[D_RWKV + D_FNET:, :],
                     preferred_element_type=F32))
    x1 = x_ref[...] + gate1 * mix
    h2 = (_rms(x1, g2_ref[...]) * (1.0 + scale2) + shift2).astype(BF16)
    tf = 1024
    acc = jnp.zeros_like(x1)
    for j in range(D_FF // tf):
        ff = jnp.dot(h2, w1_ref[:, j * tf:(j + 1) * tf], preferred_element_type=F32)
        ff = jnp.square(jnp.maximum(ff, 0.0))
        acc = acc + jnp.dot(ff.astype(BF16), w2_ref[j * tf:(j + 1) * tf, :],
                            preferred_element_type=F32)
    x2 = x1 + gate2 * acc
    if final:
        x2 = _rms(x2, fg_ref[...])
    o_ref[...] = x2


def _post(x, ya, yb, yc, mod, mod_row, layer, w_out_b, g2, w1_b, w2_b, fg, final, tm):
    t = x.shape[0]
    tok = lambda i: (i, 0)
    full = lambda i: (0, 0)
    return pl.pallas_call(
        functools.partial(_post_kernel, final),
        out_shape=jax.ShapeDtypeStruct((t, D_MODEL), F32),
        grid=(t // tm,),
        in_specs=[
            pl.BlockSpec((tm, D_MODEL), tok),
            pl.BlockSpec((tm, D_RWKV), tok),
            pl.BlockSpec((tm, D_FNET), tok),
            pl.BlockSpec((tm, D_POOL), tok),
            pl.BlockSpec((None, 1, 1, 6 * D_MODEL), lambda i: (layer, mod_row(i), 0, 0)),
            _resident((D_MODEL, D_MODEL), layer),
            _resident((1, D_MODEL), layer),
            _resident((D_MODEL, D_FF), layer),
            _resident((D_FF, D_MODEL), layer),
            pl.BlockSpec((1, D_MODEL), full),
        ],
        out_specs=pl.BlockSpec((tm, D_MODEL), tok),
        compiler_params=pltpu.CompilerParams(
            dimension_semantics=("arbitrary",), vmem_limit_bytes=VMEM_LIMIT),
        name="post",
    )(x, ya, yb, yc, mod, w_out_b, g2.reshape(DEPTH, 1, D_MODEL), w1_b, w2_b, fg.reshape(1, D_MODEL))


def _layer(x, bsz, seq, layer, mod, mod_rows, s0, s_prev, want_state, pairs, per_step, consts, sw, mixw, fg,
           tm):
    (norm1_g, w_in_b, mu_shift, w0, w_up, a0, a_up, g_up, k_k, k_a, r_k, ln_w, ln_b, w_out_b, norm2_g,
     w1_b, w2_b) = sw
    wf_bd, wp_bd, pool_scale = mixw
    tm_pre = max(tm, seq)
    mod_row = mod_rows(tm)
    pa, pf, pp = _pre(x, mod, mod_rows(tm_pre), layer, norm1_g, w_in_b, mu_shift, seq, tm_pre)
    ya, s_fin = _rwkv(pa.reshape(bsz, seq, SHIFT_END), layer, s0, s_prev, want_state, pairs, per_step,
                      w0, w_up, a0, a_up, g_up, k_k, k_a, r_k, ln_w, ln_b)
    yb, yc = _fpool(pf.reshape(bsz, seq, D_FNET), pp.reshape(bsz, seq, D_POOL), consts, wf_bd,
                    wp_bd, pool_scale)
    x = _post(x, ya.reshape(bsz * seq, D_RWKV), yb.reshape(bsz * seq, D_FNET),
              yc.reshape(bsz * seq, D_POOL), mod, mod_row, layer, w_out_b, norm2_g, w1_b, w2_b, fg,
              layer == DEPTH - 1, tm)
    return x, s_fin


def kernel(x_prompt, x_sample, state_wkv, c, c_ctx, w_ada, b_ada, norm1_g, w_in, mu_shift, w0, w_up, a0, a_up, g_up, k_k, k_a, r_k, ln_x_w, ln_x_b, w_fnet, w_pool, pool_scale, w_out, norm2_g, w_ff1, w_ff2, final_norm_g):
    bp, lp, _ = x_prompt.shape
    bs, ls, _ = x_sample.shape
    tm = 512
    n_rows = 8
    cond = jnp.concatenate([c_ctx[None], c, jnp.zeros((n_rows - 1 - bs, D_MODEL), F32)], axis=0)
    mod_all = _ada(cond, w_ada, b_ada)
    consts_p = _mixer_constants(lp, False)
    consts_s = _mixer_constants(ls, True)
    xp = x_prompt.reshape(bp * lp, D_MODEL)
    xs = x_sample.reshape(bs * ls, D_MODEL)
    row_p = lambda tile: (lambda i: 0)
    row_s = lambda tile: (lambda i: 1 + (i * tile) // ls)
    sw = (norm1_g, w_in.astype(BF16), mu_shift, w0, w_up, a0, a_up, g_up, k_k, k_a, r_k, ln_x_w, ln_x_b,
          w_out.astype(BF16), norm2_g, w_ff1.astype(BF16), w_ff2.astype(BF16))
    mod = mod_all.reshape(DEPTH, n_rows, 1, 6 * D_MODEL)
    new_state_wkv = None
    for l in range(DEPTH):
        mixw = (_block_diag(w_fnet[l]), _block_diag(w_pool[l]), pool_scale[l])
        xp, new_state_wkv = _layer(xp, bp, lp, l, mod, row_p, None, new_state_wkv, True, 4, 2, consts_p, sw,
                                   mixw, final_norm_g, tm)
        xs, _ = _layer(xs, bs, ls, l, mod, row_s, state_wkv, None, False, 2, 1, consts_s, sw, mixw,
                       final_norm_g, tm)
    return (xp.reshape(bp, lp, D_MODEL), xs.reshape(bs, ls, D_MODEL), new_state_wkv)
```
